```python
import jax, jax.numpy as jnp
from jax import lax
import numpy as np

D_MODEL = 1024
BATCH = 32
SEQ = 2048
DEPTH = 2

GRID_W = 64
CTX_LEN = 256
CONV_DIM = 512
MLSTM_DIM = D_MODEL - CONV_DIM
MLSTM_HEADS = 4
MLSTM_HEAD_DIM = MLSTM_DIM // MLSTM_HEADS
CHUNK = 128
D_FF = 2816
N_EXPERTS = 8
TOP_K = 2
ROUTE_BLOCK = 256
N_DENSE = (DEPTH + 1) // 2
N_MOE = DEPTH // 2
EPS = 1e-6
Q0 = 3 * CONV_DIM
O0 = Q0 + 3 * MLSTM_DIM + 4 * MLSTM_HEADS
P_IN = O0 + MLSTM_DIM

kernel_name = "hybrid_conv_mlstm_prefix_dit"


def rmsnorm(x, g):
    xf = x.astype(jnp.float32)
    y = xf * lax.rsqrt(jnp.mean(xf * xf, axis=-1, keepdims=True) + EPS)
    return (y * g.astype(jnp.float32)).astype(x.dtype)


def modulate(u, shift, scale):
    return u * (1 + scale) + shift


def conv3(z, w):
    L = z.shape[-2]
    zp = jnp.pad(z, [(0, 0)] * (z.ndim - 2) + [(1, 1), (0, 0)])
    return zp[..., :L, :] * w[0] + zp[..., 1:L + 1, :] * w[1] + zp[..., 2:, :] * w[2]


def short_conv_mixer(u, w_cols, cw, rows):
    b, t, _ = u.shape
    gb, gc, hh = jnp.split(u @ w_cols, 3, axis=-1)
    y = gc * hh
    if rows is None:
        y = conv3(y, cw)
    else:
        y = conv3(y.reshape(b, rows, GRID_W, CONV_DIM), cw).reshape(b, t, CONV_DIM)
    return gb * y


def to_heads(z):
    b, t, _ = z.shape
    return z.reshape(b, t, MLSTM_HEADS, MLSTM_HEAD_DIM).transpose(0, 2, 1, 3).astype(jnp.float32)


def mlstm_inputs(u, w_cols, bg):
    b, t, _ = u.shape
    q, k, v, g = jnp.split(u @ w_cols, [MLSTM_DIM, 2 * MLSTM_DIM, 3 * MLSTM_DIM], axis=-1)
    g = (g.reshape(b, t, 4, MLSTM_HEADS) + bg).astype(jnp.float32).transpose(0, 2, 3, 1)
    return to_heads(q), to_heads(k) * (MLSTM_HEAD_DIM ** -0.5), to_heads(v), g


def mlstm_scan(q, k, v, log_i, log_f, state, return_h):
    b, h, t, d = q.shape
    nc = t // CHUNK
    lower = jnp.tril(jnp.ones((CHUNK, CHUNK), dtype=bool))

    def blocks(a):
        return jnp.moveaxis(a.reshape((b, h, nc, CHUNK) + a.shape[3:]), 2, 0)

    def step(carry, xs):
        C, n, m = carry
        qc, kc, vc, ic, fc = xs
        bcum = jnp.cumsum(fc, axis=-1)
        bL = bcum[..., -1]
        w_end = bL[..., None] - bcum + ic
        m_new = jnp.maximum(bL + m, jnp.max(w_end, axis=-1))
        dec = jnp.exp(bL + m - m_new)
        wk = jnp.exp(w_end - m_new[..., None])
        C_new = dec[..., None, None] * C + jnp.einsum('bhl,bhld,bhle->bhde', wk, kc, vc)
        n_new = dec[..., None] * n + jnp.einsum('bhl,bhld->bhd', wk, kc)
        if not return_h:
            return (C_new, n_new, m_new), None
        dmat = bcum[..., :, None] - bcum[..., None, :] + ic[..., None, :]
        dmat = jnp.where(lower, dmat, -jnp.inf)
        inter = bcum + m[..., None]
        m_t = jnp.maximum(inter, jnp.max(dmat, axis=-1))
        s = jnp.einsum('bhtd,bhsd->bhts', qc, kc) * jnp.exp(dmat - m_t[..., None])
        w_int = jnp.exp(inter - m_t)
        num = jnp.einsum('bhts,bhsd->bhtd', s, vc) + w_int[..., None] * jnp.einsum('bhtd,bhde->bhte', qc, C)
        den = jnp.sum(s, axis=-1) + w_int * jnp.einsum('bhtd,bhd->bht', qc, n)
        hout = num / jnp.maximum(jnp.abs(den), jnp.exp(-m_t))[..., None]
        return (C_new, n_new, m_new), hout

    state, hs = lax.scan(step, state, (blocks(q), blocks(k), blocks(v), blocks(log_i), blocks(log_f)))
    if return_h:
        hs = jnp.moveaxis(hs, 0, 2).reshape(b, h, t, d)
    return state, hs


def mlstm_bidirectional(lat, ctxs, need_ctx):
    q, k, v, g = lat
    qc, kc, vc, gc = ctxs
    b, h, _, d = q.shape
    out_lat, out_ctx = None, None
    for direction in range(2):
        rev = direction == 1
        orient = (lambda a: jnp.flip(a, axis=2)) if rev else (lambda a: a)
        state0 = (jnp.zeros((b, h, d, d), jnp.float32), jnp.zeros((b, h, d), jnp.float32),
                  jnp.zeros((b, h), jnp.float32))
        st, hc = mlstm_scan(orient(qc), orient(kc), orient(vc), orient(gc[:, 2 * direction]),
                            orient(jax.nn.log_sigmoid(gc[:, 2 * direction + 1])), state0, need_ctx)
        _, hl = mlstm_scan(orient(q), orient(k), orient(v), orient(g[:, 2 * direction]),
                           orient(jax.nn.log_sigmoid(g[:, 2 * direction + 1])), st, True)
        out_lat = orient(hl) if out_lat is None else out_lat + orient(hl)
        if need_ctx:
            out_ctx = orient(hc) if out_ctx is None else out_ctx + orient(hc)
    return out_lat, out_ctx


def mlstm_merge(hh, g_norm, o):
    hn = hh * lax.rsqrt(jnp.mean(hh * hh, axis=-1, keepdims=True) + EPS)
    b, _, t, _ = hn.shape
    hn = hn.transpose(0, 2, 1, 3).reshape(b, t, MLSTM_DIM) * g_norm.astype(jnp.float32)
    return hn.astype(o.dtype) * o


def swiglu(u, w1, w3, w2):
    return (jax.nn.silu(u @ w1) * (u @ w3)) @ w2


def moe_swiglu(xf, router, w1, w3, w2):
    n, dm = xf.shape
    logits = xf.astype(jnp.float32) @ router.astype(jnp.float32)
    top_v, top_i = lax.top_k(logits, TOP_K)
    gates = jax.nn.softmax(top_v, axis=-1).astype(xf.dtype)
    a = n * TOP_K
    flat_e = top_i.reshape(-1)
    flat_t = jnp.repeat(jnp.arange(n, dtype=jnp.int32), TOP_K)
    flat_g = gates.reshape(-1)
    order = jnp.argsort(flat_e)
    se = flat_e[order]
    counts = jnp.bincount(flat_e, length=N_EXPERTS)
    padded = (counts + ROUTE_BLOCK - 1) // ROUTE_BLOCK * ROUTE_BLOCK
    start = jnp.cumsum(counts) - counts
    pend = jnp.cumsum(padded)
    pstart = pend - padded
    dest = pstart[se] + (jnp.arange(a, dtype=jnp.int32) - start[se])
    cap = -(-a // ROUTE_BLOCK) * ROUTE_BLOCK + N_EXPERTS * ROUTE_BLOCK
    tok_buf = jnp.zeros((cap,), jnp.int32).at[dest].set(flat_t[order])
    gate_buf = jnp.zeros((cap,), xf.dtype).at[dest].set(flat_g[order])
    nblk = cap // ROUTE_BLOCK
    blk_e = jnp.minimum(jnp.searchsorted(pend, jnp.arange(nblk) * ROUTE_BLOCK, side='right'), N_EXPERTS - 1)
    xb = xf[tok_buf].reshape(nblk, ROUTE_BLOCK, dm)

    def expert_block(args):
        xblk, e = args
        return swiglu(xblk, w1[e], w3[e], w2[e])

    yb = lax.map(expert_block, (xb, blk_e)).reshape(cap, dm)
    return jnp.zeros_like(xf).at[tok_buf].add(yb * gate_buf[:, None])


def channel_mixer(layer, u, ffn_w1, ffn_w3, ffn_w2, moe_router, moe_w1, moe_w3, moe_w2):
    j = layer // 2
    if layer % 2 == 0:
        return swiglu(u, ffn_w1[j], ffn_w3[j], ffn_w2[j])
    b, t, dm = u.shape
    return moe_swiglu(u.reshape(b * t, dm), moe_router[j], moe_w1[j], moe_w3[j], moe_w2[j]).reshape(b, t, dm)


def setup_inputs(seed: int = 0) -> dict:
    key = jax.random.key(seed)
    ks = jax.random.split(key, 24)
    nrm = lambda k, s, sc: jax.random.normal(k, s, jnp.float32) * sc
    D, H = D_MODEL, MLSTM_HEADS
    ib = nrm(ks[10], (DEPTH, 2, H), 0.1)
    fb = jnp.linspace(3.0, 6.0, H, dtype=jnp.float32)[None, None] + nrm(ks[11], (DEPTH, 2, H), 0.1)
    b_gates = jnp.stack([ib[:, 0], fb[:, 0], ib[:, 1], fb[:, 1]], axis=1)
    return {
        "x": nrm(ks[0], (BATCH, SEQ, D), 1.0),
        "c": nrm(ks[1], (BATCH, D), 1.0),
        "ctx": nrm(ks[2], (BATCH, CTX_LEN, D), 1.0),
        "c_ctx": nrm(ks[3], (D,), 1.0),
        "norm1_g": 1.0 + nrm(ks[4], (DEPTH, D), 0.05),
        "norm2_g": 1.0 + nrm(ks[5], (DEPTH, D), 0.05),
        "w_ada": nrm(ks[6], (DEPTH, D, 6 * D), 0.5 * D ** -0.5),
        "b_ada": nrm(ks[7], (DEPTH, 6 * D), 0.02),
        "w_in": nrm(ks[8], (DEPTH, D, P_IN), D ** -0.5),
        "conv_w": nrm(ks[9], (DEPTH, 3, CONV_DIM), 0.5),
        "b_gates": b_gates,
        "mlstm_norm_g": 1.0 + nrm(ks[12], (DEPTH, MLSTM_DIM), 0.05),
        "w_out": nrm(ks[13], (DEPTH, D, D), D ** -0.5),
        "ffn_w1": nrm(ks[14], (N_DENSE, D, D_FF), D ** -0.5),
        "ffn_w3": nrm(ks[15], (N_DENSE, D, D_FF), D ** -0.5),
        "ffn_w2": nrm(ks[16], (N_DENSE, D_FF, D), D_FF ** -0.5),
        "moe_router": nrm(ks[17], (N_MOE, D, N_EXPERTS), D ** -0.5),
        "moe_w1": nrm(ks[18], (N_MOE, N_EXPERTS, D, D_FF), D ** -0.5),
        "moe_w3": nrm(ks[19], (N_MOE, N_EXPERTS, D, D_FF), D ** -0.5),
        "moe_w2": nrm(ks[20], (N_MOE, N_EXPERTS, D_FF, D), D_FF ** -0.5),
        "final_norm_g": 1.0 + nrm(ks[21], (D,), 0.05),
    }


def reference(x, c, ctx, c_ctx, norm1_g, norm2_g, w_ada, b_ada, w_in, conv_w, b_gates,
              mlstm_norm_g, w_out, ffn_w1, ffn_w3, ffn_w2, moe_router, moe_w1, moe_w3, moe_w2,
              final_norm_g):
    rows = x.shape[1] // GRID_W
    s_lat = jax.nn.silu(c)
    s_ctx = jax.nn.silu(c_ctx)
    h, hc = x, ctx
    for layer in range(DEPTH):
        last = layer == DEPTH - 1
        mod = s_lat @ w_ada[layer] + b_ada[layer]
        mod_c = s_ctx @ w_ada[layer] + b_ada[layer]
        sh1, sc1, g1, sh2, sc2, g2 = jnp.split(mod[:, None, :], 6, axis=-1)
        csh1, csc1, cg1, csh2, csc2, cg2 = jnp.split(mod_c, 6, axis=-1)
        w = w_in[layer]

        u = modulate(rmsnorm(h, norm1_g[layer]), sh1, sc1)
        uc = modulate(rmsnorm(hc, norm1_g[layer]), csh1, csc1)
        lat_in = mlstm_inputs(u, w[:, Q0:O0], b_gates[layer])
        ctx_in = mlstm_inputs(uc, w[:, Q0:O0], b_gates[layer])
        m_lat, m_ctx = mlstm_bidirectional(lat_in, ctx_in, not last)
        y_conv = short_conv_mixer(u, w[:, :Q0], conv_w[layer], rows)
        y_mlstm = mlstm_merge(m_lat, mlstm_norm_g[layer], jax.nn.sigmoid(u @ w[:, O0:]))
        h = h + g1 * (jnp.concatenate([y_conv, y_mlstm], axis=-1) @ w_out[layer])

        u2 = modulate(rmsnorm(h, norm2_g[layer]), sh2, sc2)
        h = h + g2 * channel_mixer(layer, u2, ffn_w1, ffn_w3, ffn_w2, moe_router, moe_w1, moe_w3, moe_w2)

        if not last:
            yc_conv = short_conv_mixer(uc, w[:, :Q0], conv_w[layer], None)
            yc_mlstm = mlstm_merge(m_ctx, mlstm_norm_g[layer], jax.nn.sigmoid(uc @ w[:, O0:]))
            hc = hc + cg1 * (jnp.concatenate([yc_conv, yc_mlstm], axis=-1) @ w_out[layer])
            uc2 = modulate(rmsnorm(hc, norm2_g[layer]), csh2, csc2)
            hc = hc + cg2 * channel_mixer(layer, uc2, ffn_w1, ffn_w3, ffn_w2, moe_router, moe_w1, moe_w3, moe_w2)
    return rmsnorm(h, final_norm_g)
```

```python
import functools

import jax
import jax.numpy as jnp
from jax import lax
from jax.experimental import pallas as pl
from jax.experimental.pallas import tpu as pltpu

GRID_W = 64
CHUNK = 128
EPS = 1e-6
TOP_K = 2
ROW_TILE = 512
MOE_BLOCK = 256
DISPATCH_TILE = 2048
COMBINE_TILE = 256
LANES = 128
SUBLANES = 8
VMEM_LIMIT = 56 * 1024 * 1024

F32 = jnp.float32
BF16 = jnp.bfloat16
HIGHEST = lax.Precision.HIGHEST


def _dot(a, b):
    return jnp.dot(a, b, preferred_element_type=F32)


def _params(*sem):
    return pltpu.CompilerParams(dimension_semantics=sem, vmem_limit_bytes=VMEM_LIMIT)


def _ff_chunks(d_ff, max_chunk=1024):
    out, c0 = [], 0
    while c0 < d_ff:
        c1 = min(c0 + max_chunk, d_ff)
        out.append((c0, c1))
        c0 = c1
    return out


def _ada_kernel(c_ref, w_ref, b_ref, o_ref):
    s = c_ref[...]
    s = s * jax.nn.sigmoid(s)
    o_ref[0] = jnp.dot(s, w_ref[0], precision=HIGHEST, preferred_element_type=F32) + b_ref[0]


def _ada(c_all, w_ada, b_ada):
    depth, d, d6 = w_ada.shape
    rows = c_all.shape[0]
    tn = d6 // 4
    return pl.pallas_call(
        _ada_kernel,
        grid=(depth, d6 // tn),
        in_specs=[
            pl.BlockSpec((rows, d), lambda l, j: (0, 0)),
            pl.BlockSpec((1, d, tn), lambda l, j: (l, 0, j)),
            pl.BlockSpec((1, 1, tn), lambda l, j: (l, 0, j)),
        ],
        out_specs=pl.BlockSpec((1, rows, tn), lambda l, j: (l, 0, j)),
        out_shape=jax.ShapeDtypeStruct((depth, rows, d6), F32),
        compiler_params=_params("parallel", "parallel"),
        name="adaln_mod",
    )(c_all, w_ada, b_ada.reshape(depth, 1, d6))


def _norm_mod(x, g, shift, scale):
    y = x * lax.rsqrt(jnp.mean(x * x, axis=-1, keepdims=True) + EPS)
    return (y * g) * (1.0 + scale) + shift


def _in_kernel(x_ref, mod_ref, g_ref, wq_ref, wk_ref, wv_ref, wg_ref, bg_ref, *rest,
               row_w, full, kscale, n_gate):
    if full:
        wc_ref, wo_ref, cw_ref, q_ref, k_ref, v_ref, gt_ref, yc_ref, og_ref = rest
    else:
        q_ref, k_ref, v_ref, gt_ref = rest
    x = x_ref[0]
    u = _norm_mod(x, g_ref[...], mod_ref[0, 0:1, :], mod_ref[0, 1:2, :]).astype(BF16)
    q_ref[0] = _dot(u, wq_ref[...]).astype(BF16)
    k_ref[0] = (_dot(u, wk_ref[...]) * kscale).astype(BF16)
    v_ref[0] = _dot(u, wv_ref[...]).astype(BF16)
    gt_ref[0] = _dot(u, wg_ref[...])[:, :n_gate] + bg_ref[...]
    if full:
        cd = cw_ref.shape[1]
        c3 = _dot(u, wc_ref[...])
        z = c3[:, cd:2 * cd] * c3[:, 2 * cd:]
        tm = z.shape[0]
        t = lax.broadcasted_iota(jnp.int32, (tm, 1), 0) % row_w
        zprev = jnp.where(t == 0, 0.0, pltpu.roll(z, 1, axis=0))
        znext = jnp.where(t == row_w - 1, 0.0, pltpu.roll(z, tm - 1, axis=0))
        conv = zprev * cw_ref[0:1, :] + z * cw_ref[1:2, :] + znext * cw_ref[2:3, :]
        yc_ref[0] = (c3[:, :cd] * conv).astype(BF16)
        og_ref[0] = jax.nn.sigmoid(_dot(u, wo_ref[...])).astype(BF16)


def _in_proj(h, mod, mod_row, norm_g, wts, row_w, full):
    b, t, d = h.shape
    tm = min(ROW_TILE, t)
    md = wts["wq"].shape[1]
    n_gate = wts["bg"].shape[1]
    head_dim = md // (n_gate // 4)
    const = lambda shape: pl.BlockSpec(shape, lambda i, j: (0,) * len(shape))
    tile = lambda w: pl.BlockSpec((1, tm, w), lambda i, j: (i, j, 0))
    in_specs = [
        tile(d),
        pl.BlockSpec((1, 6, d), (lambda i, j: (i, 0, 0)) if mod_row is None
                     else (lambda i, j: (mod_row, 0, 0))),
        const((1, d)),
        const(wts["wq"].shape), const(wts["wk"].shape), const(wts["wv"].shape),
        const(wts["wg"].shape), const(wts["bg"].shape),
    ]
    args = [h, mod, norm_g, wts["wq"], wts["wk"], wts["wv"], wts["wg"], wts["bg"]]
    out_specs = [tile(md), tile(md), tile(md), tile(n_gate)]
    out_shape = [jax.ShapeDtypeStruct((b, t, md), BF16)] * 3 + [jax.ShapeDtypeStruct((b, t, n_gate), F32)]
    if full:
        cd = wts["cw"].shape[1]
        in_specs += [const(wts["wc"].shape), const(wts["wo"].shape), const(wts["cw"].shape)]
        args += [wts["wc"], wts["wo"], wts["cw"]]
        out_specs += [tile(cd), tile(md)]
        out_shape += [jax.ShapeDtypeStruct((b, t, cd), BF16), jax.ShapeDtypeStruct((b, t, md), BF16)]
    return pl.pallas_call(
        functools.partial(_in_kernel, row_w=row_w, full=full, kscale=head_dim ** -0.5, n_gate=n_gate),
        grid=(b, t // tm),
        in_specs=in_specs,
        out_specs=out_specs,
        out_shape=out_shape,
        compiler_params=_params("parallel", "parallel"),
        name="in_proj",
    )(*args)


def _lane_scan(x, op, ident, reverse):
    n = x.shape[1]
    lane = lax.broadcasted_iota(jnp.int32, x.shape, 1)
    s = 1
    while s < n:
        if reverse:
            shifted = jnp.where(lane < n - s, pltpu.roll(x, n - s, axis=1), ident)
        else:
            shifted = jnp.where(lane >= s, pltpu.roll(x, s, axis=1), ident)
        x = op(x, shifted)
        s *= 2
    return x


def _mlstm_kernel(gr_ref, qc_ref, kc_ref, vc_ref, ql_ref, kl_ref, vl_ref, ogl_ref, gain_ref,
                  *rest, need_ctx):
    if need_ctx:
        ogc_ref, yc_ref, yl_ref, cols_ref, rows_ref, hc_ref, hl_ref = rest
    else:
        yl_ref, cols_ref, rows_ref, hc_ref, hl_ref = rest
        ogc_ref = yc_ref = None
    L = CHUNK
    hd = ql_ref.shape[2]
    ncc = qc_ref.shape[1] // L
    ncl = ql_ref.shape[1] // L
    nct = ncc + ncl

    sub = lax.broadcasted_iota(jnp.int32, (SUBLANES, L), 0)

    def prologue(c, carry):
        g4 = gr_ref[0, 0, c]
        for d in range(2):
            rev = d == 1
            ig = jnp.broadcast_to(g4[2 * d:2 * d + 1, :], (SUBLANES, L))
            lf = jnp.broadcast_to(jax.nn.log_sigmoid(g4[2 * d + 1:2 * d + 2, :]), (SUBLANES, L))
            a = _lane_scan(lf, jnp.add, 0.0, rev)
            r = ig - a
            p = _lane_scan(r, jnp.maximum, -jnp.inf, rev)
            tile = jnp.where(sub == 0, a, jnp.where(sub == 1, r, jnp.where(sub == 2, p, 0.0)))
            rows_ref[d, pl.ds(pl.multiple_of(c * SUBLANES, SUBLANES), SUBLANES), :] = tile
            full_tile = jnp.concatenate([tile, jnp.zeros((L - SUBLANES, L), F32)], axis=0)
            cols_ref[d, pl.ds(pl.multiple_of(c * L, L), L), :] = full_tile.T
        return carry

    lax.fori_loop(0, nct, prologue, 0)

    tpos = lax.broadcasted_iota(jnp.int32, (L, L), 0)
    spos = lax.broadcasted_iota(jnp.int32, (L, L), 1)
    ones_col = jnp.where(lax.broadcasted_iota(jnp.int32, (L, hd), 1) == 0, 1.0, 0.0).astype(BF16)
    gain = gain_ref[...]

    def chunk_step(d, q_ref, k_ref, v_ref, h_ref, og_ref, y_ref, j, g, caug, m, want_h):
        off = pl.multiple_of(j * L, L)
        k = k_ref[0, pl.ds(off, L), :]
        v = v_ref[0, pl.ds(off, L), :]
        vaug = jnp.concatenate([v, ones_col], axis=1)
        ct = cols_ref[d, pl.ds(pl.multiple_of(g * L, L), L), :]
        rt = rows_ref[d, pl.ds(pl.multiple_of(g * SUBLANES, SUBLANES), SUBLANES), :]
        a_col, r_col, p_col = ct[:, 0:1], ct[:, 1:2], ct[:, 2:3]
        last = 0 if d == 1 else L - 1
        b_tot = rt[0:1, last:last + 1]
        p_last = rt[2:3, last:last + 1]
        if want_h:
            q = q_ref[0, pl.ds(off, L), :]
            r_row = rt[1:2, :]
            m_col = jnp.maximum(p_col, m)
            mask = (spos >= tpos) if d == 1 else (spos <= tpos)
            wmat = jnp.where(mask, jnp.exp(r_row - m_col), 0.0)
            s = lax.dot_general(q, k, (((1,), (1,)), ((), ())), preferred_element_type=F32)
            sb = (s * wmat).astype(BF16)
            nd = _dot(sb, vaug) + jnp.exp(m - m_col) * _dot(q, caug.astype(BF16))
            den = jnp.maximum(jnp.abs(nd[:, hd:hd + 1]), jnp.exp(-(a_col + m_col)))
            hout = nd[:, :hd] * (1.0 / den)
            if d == 0:
                h_ref[pl.ds(off, L), :] = hout
            else:
                hh = h_ref[pl.ds(off, L), :] + hout
                hn = hh * lax.rsqrt(jnp.mean(hh * hh, axis=-1, keepdims=True) + EPS)
                og = og_ref[0, pl.ds(off, L), :].astype(F32)
                y_ref[0, pl.ds(off, L), :] = (hn * gain * og).astype(BF16)
        m_last = jnp.maximum(p_last, m)
        kw = (k.astype(F32) * jnp.exp(r_col - m_last)).astype(BF16)
        upd = lax.dot_general(kw, vaug, (((0,), (0,)), ((), ())), preferred_element_type=F32)
        caug = jnp.exp(m - m_last) * caug + upd
        return caug, b_tot + m_last

    for d in range(2):
        state = (jnp.zeros((hd, 2 * hd), F32), jnp.zeros((1, 1), F32))

        def ctx_body(i, st, d=d):
            j = (ncc - 1 - i) if d == 1 else i
            return chunk_step(d, qc_ref, kc_ref, vc_ref, hc_ref, ogc_ref, yc_ref, j, j, st[0], st[1], need_ctx)

        def lat_body(i, st, d=d):
            j = (ncl - 1 - i) if d == 1 else i
            return chunk_step(d, ql_ref, kl_ref, vl_ref, hl_ref, ogl_ref, yl_ref, j, ncc + j, st[0], st[1], True)

        state = lax.fori_loop(0, ncc, ctx_body, state)
        lax.fori_loop(0, ncl, lat_body, state)


def _mlstm(gates_c, gates_l, qkv_c, qkv_l, og_c, og_l, gain, need_ctx):
    b, tc, md = qkv_c[0].shape
    tl = qkv_l[0].shape[1]
    n_gate = gates_c.shape[2]
    heads = n_gate // 4
    hd = md // heads
    L = CHUNK
    nct = (tc + tl) // L
    g = jnp.concatenate([gates_c, gates_l], axis=1).reshape(b, nct, L, 4, heads)
    g = g.transpose(0, 4, 1, 3, 2)
    seq = lambda t: pl.BlockSpec((1, t, hd), lambda i, h: (i, 0, h))
    in_specs = [pl.BlockSpec((1, 1, nct, 4, L), lambda i, h: (i, h, 0, 0, 0)),
                seq(tc), seq(tc), seq(tc), seq(tl), seq(tl), seq(tl), seq(tl),
                pl.BlockSpec((1, hd), lambda i, h: (0, h))]
    args = [g, *qkv_c, *qkv_l, og_l, gain]
    out_specs = [seq(tl)]
    out_shape = [jax.ShapeDtypeStruct((b, tl, md), BF16)]
    if need_ctx:
        in_specs.append(seq(tc))
        args.append(og_c)
        out_specs = [seq(tc)] + out_specs
        out_shape = [jax.ShapeDtypeStruct((b, tc, md), BF16)] + out_shape
    res = pl.pallas_call(
        functools.partial(_mlstm_kernel, need_ctx=need_ctx),
        grid=(b, heads),
        in_specs=in_specs,
        out_specs=out_specs,
        out_shape=out_shape,
        scratch_shapes=[
            pltpu.VMEM((2, nct * L, L), F32),
            pltpu.VMEM((2, nct * SUBLANES, L), F32),
            pltpu.VMEM((tc, hd), F32),
            pltpu.VMEM((tl, hd), F32),
        ],
        compiler_params=_params("parallel", "parallel"),
        name="mlstm_scan",
    )(*args)
    return (res[0], res[1]) if need_ctx else (None, res[0])


def _mix_residual(h_ref, yc_ref, ym_ref, mod_ref, woc_ref, wom_ref):
    mix = _dot(yc_ref[0], woc_ref[...]) + _dot(ym_ref[0], wom_ref[...])
    return h_ref[0] + mod_ref[0, 2:3, :] * mix


def _out_dense_kernel(h_ref, yc_ref, ym_ref, mod_ref, g2_ref, woc_ref, wom_ref, w1_ref, w3_ref, w2_ref,
                      *rest, chunks, final):
    if final:
        gf_ref, o_ref = rest
    else:
        (o_ref,) = rest
    h = _mix_residual(h_ref, yc_ref, ym_ref, mod_ref, woc_ref, wom_ref)
    u = _norm_mod(h, g2_ref[...], mod_ref[0, 3:4, :], mod_ref[0, 4:5, :]).astype(BF16)
    acc = None
    for c0, c1 in chunks:
        a = _dot(u, w1_ref[:, c0:c1])
        hid = (a * jax.nn.sigmoid(a) * _dot(u, w3_ref[:, c0:c1])).astype(BF16)
        part = _dot(hid, w2_ref[c0:c1, :])
        acc = part if acc is None else acc + part
    h = h + mod_ref[0, 5:6, :] * acc
    if final:
        h = h * lax.rsqrt(jnp.mean(h * h, axis=-1, keepdims=True) + EPS) * gf_ref[...]
    o_ref[0] = h


def _out_moe_kernel(h_ref, yc_ref, ym_ref, mod_ref, g2_ref, woc_ref, wom_ref, router_ref,
                    hm_ref, u_ref, ti_ref, tg_ref, *, n_exp):
    h = _mix_residual(h_ref, yc_ref, ym_ref, mod_ref, woc_ref, wom_ref)
    hm_ref[0] = h
    u = _norm_mod(h, g2_ref[...], mod_ref[0, 3:4, :], mod_ref[0, 4:5, :])
    u_ref[0] = u
    logits = jnp.dot(u, router_ref[...], precision=HIGHEST, preferred_element_type=F32)
    lane = lax.broadcasted_iota(jnp.int32, logits.shape, 1)
    logits = jnp.where(lane < n_exp, logits, -jnp.inf)
    v1 = jnp.max(logits, axis=-1, keepdims=True)
    i1 = jnp.min(jnp.where(logits == v1, lane, LANES), axis=-1, keepdims=True)
    rest = jnp.where(lane == i1, -jnp.inf, logits)
    v2 = jnp.max(rest, axis=-1, keepdims=True)
    i2 = jnp.min(jnp.where(rest == v2, lane, LANES), axis=-1, keepdims=True)
    e2 = jnp.exp(v2 - v1)
    inv = 1.0 / (1.0 + e2)
    first = lax.broadcasted_iota(jnp.int32, ti_ref.shape[1:], 1) == 0
    ti_ref[0] = jnp.where(first, i1, i2)
    tg_ref[0] = jnp.where(first, inv, e2 * inv)


def _out_common(h, yc, ym, mod, mod_row, norm_g, wts):
    b, t, d = h.shape
    tm = min(ROW_TILE, t)
    const = lambda shape: pl.BlockSpec(shape, lambda i, j: (0,) * len(shape))
    tile = lambda w: pl.BlockSpec((1, tm, w), lambda i, j: (i, j, 0))
    in_specs = [
        tile(d), tile(yc.shape[2]), tile(ym.shape[2]),
        pl.BlockSpec((1, 6, d), (lambda i, j: (i, 0, 0)) if mod_row is None
                     else (lambda i, j: (mod_row, 0, 0))),
        const((1, d)), const(wts["woc"].shape), const(wts["wom"].shape),
    ]
    args = [h, yc, ym, mod, norm_g, wts["woc"], wts["wom"]]
    return b, t, d, tm, const, tile, in_specs, args


def _out_dense(h, yc, ym, mod, mod_row, norm_g, wts, ffn, final_g):
    b, t, d, tm, const, tile, in_specs, args = _out_common(h, yc, ym, mod, mod_row, norm_g, wts)
    w1, w3, w2 = ffn
    in_specs += [const(w1.shape), const(w3.shape), const(w2.shape)]
    args += [w1, w3, w2]
    if final_g is not None:
        in_specs.append(const((1, d)))
        args.append(final_g)
    return pl.pallas_call(
        functools.partial(_out_dense_kernel, chunks=_ff_chunks(w1.shape[1]), final=final_g is not None),
        grid=(b, t // tm),
        in_specs=in_specs,
        out_specs=tile(d),
        out_shape=jax.ShapeDtypeStruct((b, t, d), F32),
        compiler_params=_params("parallel", "parallel"),
        name="out_dense_ffn",
    )(*args)


def _out_moe(h, yc, ym, mod, mod_row, norm_g, wts, router_pad, n_exp):
    b, t, d, tm, const, tile, in_specs, args = _out_common(h, yc, ym, mod, mod_row, norm_g, wts)
    in_specs.append(const(router_pad.shape))
    args.append(router_pad)
    return pl.pallas_call(
        functools.partial(_out_moe_kernel, n_exp=n_exp),
        grid=(b, t // tm),
        in_specs=in_specs,
        out_specs=[tile(d), tile(d), tile(TOP_K), tile(TOP_K)],
        out_shape=[jax.ShapeDtypeStruct((b, t, d), F32), jax.ShapeDtypeStruct((b, t, d), F32),
                   jax.ShapeDtypeStruct((b, t, TOP_K), jnp.int32), jax.ShapeDtypeStruct((b, t, TOP_K), F32)],
        compiler_params=_params("parallel", "parallel"),
        name="out_router",
    )(*args)


DMA_WINDOW = 16


def _dispatch_kernel(src_ref, dst_ref, u_hbm, z_hbm, xs_hbm, sem, *, n_tok):
    n = src_ref.shape[2]

    def real_copy(s, t):
        return pltpu.make_async_copy(u_hbm.at[pl.ds(s, 1), :], xs_hbm.at[pl.ds(t, 1), :], sem)

    def start(i):
        s = src_ref[0, 0, i]
        t = dst_ref[0, 0, i]

        @pl.when(s < n_tok)
        def _():
            real_copy(s, t).start()

        @pl.when(s >= n_tok)
        def _():
            pltpu.make_async_copy(z_hbm.at[pl.ds(0, 1), :], xs_hbm.at[pl.ds(t, 1), :], sem).start()

    def wait(i):
        real_copy(0, dst_ref[0, 0, i]).wait()

    def body(i, c):
        start(i)

        @pl.when(i >= DMA_WINDOW)
        def _():
            wait(i - DMA_WINDOW)
        return c

    lax.fori_loop(0, n, body, 0)

    def drain(i, c):
        wait(i)
        return c

    lax.fori_loop(n - DMA_WINDOW, n, drain, 0)


def _dispatch(u_flat, src, dst, cap):
    n_tok, d = u_flat.shape
    total = src.shape[0]
    tile = DISPATCH_TILE
    steps = total // tile
    zrow = jnp.zeros((SUBLANES, d), F32)
    smem = lambda: pl.BlockSpec((1, 1, tile), lambda i: (i, 0, 0), memory_space=pltpu.SMEM)
    return pl.pallas_call(
        functools.partial(_dispatch_kernel, n_tok=n_tok),
        grid=(steps,),
        in_specs=[smem(), smem(), pl.BlockSpec(memory_space=pl.ANY), pl.BlockSpec(memory_space=pl.ANY)],
        out_specs=pl.BlockSpec(memory_space=pl.ANY),
        out_shape=jax.ShapeDtypeStruct((cap, d), F32),
        scratch_shapes=[pltpu.SemaphoreType.DMA(())],
        compiler_params=_params("arbitrary"),
        name="moe_dispatch",
    )(src.reshape(steps, 1, tile), dst.reshape(steps, 1, tile), u_flat, zrow)


def _expert_kernel(be_ref, nu_ref, x_ref, w1_ref, w3_ref, w2_ref, y_ref, *, chunks):
    i = pl.program_id(0)

    @pl.when(i < nu_ref[0])
    def _():
        x = x_ref[...].astype(BF16)
        acc = None
        for c0, c1 in chunks:
            a = _dot(x, w1_ref[0, :, c0:c1])
            hid = (a * jax.nn.sigmoid(a) * _dot(x, w3_ref[0, :, c0:c1])).astype(BF16)
            part = _dot(hid, w2_ref[0, c0:c1, :])
            acc = part if acc is None else acc + part
        y_ref[...] = acc

    @pl.when(i >= nu_ref[0])
    def _():
        y_ref[...] = jnp.zeros(y_ref.shape, F32)


def _experts(xs, blk_e, n_used, w1, w3, w2):
    cap, d = xs.shape
    f = w1.shape[2]
    nblk = cap // MOE_BLOCK
    grid_spec = pltpu.PrefetchScalarGridSpec(
        num_scalar_prefetch=2,
        grid=(nblk,),
        in_specs=[
            pl.BlockSpec((MOE_BLOCK, d), lambda i, be, nu: (i, 0)),
            pl.BlockSpec((1, d, f), lambda i, be, nu: (be[i], 0, 0)),
            pl.BlockSpec((1, d, f), lambda i, be, nu: (be[i], 0, 0)),
            pl.BlockSpec((1, f, d), lambda i, be, nu: (be[i], 0, 0)),
        ],
        out_specs=pl.BlockSpec((MOE_BLOCK, d), lambda i, be, nu: (i, 0)),
    )
    return pl.pallas_call(
        functools.partial(_expert_kernel, chunks=_ff_chunks(f)),
        grid_spec=grid_spec,
        out_shape=jax.ShapeDtypeStruct((cap, d), F32),
        compiler_params=_params("arbitrary"),
        name="moe_experts",
    )(blk_e, n_used, xs, w1, w3, w2)


def _combine_kernel(dst_ref, dnext_ref, h_ref, gate_ref, mod_ref, *rest, final):
    if final:
        gf_ref, y_hbm, o_ref, ybuf, sem = rest
    else:
        y_hbm, o_ref, ybuf, sem = rest
    i = pl.program_id(0)
    n = pl.num_programs(0)
    tc = h_ref.shape[0]

    def issue(idx_ref, slot):
        def body(r, c):
            for k in range(TOP_K):
                pltpu.make_async_copy(y_hbm.at[pl.ds(idx_ref[0, 0, TOP_K * r + k], 1), :],
                                      ybuf.at[slot, k, pl.ds(r, 1), :], sem.at[slot]).start()
            return c
        lax.fori_loop(0, tc, body, 0)

    @pl.when(i == 0)
    def _():
        issue(dst_ref, 0)

    slot = i % 2

    @pl.when(i + 1 < n)
    def _():
        issue(dnext_ref, 1 - slot)

    for k in range(TOP_K):
        pltpu.make_async_copy(y_hbm.at[pl.ds(0, tc), :], ybuf.at[slot, k], sem.at[slot]).wait()

    gate = gate_ref[...]
    y = gate[:, 0:1] * ybuf[slot, 0] + gate[:, 1:2] * ybuf[slot, 1]
    h = h_ref[...] + mod_ref[0, 5:6, :] * y
    if final:
        h = h * lax.rsqrt(jnp.mean(h * h, axis=-1, keepdims=True) + EPS) * gf_ref[...]
    o_ref[...] = h


def _combine(h_flat, y, dest, gates, mod, mod_row, tokens_per_batch, final_g):
    n_tok, d = h_flat.shape
    tc = COMBINE_TILE
    steps = n_tok // tc
    per_b = tokens_per_batch // tc
    dst2 = dest.reshape(steps, 1, TOP_K * tc)
    smem = lambda fn: pl.BlockSpec((1, 1, TOP_K * tc), fn, memory_space=pltpu.SMEM)
    in_specs = [
        smem(lambda i: (i, 0, 0)),
        smem(lambda i: (jnp.minimum(i + 1, steps - 1), 0, 0)),
        pl.BlockSpec((tc, d), lambda i: (i, 0)),
        pl.BlockSpec((tc, TOP_K), lambda i: (i, 0)),
        pl.BlockSpec((1, 6, d), (lambda i: (i // per_b, 0, 0)) if mod_row is None
                     else (lambda i: (mod_row, 0, 0))),
    ]
    args = [dst2, dst2, h_flat, gates, mod]
    if final_g is not None:
        in_specs.append(pl.BlockSpec((1, d), lambda i: (0, 0)))
        args.append(final_g)
    in_specs.append(pl.BlockSpec(memory_space=pl.ANY))
    args.append(y)
    return pl.pallas_call(
        functools.partial(_combine_kernel, final=final_g is not None),
        grid=(steps,),
        in_specs=in_specs,
        out_specs=pl.BlockSpec((tc, d), lambda i: (i, 0)),
        out_shape=jax.ShapeDtypeStruct((n_tok, d), F32),
        scratch_shapes=[pltpu.VMEM((2, TOP_K, tc, d), F32), pltpu.SemaphoreType.DMA((2,))],
        compiler_params=_params("arbitrary"),
        name="moe_combine",
    )(*args)


def _routing(top_i, n_exp):
    e = top_i.reshape(-1)
    a = e.shape[0]
    onehot = (e[:, None] == jnp.arange(n_exp, dtype=jnp.int32)[None, :]).astype(jnp.int32)
    csum = jnp.cumsum(onehot, axis=0)
    rank = jnp.sum(onehot * csum, axis=1) - 1
    counts = csum[-1]
    padded = (counts + MOE_BLOCK - 1) // MOE_BLOCK * MOE_BLOCK
    pend = jnp.cumsum(padded)
    pstart = pend - padded
    dest = pstart[e] + rank
    cap = -(-a // MOE_BLOCK) * MOE_BLOCK + n_exp * MOE_BLOCK
    nblk = cap // MOE_BLOCK
    blk_e = jnp.minimum(jnp.searchsorted(pend, jnp.arange(nblk, dtype=jnp.int32) * MOE_BLOCK, side="right"),
                        n_exp - 1).astype(jnp.int32)
    n_used = (pend[-1] // MOE_BLOCK).astype(jnp.int32).reshape(1)
    n_pad = cap - a
    gap_start = jnp.concatenate([pstart + counts, pend[-1:]])
    gap_len = jnp.concatenate([padded - counts, (cap - pend[-1])[None]])
    gap_end = jnp.cumsum(gap_len)
    pidx = jnp.arange(n_pad, dtype=jnp.int32)
    grp = jnp.searchsorted(gap_end, pidx, side="right")
    pad_dst = gap_start[grp] + (pidx - (gap_end - gap_len)[grp])
    return dest.astype(jnp.int32), pad_dst.astype(jnp.int32), blk_e, n_used, cap


def _moe_mixer(h_mid, u, top_i, top_g, mod, mod_row, moe_w, n_exp, final_g):
    b, t, d = h_mid.shape
    n_tok = b * t
    dest, pad_dst, blk_e, n_used, cap = _routing(top_i, n_exp)
    tok = jnp.repeat(jnp.arange(n_tok, dtype=jnp.int32), TOP_K)
    src = jnp.concatenate([tok, jnp.full(pad_dst.shape, n_tok, jnp.int32)])
    dst = jnp.concatenate([dest, pad_dst])
    xs = _dispatch(u.reshape(n_tok, d), src, dst, cap)
    y = _experts(xs, blk_e, n_used, *moe_w)
    out = _combine(h_mid.reshape(n_tok, d), y, dest, top_g.reshape(n_tok, TOP_K), mod, mod_row, t, final_g)
    return out.reshape(b, t, d)


def _layer_weights(w_in_l, conv_w_l, b_gates_l, w_out_l):
    cd = conv_w_l.shape[1]
    n_gate = b_gates_l.size
    md = (w_in_l.shape[1] - 3 * cd - n_gate) // 4
    q0 = 3 * cd
    g0 = q0 + 3 * md
    wg = jnp.pad(w_in_l[:, g0:g0 + n_gate], ((0, 0), (0, LANES - n_gate)))
    return {
        "wc": w_in_l[:, :q0].astype(BF16),
        "wq": w_in_l[:, q0:q0 + md].astype(BF16),
        "wk": w_in_l[:, q0 + md:q0 + 2 * md].astype(BF16),
        "wv": w_in_l[:, q0 + 2 * md:g0].astype(BF16),
        "wg": wg.astype(BF16),
        "bg": b_gates_l.reshape(1, n_gate),
        "wo": w_in_l[:, g0 + n_gate:].astype(BF16),
        "cw": conv_w_l,
        "woc": w_out_l[:cd].astype(BF16),
        "wom": w_out_l[cd:].astype(BF16),
    }


def kernel(x, c, ctx, c_ctx, norm1_g, norm2_g, w_ada, b_ada, w_in, conv_w, b_gates, mlstm_norm_g, w_out,
           ffn_w1, ffn_w3, ffn_w2, moe_router, moe_w1, moe_w3, moe_w2, final_norm_g):
    b, t, d = x.shape
    tc = ctx.shape[1]
    depth = w_in.shape[0]
    n_exp = moe_router.shape[-1]
    assert t % GRID_W == 0 and t % CHUNK == 0 and tc % CHUNK == 0

    ctx_row = b
    n_rows = -(-(b + 1) // SUBLANES) * SUBLANES
    c_all = jnp.concatenate([c, c_ctx[None, :], jnp.zeros((n_rows - b - 1, d), F32)], axis=0)
    mod_all = _ada(c_all, w_ada, b_ada).reshape(depth, n_rows, 6, d)
    final_g = final_norm_g.reshape(1, d)

    h, hc = x, ctx
    for layer in range(depth):
        last = layer == depth - 1
        j = layer // 2
        mod = mod_all[layer]
        wts = _layer_weights(w_in[layer], conv_w[layer], b_gates[layer], w_out[layer])
        n1 = norm1_g[layer].reshape(1, d)
        n2 = norm2_g[layer].reshape(1, d)

        ql, kl, vl, gl, ycl, ogl = _in_proj(h, mod, None, n1, wts, GRID_W, True)
        res_c = _in_proj(hc, mod, ctx_row, n1, wts, tc, not last)
        qc, kc, vc, gc = res_c[:4]
        ogc = None if last else res_c[5]
        ymc, yml = _mlstm(gc, gl, (qc, kc, vc), (ql, kl, vl), ogc, ogl,
                          mlstm_norm_g[layer].reshape(1, -1), not last)

        def mixer(hh, yc_, ym_, mod_row, fin):
            if layer % 2 == 0:
                ffn = (ffn_w1[j].astype(BF16), ffn_w3[j].astype(BF16), ffn_w2[j].astype(BF16))
                return _out_dense(hh, yc_, ym_, mod, mod_row, n2, wts, ffn, fin)
            router_pad = jnp.pad(moe_router[j], ((0, 0), (0, LANES - n_exp)))
            h_mid, u2, top_i, top_g = _out_moe(hh, yc_, ym_, mod, mod_row, n2, wts, router_pad, n_exp)
            moe_w = (moe_w1[j].astype(BF16), moe_w3[j].astype(BF16), moe_w2[j].astype(BF16))
            return _moe_mixer(h_mid, u2, top_i, top_g, mod, mod_row, moe_w, n_exp, fin)

        h = mixer(h, ycl, yml, None, final_g if last else None)
        if not last:
            hc = mixer(hc, res_c[4], ymc, ctx_row, None)
    return h
```

```python
import functools

import jax
import jax.numpy as jnp
from jax import lax
from jax.experimental import pallas as pl
from jax.experimental.pallas import tpu as pltpu

GRID_W = 64
CHUNK = 128
EPS = 1e-6
TOP_K = 2
ROW_TILE = 512
MOE_BLOCK = 256
COMBINE_TILE = 256
LANES = 128
SUBLANES = 8
VMEM_LIMIT = 56 * 1024 * 1024

F32 = jnp.float32
BF16 = jnp.bfloat16
HIGHEST = lax.Precision.HIGHEST


def _dot(a, b):
    return jnp.dot(a, b, preferred_element_type=F32)


def _params(*sem):
    return pltpu.CompilerParams(dimension_semantics=sem, vmem_limit_bytes=VMEM_LIMIT)


def _ff_chunks(d_ff, max_chunk=1024):
    out, c0 = [], 0
    while c0 < d_ff:
        c1 = min(c0 + max_chunk, d_ff)
        out.append((c0, c1))
        c0 = c1
    return out


def _ada_kernel(c_ref, w_ref, b_ref, o_ref):
    s = c_ref[...]
    s = s * jax.nn.sigmoid(s)
    o_ref[0] = jnp.dot(s, w_ref[0], precision=HIGHEST, preferred_element_type=F32) + b_ref[0]


def _ada(c_all, w_ada, b_ada):
    depth, d, d6 = w_ada.shape
    rows = c_all.shape[0]
    tn = d6 // 4
    return pl.pallas_call(
        _ada_kernel,
        grid=(depth, d6 // tn),
        in_specs=[
            pl.BlockSpec((rows, d), lambda l, j: (0, 0)),
            pl.BlockSpec((1, d, tn), lambda l, j: (l, 0, j)),
            pl.BlockSpec((1, 1, tn), lambda l, j: (l, 0, j)),
        ],
        out_specs=pl.BlockSpec((1, rows, tn), lambda l, j: (l, 0, j)),
        out_shape=jax.ShapeDtypeStruct((depth, rows, d6), F32),
        compiler_params=_params("parallel", "parallel"),
        name="adaln_mod",
    )(c_all, w_ada, b_ada.reshape(depth, 1, d6))


def _norm_mod(x, g, shift, scale):
    y = x * lax.rsqrt(jnp.mean(x * x, axis=-1, keepdims=True) + EPS)
    return (y * g) * (1.0 + scale) + shift


def _in_kernel(x_ref, mod_ref, g_ref, wq_ref, wk_ref, wv_ref, wg_ref, bg_ref, *rest,
               row_w, full, kscale, n_gate):
    if full:
        wc_ref, wo_ref, cw_ref, q_ref, k_ref, v_ref, gt_ref, yc_ref, og_ref = rest
    else:
        q_ref, k_ref, v_ref, gt_ref = rest
    x = x_ref[0]
    u = _norm_mod(x, g_ref[...], mod_ref[0, 0:1, :], mod_ref[0, 1:2, :]).astype(BF16)
    q_ref[0] = _dot(u, wq_ref[...]).astype(BF16)
    k_ref[0] = (_dot(u, wk_ref[...]) * kscale).astype(BF16)
    v_ref[0] = _dot(u, wv_ref[...]).astype(BF16)
    gt_ref[0] = _dot(u, wg_ref[...])[:, :n_gate] + bg_ref[...]
    if full:
        cd = cw_ref.shape[1]
        c3 = _dot(u, wc_ref[...])
        z = c3[:, cd:2 * cd] * c3[:, 2 * cd:]
        tm = z.shape[0]
        t = lax.broadcasted_iota(jnp.int32, (tm, 1), 0) % row_w
        zprev = jnp.where(t == 0, 0.0, pltpu.roll(z, 1, axis=0))
        znext = jnp.where(t == row_w - 1, 0.0, pltpu.roll(z, tm - 1, axis=0))
        conv = zprev * cw_ref[0:1, :] + z * cw_ref[1:2, :] + znext * cw_ref[2:3, :]
        yc_ref[0] = (c3[:, :cd] * conv).astype(BF16)
        og_ref[0] = jax.nn.sigmoid(_dot(u, wo_ref[...])).astype(BF16)


def _in_proj(h, mod, mod_row, norm_g, wts, row_w, full):
    b, t, d = h.shape
    tm = min(ROW_TILE, t)
    md = wts["wq"].shape[1]
    n_gate = wts["bg"].shape[1]
    head_dim = md // (n_gate // 4)
    const = lambda shape: pl.BlockSpec(shape, lambda i, j: (0,) * len(shape))
    tile = lambda w: pl.BlockSpec((1, tm, w), lambda i, j: (i, j, 0))
    in_specs = [
        tile(d),
        pl.BlockSpec((1, 6, d), (lambda i, j: (i, 0, 0)) if mod_row is None
                     else (lambda i, j: (mod_row, 0, 0))),
        const((1, d)),
        const(wts["wq"].shape), const(wts["wk"].shape), const(wts["wv"].shape),
        const(wts["wg"].shape), const(wts["bg"].shape),
    ]
    args = [h, mod, norm_g, wts["wq"], wts["wk"], wts["wv"], wts["wg"], wts["bg"]]
    out_specs = [tile(md), tile(md), tile(md), tile(n_gate)]
    out_shape = [jax.ShapeDtypeStruct((b, t, md), BF16)] * 3 + [jax.ShapeDtypeStruct((b, t, n_gate), F32)]
    if full:
        cd = wts["cw"].shape[1]
        in_specs += [const(wts["wc"].shape), const(wts["wo"].shape), const(wts["cw"].shape)]
        args += [wts["wc"], wts["wo"], wts["cw"]]
        out_specs += [tile(cd), tile(md)]
        out_shape += [jax.ShapeDtypeStruct((b, t, cd), BF16), jax.ShapeDtypeStruct((b, t, md), BF16)]
    return pl.pallas_call(
        functools.partial(_in_kernel, row_w=row_w, full=full, kscale=head_dim ** -0.5, n_gate=n_gate),
        grid=(b, t // tm),
        in_specs=in_specs,
        out_specs=out_specs,
        out_shape=out_shape,
        compiler_params=_params("parallel", "parallel"),
        name="in_proj",
    )(*args)


def _lane_scan(x, op, ident, reverse):
    n = x.shape[1]
    lane = lax.broadcasted_iota(jnp.int32, x.shape, 1)
    s = 1
    while s < n:
        if reverse:
            shifted = jnp.where(lane < n - s, pltpu.roll(x, n - s, axis=1), ident)
        else:
            shifted = jnp.where(lane >= s, pltpu.roll(x, s, axis=1), ident)
        x = op(x, shifted)
        s *= 2
    return x


def _mlstm_kernel(gr_ref, qc_ref, kc_ref, vc_ref, ql_ref, kl_ref, vl_ref, ogl_ref, gain_ref,
                  *rest, need_ctx):
    if need_ctx:
        ogc_ref, yc_ref, yl_ref, cols_ref, rows_ref, hc_ref, hl_ref = rest
    else:
        yl_ref, cols_ref, rows_ref, hc_ref, hl_ref = rest
        ogc_ref = yc_ref = None
    L = CHUNK
    hd = ql_ref.shape[2]
    ncc = qc_ref.shape[1] // L
    ncl = ql_ref.shape[1] // L
    nct = ncc + ncl

    sub = lax.broadcasted_iota(jnp.int32, (SUBLANES, L), 0)

    def prologue(c, carry):
        g4 = gr_ref[0, 0, c]
        for d in range(2):
            rev = d == 1
            ig = jnp.broadcast_to(g4[2 * d:2 * d + 1, :], (SUBLANES, L))
            lf = jnp.broadcast_to(jax.nn.log_sigmoid(g4[2 * d + 1:2 * d + 2, :]), (SUBLANES, L))
            a = _lane_scan(lf, jnp.add, 0.0, rev)
            r = ig - a
            p = _lane_scan(r, jnp.maximum, -jnp.inf, rev)
            tile = jnp.where(sub == 0, a, jnp.where(sub == 1, r, jnp.where(sub == 2, p, 0.0)))
            rows_ref[d, pl.ds(pl.multiple_of(c * SUBLANES, SUBLANES), SUBLANES), :] = tile
            full_tile = jnp.concatenate([tile, jnp.zeros((L - SUBLANES, L), F32)], axis=0)
            cols_ref[d, pl.ds(pl.multiple_of(c * L, L), L), :] = full_tile.T
        return carry

    lax.fori_loop(0, nct, prologue, 0)

    tpos = lax.broadcasted_iota(jnp.int32, (L, L), 0)
    spos = lax.broadcasted_iota(jnp.int32, (L, L), 1)
    ones_col = jnp.where(lax.broadcasted_iota(jnp.int32, (L, hd), 1) == 0, 1.0, 0.0).astype(BF16)
    gain = gain_ref[...]

    def chunk_step(d, q_ref, k_ref, v_ref, h_ref, og_ref, y_ref, j, g, caug, m, want_h):
        off = pl.multiple_of(j * L, L)
        k = k_ref[0, pl.ds(off, L), :]
        v = v_ref[0, pl.ds(off, L), :]
        vaug = jnp.concatenate([v, ones_col], axis=1)
        ct = cols_ref[d, pl.ds(pl.multiple_of(g * L, L), L), :]
        rt = rows_ref[d, pl.ds(pl.multiple_of(g * SUBLANES, SUBLANES), SUBLANES), :]
        a_col, r_col, p_col = ct[:, 0:1], ct[:, 1:2], ct[:, 2:3]
        last = 0 if d == 1 else L - 1
        b_tot = rt[0:1, last:last + 1]
        p_last = rt[2:3, last:last + 1]
        if want_h:
            q = q_ref[0, pl.ds(off, L), :]
            r_row = rt[1:2, :]
            m_col = jnp.maximum(p_col, m)
            mask = (spos >= tpos) if d == 1 else (spos <= tpos)
            wmat = jnp.where(mask, jnp.exp(r_row - m_col), 0.0)
            s = lax.dot_general(q, k, (((1,), (1,)), ((), ())), preferred_element_type=F32)
            sb = (s * wmat).astype(BF16)
            nd = _dot(sb, vaug) + jnp.exp(m - m_col) * _dot(q, caug.astype(BF16))
            den = jnp.maximum(jnp.abs(nd[:, hd:hd + 1]), jnp.exp(-(a_col + m_col)))
            hout = nd[:, :hd] * (1.0 / den)
            if d == 0:
                h_ref[pl.ds(off, L), :] = hout
            else:
                hh = h_ref[pl.ds(off, L), :] + hout
                hn = hh * lax.rsqrt(jnp.mean(hh * hh, axis=-1, keepdims=True) + EPS)
                og = og_ref[0, pl.ds(off, L), :].astype(F32)
                y_ref[0, pl.ds(off, L), :] = (hn * gain * og).astype(BF16)
        m_last = jnp.maximum(p_last, m)
        kw = (k.astype(F32) * jnp.exp(r_col - m_last)).astype(BF16)
        upd = lax.dot_general(kw, vaug, (((0,), (0,)), ((), ())), preferred_element_type=F32)
        caug = jnp.exp(m - m_last) * caug + upd
        return caug, b_tot + m_last

    for d in range(2):
        state = (jnp.zeros((hd, 2 * hd), F32), jnp.zeros((1, 1), F32))

        def ctx_body(i, st, d=d):
            j = (ncc - 1 - i) if d == 1 else i
            return chunk_step(d, qc_ref, kc_ref, vc_ref, hc_ref, ogc_ref, yc_ref, j, j, st[0], st[1], need_ctx)

        def lat_body(i, st, d=d):
            j = (ncl - 1 - i) if d == 1 else i
            return chunk_step(d, ql_ref, kl_ref, vl_ref, hl_ref, ogl_ref, yl_ref, j, ncc + j, st[0], st[1], True)

        state = lax.fori_loop(0, ncc, ctx_body, state)
        lax.fori_loop(0, ncl, lat_body, state)


def _mlstm(gates_c, gates_l, qkv_c, qkv_l, og_c, og_l, gain, need_ctx):
    b, tc, md = qkv_c[0].shape
    tl = qkv_l[0].shape[1]
    n_gate = gates_c.shape[2]
    heads = n_gate // 4
    hd = md // heads
    L = CHUNK
    nct = (tc + tl) // L
    g = jnp.concatenate([gates_c, gates_l], axis=1).reshape(b, nct, L, 4, heads)
    g = g.transpose(0, 4, 1, 3, 2)
    seq = lambda t: pl.BlockSpec((1, t, hd), lambda i, h: (i, 0, h))
    in_specs = [pl.BlockSpec((1, 1, nct, 4, L), lambda i, h: (i, h, 0, 0, 0)),
                seq(tc), seq(tc), seq(tc), seq(tl), seq(tl), seq(tl), seq(tl),
                pl.BlockSpec((1, hd), lambda i, h: (0, h))]
    args = [g, *qkv_c, *qkv_l, og_l, gain]
    out_specs = [seq(tl)]
    out_shape = [jax.ShapeDtypeStruct((b, tl, md), BF16)]
    if need_ctx:
        in_specs.append(seq(tc))
        args.append(og_c)
        out_specs = [seq(tc)] + out_specs
        out_shape = [jax.ShapeDtypeStruct((b, tc, md), BF16)] + out_shape
    res = pl.pallas_call(
        functools.partial(_mlstm_kernel, need_ctx=need_ctx),
        grid=(b, heads),
        in_specs=in_specs,
        out_specs=out_specs,
        out_shape=out_shape,
        scratch_shapes=[
            pltpu.VMEM((2, nct * L, L), F32),
            pltpu.VMEM((2, nct * SUBLANES, L), F32),
            pltpu.VMEM((tc, hd), F32),
            pltpu.VMEM((tl, hd), F32),
        ],
        compiler_params=_params("parallel", "parallel"),
        name="mlstm_scan",
    )(*args)
    return (res[0], res[1]) if need_ctx else (None, res[0])


def _mix_residual(h_ref, yc_ref, ym_ref, mod_ref, woc_ref, wom_ref):
    mix = _dot(yc_ref[0], woc_ref[...]) + _dot(ym_ref[0], wom_ref[...])
    return h_ref[0] + mod_ref[0, 2:3, :] * mix


def _out_dense_kernel(h_ref, yc_ref, ym_ref, mod_ref, g2_ref, woc_ref, wom_ref, w1_ref, w3_ref, w2_ref,
                      *rest, chunks, final):
    if final:
        gf_ref, o_ref = rest
    else:
        (o_ref,) = rest
    h = _mix_residual(h_ref, yc_ref, ym_ref, mod_ref, woc_ref, wom_ref)
    u = _norm_mod(h, g2_ref[...], mod_ref[0, 3:4, :], mod_ref[0, 4:5, :]).astype(BF16)
    acc = None
    for c0, c1 in chunks:
        a = _dot(u, w1_ref[:, c0:c1])
        hid = (a * jax.nn.sigmoid(a) * _dot(u, w3_ref[:, c0:c1])).astype(BF16)
        part = _dot(hid, w2_ref[c0:c1, :])
        acc = part if acc is None else acc + part
    h = h + mod_ref[0, 5:6, :] * acc
    if final:
        h = h * lax.rsqrt(jnp.mean(h * h, axis=-1, keepdims=True) + EPS) * gf_ref[...]
    o_ref[0] = h


def _out_moe_kernel(h_ref, yc_ref, ym_ref, mod_ref, g2_ref, woc_ref, wom_ref, router_ref,
                    hm_ref, u_ref, ti_ref, tg_ref, *, n_exp):
    h = _mix_residual(h_ref, yc_ref, ym_ref, mod_ref, woc_ref, wom_ref)
    hm_ref[0] = h
    u = _norm_mod(h, g2_ref[...], mod_ref[0, 3:4, :], mod_ref[0, 4:5, :])
    u_ref[0] = u
    logits = jnp.dot(u, router_ref[...], precision=HIGHEST, preferred_element_type=F32)
    lane = lax.broadcasted_iota(jnp.int32, logits.shape, 1)
    logits = jnp.where(lane < n_exp, logits, -jnp.inf)
    v1 = jnp.max(logits, axis=-1, keepdims=True)
    i1 = jnp.min(jnp.where(logits == v1, lane, LANES), axis=-1, keepdims=True)
    rest = jnp.where(lane == i1, -jnp.inf, logits)
    v2 = jnp.max(rest, axis=-1, keepdims=True)
    i2 = jnp.min(jnp.where(rest == v2, lane, LANES), axis=-1, keepdims=True)
    e2 = jnp.exp(v2 - v1)
    inv = 1.0 / (1.0 + e2)
    first = lax.broadcasted_iota(jnp.int32, ti_ref.shape[1:], 1) == 0
    ti_ref[0] = jnp.where(first, i1, i2)
    tg_ref[0] = jnp.where(first, inv, e2 * inv)


def _out_common(h, yc, ym, mod, mod_row, norm_g, wts):
    b, t, d = h.shape
    tm = min(ROW_TILE, t)
    const = lambda shape: pl.BlockSpec(shape, lambda i, j: (0,) * len(shape))
    tile = lambda w: pl.BlockSpec((1, tm, w), lambda i, j: (i, j, 0))
    in_specs = [
        tile(d), tile(yc.shape[2]), tile(ym.shape[2]),
        pl.BlockSpec((1, 6, d), (lambda i, j: (i, 0, 0)) if mod_row is None
                     else (lambda i, j: (mod_row, 0, 0))),
        const((1, d)), const(wts["woc"].shape), const(wts["wom"].shape),
    ]
    args = [h, yc, ym, mod, norm_g, wts["woc"], wts["wom"]]
    return b, t, d, tm, const, tile, in_specs, args


def _out_dense(h, yc, ym, mod, mod_row, norm_g, wts, ffn, final_g):
    b, t, d, tm, const, tile, in_specs, args = _out_common(h, yc, ym, mod, mod_row, norm_g, wts)
    w1, w3, w2 = ffn
    in_specs += [const(w1.shape), const(w3.shape), const(w2.shape)]
    args += [w1, w3, w2]
    if final_g is not None:
        in_specs.append(const((1, d)))
        args.append(final_g)
    return pl.pallas_call(
        functools.partial(_out_dense_kernel, chunks=_ff_chunks(w1.shape[1]), final=final_g is not None),
        grid=(b, t // tm),
        in_specs=in_specs,
        out_specs=tile(d),
        out_shape=jax.ShapeDtypeStruct((b, t, d), F32),
        compiler_params=_params("parallel", "parallel"),
        name="out_dense_ffn",
    )(*args)


def _out_moe(h, yc, ym, mod, mod_row, norm_g, wts, router_pad, n_exp):
    b, t, d, tm, const, tile, in_specs, args = _out_common(h, yc, ym, mod, mod_row, norm_g, wts)
    in_specs.append(const(router_pad.shape))
    args.append(router_pad)
    return pl.pallas_call(
        functools.partial(_out_moe_kernel, n_exp=n_exp),
        grid=(b, t // tm),
        in_specs=in_specs,
        out_specs=[tile(d), tile(d), tile(TOP_K), tile(TOP_K)],
        out_shape=[jax.ShapeDtypeStruct((b, t, d), F32), jax.ShapeDtypeStruct((b, t, d), F32),
                   jax.ShapeDtypeStruct((b, t, TOP_K), jnp.int32), jax.ShapeDtypeStruct((b, t, TOP_K), F32)],
        compiler_params=_params("parallel", "parallel"),
        name="out_router",
    )(*args)


ISSUE_UNROLL = 8


def _expert_kernel(be_ref, nu_ref, tok_ref, toknext_ref, u_hbm, w1_ref, w3_ref, w2_ref, y_ref, xbuf, sem,
                   *, chunks):
    i = pl.program_id(0)
    n = pl.num_programs(0)
    blk = xbuf.shape[1]
    slot = i % 2

    def issue(idx_ref, to_slot):
        def body(r, c):
            pltpu.make_async_copy(u_hbm.at[pl.ds(idx_ref[0, 0, r], 1), :],
                                  xbuf.at[to_slot, pl.ds(r, 1), :], sem.at[to_slot]).start()
            return c
        lax.fori_loop(0, blk, body, 0, unroll=ISSUE_UNROLL)

    @pl.when(i == 0)
    def _():
        issue(tok_ref, 0)

    @pl.when(i + 1 < n)
    def _():
        issue(toknext_ref, 1 - slot)

    pltpu.make_async_copy(u_hbm.at[pl.ds(0, blk), :], xbuf.at[slot], sem.at[slot]).wait()

    @pl.when(i < nu_ref[0])
    def _():
        x = xbuf[slot].astype(BF16)
        acc = None
        for c0, c1 in chunks:
            a = _dot(x, w1_ref[0, :, c0:c1])
            hid = (a * jax.nn.sigmoid(a) * _dot(x, w3_ref[0, :, c0:c1])).astype(BF16)
            part = _dot(hid, w2_ref[0, c0:c1, :])
            acc = part if acc is None else acc + part
        y_ref[...] = acc

    @pl.when(i >= nu_ref[0])
    def _():
        y_ref[...] = jnp.zeros(y_ref.shape, F32)


def _experts(u_flat, tok_buf, blk_e, n_used, w1, w3, w2):
    n_tok, d = u_flat.shape
    f = w1.shape[2]
    nblk = tok_buf.shape[0] // MOE_BLOCK
    tok3 = tok_buf.reshape(nblk, 1, MOE_BLOCK)
    smem = lambda fn: pl.BlockSpec((1, 1, MOE_BLOCK), fn, memory_space=pltpu.SMEM)
    grid_spec = pltpu.PrefetchScalarGridSpec(
        num_scalar_prefetch=2,
        grid=(nblk,),
        in_specs=[
            smem(lambda i, be, nu: (i, 0, 0)),
            smem(lambda i, be, nu: (jnp.minimum(i + 1, nblk - 1), 0, 0)),
            pl.BlockSpec(memory_space=pl.ANY),
            pl.BlockSpec((1, d, f), lambda i, be, nu: (be[i], 0, 0)),
            pl.BlockSpec((1, d, f), lambda i, be, nu: (be[i], 0, 0)),
            pl.BlockSpec((1, f, d), lambda i, be, nu: (be[i], 0, 0)),
        ],
        out_specs=pl.BlockSpec((MOE_BLOCK, d), lambda i, be, nu: (i, 0)),
        scratch_shapes=[pltpu.VMEM((2, MOE_BLOCK, d), F32), pltpu.SemaphoreType.DMA((2,))],
    )
    return pl.pallas_call(
        functools.partial(_expert_kernel, chunks=_ff_chunks(f)),
        grid_spec=grid_spec,
        out_shape=jax.ShapeDtypeStruct((nblk * MOE_BLOCK, d), F32),
        compiler_params=_params("arbitrary"),
        name="moe_experts",
    )(blk_e, n_used, tok3, tok3, u_flat, w1, w3, w2)


def _combine_kernel(dst_ref, dnext_ref, h_ref, gate_ref, mod_ref, *rest, final):
    if final:
        gf_ref, y_hbm, o_ref, ybuf, sem = rest
    else:
        y_hbm, o_ref, ybuf, sem = rest
    i = pl.program_id(0)
    n = pl.num_programs(0)
    tc = h_ref.shape[0]

    def issue(idx_ref, slot):
        def body(r, c):
            for k in range(TOP_K):
                pltpu.make_async_copy(y_hbm.at[pl.ds(idx_ref[0, 0, TOP_K * r + k], 1), :],
                                      ybuf.at[slot, k, pl.ds(r, 1), :], sem.at[slot]).start()
            return c
        lax.fori_loop(0, tc, body, 0)

    @pl.when(i == 0)
    def _():
        issue(dst_ref, 0)

    slot = i % 2

    @pl.when(i + 1 < n)
    def _():
        issue(dnext_ref, 1 - slot)

    for k in range(TOP_K):
        pltpu.make_async_copy(y_hbm.at[pl.ds(0, tc), :], ybuf.at[slot, k], sem.at[slot]).wait()

    gate = gate_ref[...]
    y = gate[:, 0:1] * ybuf[slot, 0] + gate[:, 1:2] * ybuf[slot, 1]
    h = h_ref[...] + mod_ref[0, 5:6, :] * y
    if final:
        h = h * lax.rsqrt(jnp.mean(h * h, axis=-1, keepdims=True) + EPS) * gf_ref[...]
    o_ref[...] = h


def _combine(h_flat, y, dest, gates, mod, mod_row, tokens_per_batch, final_g):
    n_tok, d = h_flat.shape
    tc = COMBINE_TILE
    steps = n_tok // tc
    per_b = tokens_per_batch // tc
    dst2 = dest.reshape(steps, 1, TOP_K * tc)
    smem = lambda fn: pl.BlockSpec((1, 1, TOP_K * tc), fn, memory_space=pltpu.SMEM)
    in_specs = [
        smem(lambda i: (i, 0, 0)),
        smem(lambda i: (jnp.minimum(i + 1, steps - 1), 0, 0)),
        pl.BlockSpec((tc, d), lambda i: (i, 0)),
        pl.BlockSpec((tc, TOP_K), lambda i: (i, 0)),
        pl.BlockSpec((1, 6, d), (lambda i: (i // per_b, 0, 0)) if mod_row is None
                     else (lambda i: (mod_row, 0, 0))),
    ]
    args = [dst2, dst2, h_flat, gates, mod]
    if final_g is not None:
        in_specs.append(pl.BlockSpec((1, d), lambda i: (0, 0)))
        args.append(final_g)
    in_specs.append(pl.BlockSpec(memory_space=pl.ANY))
    args.append(y)
    return pl.pallas_call(
        functools.partial(_combine_kernel, final=final_g is not None),
        grid=(steps,),
        in_specs=in_specs,
        out_specs=pl.BlockSpec((tc, d), lambda i: (i, 0)),
        out_shape=jax.ShapeDtypeStruct((n_tok, d), F32),
        scratch_shapes=[pltpu.VMEM((2, TOP_K, tc, d), F32), pltpu.SemaphoreType.DMA((2,))],
        compiler_params=_params("arbitrary"),
        name="moe_combine",
    )(*args)


def _routing(top_i, n_exp):
    e = top_i.reshape(-1)
    a = e.shape[0]
    experts = jnp.arange(n_exp, dtype=jnp.int32)
    onehot = (e[:, None] == experts[None, :]).astype(jnp.int32)
    csum = jnp.cumsum(onehot, axis=0)
    rank = jnp.sum(onehot * csum, axis=1) - 1
    counts = csum[-1]
    padded = (counts + MOE_BLOCK - 1) // MOE_BLOCK * MOE_BLOCK
    pend = jnp.cumsum(padded)
    pstart = pend - padded
    start = jnp.cumsum(counts) - counts
    dest = pstart[e] + rank
    cap = -(-a // MOE_BLOCK) * MOE_BLOCK + n_exp * MOE_BLOCK
    nblk = cap // MOE_BLOCK
    blk_first = jnp.arange(nblk, dtype=jnp.int32) * MOE_BLOCK
    blk_e = jnp.minimum(jnp.sum((pend[None, :] <= blk_first[:, None]).astype(jnp.int32), axis=1), n_exp - 1)
    n_used = (pend[-1] // MOE_BLOCK).astype(jnp.int32).reshape(1)
    order = jnp.argsort(e).astype(jnp.int32)
    row = jnp.arange(cap, dtype=jnp.int32)
    row_e = jnp.repeat(blk_e, MOE_BLOCK)
    within = row - pstart[row_e]
    valid = (within < counts[row_e]) & (row < pend[-1])
    tok_buf = jnp.where(valid, order[jnp.clip(start[row_e] + within, 0, a - 1)] // TOP_K, 0)
    return dest.astype(jnp.int32), tok_buf.astype(jnp.int32), blk_e.astype(jnp.int32), n_used


def _moe_mixer(h_mid, u, top_i, top_g, mod, mod_row, moe_w, n_exp, final_g):
    b, t, d = h_mid.shape
    n_tok = b * t
    dest, tok_buf, blk_e, n_used = _routing(top_i, n_exp)
    y = _experts(u.reshape(n_tok, d), tok_buf, blk_e, n_used, *moe_w)
    out = _combine(h_mid.reshape(n_tok, d), y, dest, top_g.reshape(n_tok, TOP_K), mod, mod_row, t, final_g)
    return out.reshape(b, t, d)


def _layer_weights(w_in_l, conv_w_l, b_gates_l, w_out_l):
    cd = conv_w_l.shape[1]
    n_gate = b_gates_l.size
    md = (w_in_l.shape[1] - 3 * cd - n_gate) // 4
    q0 = 3 * cd
    g0 = q0 + 3 * md
    wg = jnp.pad(w_in_l[:, g0:g0 + n_gate], ((0, 0), (0, LANES - n_gate)))
    return {
        "wc": w_in_l[:, :q0].astype(BF16),
        "wq": w_in_l[:, q0:q0 + md].astype(BF16),
        "wk": w_in_l[:, q0 + md:q0 + 2 * md].astype(BF16),
        "wv": w_in_l[:, q0 + 2 * md:g0].astype(BF16),
        "wg": wg.astype(BF16),
        "bg": b_gates_l.reshape(1, n_gate),
        "wo": w_in_l[:, g0 + n_gate:].astype(BF16),
        "cw": conv_w_l,
        "woc": w_out_l[:cd].astype(BF16),
        "wom": w_out_l[cd:].astype(BF16),
    }


def kernel(x, c, ctx, c_ctx, norm1_g, norm2_g, w_ada, b_ada, w_in, conv_w, b_gates, mlstm_norm_g, w_out,
           ffn_w1, ffn_w3, ffn_w2, moe_router, moe_w1, moe_w3, moe_w2, final_norm_g):
    b, t, d = x.shape
    tc = ctx.shape[1]
    depth = w_in.shape[0]
    n_exp = moe_router.shape[-1]
    assert t % GRID_W == 0 and t % CHUNK == 0 and tc % CHUNK == 0

    ctx_row = b
    n_rows = -(-(b + 1) // SUBLANES) * SUBLANES
    c_all = jnp.concatenate([c, c_ctx[None, :], jnp.zeros((n_rows - b - 1, d), F32)], axis=0)
    mod_all = _ada(c_all, w_ada, b_ada).reshape(depth, n_rows, 6, d)
    final_g = final_norm_g.reshape(1, d)

    h, hc = x, ctx
    for layer in range(depth):
        last = layer == depth - 1
        j = layer // 2
        mod = mod_all[layer]
        wts = _layer_weights(w_in[layer], conv_w[layer], b_gates[layer], w_out[layer])
        n1 = norm1_g[layer].reshape(1, d)
        n2 = norm2_g[layer].reshape(1, d)

        ql, kl, vl, gl, ycl, ogl = _in_proj(h, mod, None, n1, wts, GRID_W, True)
        res_c = _in_proj(hc, mod, ctx_row, n1, wts, tc, not last)
        qc, kc, vc, gc = res_c[:4]
        ogc = None if last else res_c[5]
        ymc, yml = _mlstm(gc, gl, (qc, kc, vc), (ql, kl, vl), ogc, ogl,
                          mlstm_norm_g[layer].reshape(1, -1), not last)

        def mixer(hh, yc_, ym_, mod_row, fin):
            if layer % 2 == 0:
                ffn = (ffn_w1[j].astype(BF16), ffn_w3[j].astype(BF16), ffn_w2[j].astype(BF16))
                return _out_dense(hh, yc_, ym_, mod, mod_row, n2, wts, ffn, fin)
            router_pad = jnp.pad(moe_router[j], ((0, 0), (0, LANES - n_exp)))
            h_mid, u2, top_i, top_g = _out_moe(hh, yc_, ym_, mod, mod_row, n2, wts, router_pad, n_exp)
            moe_w = (moe_w1[j].astype(BF16), moe_w3[j].astype(BF16), moe_w2[j].astype(BF16))
            return _moe_mixer(h_mid, u2, top_i, top_g, mod, mod_row, moe_w, n_exp, fin)

        h = mixer(h, ycl, yml, None, final_g if last else None)
        if not last:
            hc = mixer(hc, res_c[4], ymc, ctx_row, None)
    return h
```

```python
import functools

import jax
import jax.numpy as jnp
from jax import lax
from jax.experimental import pallas as pl
from jax.experimental.pallas import tpu as pltpu

GRID_W = 64
CHUNK = 128
EPS = 1e-6
TOP_K = 2
ROW_TILE = 512
MOE_BLOCK = 256
COMBINE_TILE = 256
LANES = 128
SUBLANES = 8
VMEM_LIMIT = 56 * 1024 * 1024

F32 = jnp.float32
BF16 = jnp.bfloat16
HIGHEST = lax.Precision.HIGHEST


def _dot(a, b):
    return jnp.dot(a, b, preferred_element_type=F32)


def _params(*sem):
    return pltpu.CompilerParams(dimension_semantics=sem, vmem_limit_bytes=VMEM_LIMIT)


def _ff_chunks(d_ff, max_chunk=1024):
    out, c0 = [], 0
    while c0 < d_ff:
        c1 = min(c0 + max_chunk, d_ff)
        out.append((c0, c1))
        c0 = c1
    return out


def _ada_kernel(c_ref, w_ref, b_ref, o_ref):
    s = c_ref[...]
    s = s * jax.nn.sigmoid(s)
    o_ref[0] = jnp.dot(s, w_ref[0], precision=HIGHEST, preferred_element_type=F32) + b_ref[0]


def _ada(c_all, w_ada, b_ada):
    depth, d, d6 = w_ada.shape
    rows = c_all.shape[0]
    tn = d6 // 4
    return pl.pallas_call(
        _ada_kernel,
        grid=(depth, d6 // tn),
        in_specs=[
            pl.BlockSpec((rows, d), lambda l, j: (0, 0)),
            pl.BlockSpec((1, d, tn), lambda l, j: (l, 0, j)),
            pl.BlockSpec((1, 1, tn), lambda l, j: (l, 0, j)),
        ],
        out_specs=pl.BlockSpec((1, rows, tn), lambda l, j: (l, 0, j)),
        out_shape=jax.ShapeDtypeStruct((depth, rows, d6), F32),
        compiler_params=_params("parallel", "parallel"),
        name="adaln_mod",
    )(c_all, w_ada, b_ada.reshape(depth, 1, d6))


def _norm_mod(x, g, shift, scale):
    y = x * lax.rsqrt(jnp.mean(x * x, axis=-1, keepdims=True) + EPS)
    return (y * g) * (1.0 + scale) + shift


def _in_kernel(x_ref, mod_ref, g_ref, wq_ref, wk_ref, wv_ref, wg_ref, bg_ref, *rest,
               row_w, full, kscale, n_gate):
    if full:
        wc_ref, wo_ref, cw_ref, q_ref, k_ref, v_ref, gt_ref, yc_ref, og_ref = rest
    else:
        q_ref, k_ref, v_ref, gt_ref = rest
    x = x_ref[0]
    u = _norm_mod(x, g_ref[...], mod_ref[0, 0:1, :], mod_ref[0, 1:2, :]).astype(BF16)
    q_ref[0] = _dot(u, wq_ref[...]).astype(BF16)
    k_ref[0] = (_dot(u, wk_ref[...]) * kscale).astype(BF16)
    v_ref[0] = _dot(u, wv_ref[...]).astype(BF16)
    gt_ref[0] = _dot(u, wg_ref[...])[:, :n_gate] + bg_ref[...]
    if full:
        cd = cw_ref.shape[1]
        c3 = _dot(u, wc_ref[...])
        z = c3[:, cd:2 * cd] * c3[:, 2 * cd:]
        tm = z.shape[0]
        t = lax.broadcasted_iota(jnp.int32, (tm, 1), 0) % row_w
        zprev = jnp.where(t == 0, 0.0, pltpu.roll(z, 1, axis=0))
        znext = jnp.where(t == row_w - 1, 0.0, pltpu.roll(z, tm - 1, axis=0))
        conv = zprev * cw_ref[0:1, :] + z * cw_ref[1:2, :] + znext * cw_ref[2:3, :]
        yc_ref[0] = (c3[:, :cd] * conv).astype(BF16)
        og_ref[0] = jax.nn.sigmoid(_dot(u, wo_ref[...])).astype(BF16)


def _in_proj(h, mod, mod_row, norm_g, wts, row_w, full):
    b, t, d = h.shape
    tm = min(ROW_TILE, t)
    md = wts["wq"].shape[1]
    n_gate = wts["bg"].shape[1]
    head_dim = md // (n_gate // 4)
    const = lambda shape: pl.BlockSpec(shape, lambda i, j: (0,) * len(shape))
    tile = lambda w: pl.BlockSpec((1, tm, w), lambda i, j: (i, j, 0))
    in_specs = [
        tile(d),
        pl.BlockSpec((1, 6, d), (lambda i, j: (i, 0, 0)) if mod_row is None
                     else (lambda i, j: (mod_row, 0, 0))),
        const((1, d)),
        const(wts["wq"].shape), const(wts["wk"].shape), const(wts["wv"].shape),
        const(wts["wg"].shape), const(wts["bg"].shape),
    ]
    args = [h, mod, norm_g, wts["wq"], wts["wk"], wts["wv"], wts["wg"], wts["bg"]]
    out_specs = [tile(md), tile(md), tile(md), tile(n_gate)]
    out_shape = [jax.ShapeDtypeStruct((b, t, md), BF16)] * 3 + [jax.ShapeDtypeStruct((b, t, n_gate), F32)]
    if full:
        cd = wts["cw"].shape[1]
        in_specs += [const(wts["wc"].shape), const(wts["wo"].shape), const(wts["cw"].shape)]
        args += [wts["wc"], wts["wo"], wts["cw"]]
        out_specs += [tile(cd), tile(md)]
        out_shape += [jax.ShapeDtypeStruct((b, t, cd), BF16), jax.ShapeDtypeStruct((b, t, md), BF16)]
    return pl.pallas_call(
        functools.partial(_in_kernel, row_w=row_w, full=full, kscale=head_dim ** -0.5, n_gate=n_gate),
        grid=(b, t // tm),
        in_specs=in_specs,
        out_specs=out_specs,
        out_shape=out_shape,
        compiler_params=_params("parallel", "parallel"),
        name="in_proj",
    )(*args)


def _lane_scan(x, op, ident, reverse):
    n = x.shape[1]
    lane = lax.broadcasted_iota(jnp.int32, x.shape, 1)
    s = 1
    while s < n:
        if reverse:
            shifted = jnp.where(lane < n - s, pltpu.roll(x, n - s, axis=1), ident)
        else:
            shifted = jnp.where(lane >= s, pltpu.roll(x, s, axis=1), ident)
        x = op(x, shifted)
        s *= 2
    return x


def _mlstm_kernel(gr_ref, qc_ref, kc_ref, vc_ref, ql_ref, kl_ref, vl_ref, ogl_ref, gain_ref,
                  *rest, need_ctx):
    if need_ctx:
        ogc_ref, yc_ref, yl_ref, cols_ref, ra_ref, rr_ref, rp_ref, hc_ref, hl_ref = rest
    else:
        yl_ref, cols_ref, ra_ref, rr_ref, rp_ref, hc_ref, hl_ref = rest
        ogc_ref = yc_ref = None
    L = CHUNK
    hd = ql_ref.shape[2]
    ncc = qc_ref.shape[1] // L
    ncl = ql_ref.shape[1] // L
    nct = ncc + ncl

    for d in range(2):
        rev = d == 1
        last = 0 if rev else L - 1
        ig = gr_ref[0, 0, 2 * d]
        lf = jax.nn.log_sigmoid(gr_ref[0, 0, 2 * d + 1])
        a = _lane_scan(lf, jnp.add, 0.0, rev)
        r = ig - a
        p = _lane_scan(r, jnp.maximum, -jnp.inf, rev)
        wk = jnp.exp(r - p[:, last:last + 1])
        ra_ref[d, 0:nct, :] = a
        rr_ref[d, 0:nct, :] = r
        rp_ref[d, 0:nct, :] = p
        pad = jnp.zeros((SUBLANES - 3, L), F32)
        for c in range(nct):
            tile = jnp.concatenate([a[c:c + 1], p[c:c + 1], wk[c:c + 1], pad], axis=0)
            cols_ref[d, c * L:(c + 1) * L, :] = tile.T

    tpos = lax.broadcasted_iota(jnp.int32, (L, L), 0)
    spos = lax.broadcasted_iota(jnp.int32, (L, L), 1)
    ones_col = jnp.where(lax.broadcasted_iota(jnp.int32, (L, hd), 1) == 0, 1.0, 0.0).astype(BF16)
    gain = gain_ref[...]

    def chunk_step(d, seq, j, g, caug, m, mode):
        q_ref, k_ref, v_ref, h_ref, og_ref, y_ref = seq
        aligned = lambda x: x if isinstance(x, int) else pl.multiple_of(x, L)
        off = aligned(j * L)
        last = 0 if d == 1 else L - 1
        k = k_ref[0, pl.ds(off, L), :]
        v = v_ref[0, pl.ds(off, L), :]
        vaug = jnp.concatenate([v, ones_col], axis=1)
        ct = cols_ref[d, pl.ds(aligned(g * L), L), :]
        a_col, p_col, wk_col = ct[:, 0:1], ct[:, 1:2], ct[:, 2:3]
        b_tot = ra_ref[d, pl.ds(g, 1), last:last + 1]
        p_last = rp_ref[d, pl.ds(g, 1), last:last + 1]
        if mode is not None:
            q = q_ref[0, pl.ds(off, L), :]
            r_row = rr_ref[d, pl.ds(g, 1), :]
            mask = (spos >= tpos) if d == 1 else (spos <= tpos)
            wmat = jnp.where(mask, jnp.exp(r_row - p_col), 0.0)
            s = lax.dot_general(q, k, (((1,), (1,)), ((), ())), preferred_element_type=F32)
            intra = _dot((s * wmat).astype(BF16), vaug)
            inter = _dot(q, caug.astype(BF16))
            m_col = jnp.maximum(p_col, m)
            nd = jnp.exp(p_col - m_col) * intra + jnp.exp(m - m_col) * inter
            den = jnp.maximum(jnp.abs(nd[:, hd:hd + 1]), jnp.exp(-(a_col + m_col)))
            hout = nd[:, :hd] * (1.0 / den)
            if mode == "store":
                h_ref[pl.ds(off, L), :] = hout
            else:
                hh = h_ref[pl.ds(off, L), :] + hout
                hn = hh * lax.rsqrt(jnp.mean(hh * hh, axis=-1, keepdims=True) + EPS)
                og = og_ref[0, pl.ds(off, L), :].astype(F32)
                y_ref[0, pl.ds(off, L), :] = (hn * gain * og).astype(BF16)
        m_last = jnp.maximum(p_last, m)
        kw = (k.astype(F32) * wk_col).astype(BF16)
        upd = lax.dot_general(kw, vaug, (((0,), (0,)), ((), ())), preferred_element_type=F32)
        caug = jnp.exp(m - m_last) * caug + jnp.exp(p_last - m_last) * upd
        return caug, b_tot + m_last

    def pair_step(seq, n, g0, i, st, mode):
        cf, mf, cb, mb = st
        cf, mf = chunk_step(0, seq, i, g0 + i, cf, mf, mode)
        cb, mb = chunk_step(1, seq, n - 1 - i, g0 + n - 1 - i, cb, mb, mode)
        return cf, mf, cb, mb

    ctx_seq = (qc_ref, kc_ref, vc_ref, hc_ref, ogc_ref, yc_ref)
    lat_seq = (ql_ref, kl_ref, vl_ref, hl_ref, ogl_ref, yl_ref)
    zc, zm = jnp.zeros((hd, 2 * hd), F32), jnp.zeros((1, 1), F32)
    st = (zc, zm, zc, zm)
    for i in range(ncc):
        mode = None if not need_ctx else ("store" if i < ncc // 2 else "final")
        st = pair_step(ctx_seq, ncc, 0, i, st, mode)
    st = lax.fori_loop(0, ncl // 2, lambda i, s: pair_step(lat_seq, ncl, ncc, i, s, "store"), st)
    lax.fori_loop(ncl // 2, ncl, lambda i, s: pair_step(lat_seq, ncl, ncc, i, s, "final"), st)


def _mlstm(gates_c, gates_l, qkv_c, qkv_l, og_c, og_l, gain, need_ctx):
    b, tc, md = qkv_c[0].shape
    tl = qkv_l[0].shape[1]
    n_gate = gates_c.shape[2]
    heads = n_gate // 4
    hd = md // heads
    L = CHUNK
    nct = (tc + tl) // L
    assert tc // L % 2 == 0 and tl // L % 2 == 0
    nrow = -(-nct // SUBLANES) * SUBLANES
    g = jnp.concatenate([gates_c, gates_l], axis=1).reshape(b, nct, L, 4, heads)
    g = g.transpose(0, 4, 3, 1, 2)
    seq = lambda t: pl.BlockSpec((1, t, hd), lambda i, h: (i, 0, h))
    in_specs = [pl.BlockSpec((1, 1, 4, nct, L), lambda i, h: (i, h, 0, 0, 0)),
                seq(tc), seq(tc), seq(tc), seq(tl), seq(tl), seq(tl), seq(tl),
                pl.BlockSpec((1, hd), lambda i, h: (0, h))]
    args = [g, *qkv_c, *qkv_l, og_l, gain]
    out_specs = [seq(tl)]
    out_shape = [jax.ShapeDtypeStruct((b, tl, md), BF16)]
    if need_ctx:
        in_specs.append(seq(tc))
        args.append(og_c)
        out_specs = [seq(tc)] + out_specs
        out_shape = [jax.ShapeDtypeStruct((b, tc, md), BF16)] + out_shape
    res = pl.pallas_call(
        functools.partial(_mlstm_kernel, need_ctx=need_ctx),
        grid=(b, heads),
        in_specs=in_specs,
        out_specs=out_specs,
        out_shape=out_shape,
        scratch_shapes=[
            pltpu.VMEM((2, nct * L, SUBLANES), F32),
            pltpu.VMEM((2, nrow, L), F32),
            pltpu.VMEM((2, nrow, L), F32),
            pltpu.VMEM((2, nrow, L), F32),
            pltpu.VMEM((tc, hd), F32),
            pltpu.VMEM((tl, hd), F32),
        ],
        compiler_params=_params("parallel", "parallel"),
        name="mlstm_scan",
    )(*args)
    return (res[0], res[1]) if need_ctx else (None, res[0])


def _mix_residual(h_ref, yc_ref, ym_ref, mod_ref, woc_ref, wom_ref):
    mix = _dot(yc_ref[0], woc_ref[...]) + _dot(ym_ref[0], wom_ref[...])
    return h_ref[0] + mod_ref[0, 2:3, :] * mix


def _out_dense_kernel(h_ref, yc_ref, ym_ref, mod_ref, g2_ref, woc_ref, wom_ref, w1_ref, w3_ref, w2_ref,
                      *rest, chunks, final):
    if final:
        gf_ref, o_ref = rest
    else:
        (o_ref,) = rest
    h = _mix_residual(h_ref, yc_ref, ym_ref, mod_ref, woc_ref, wom_ref)
    u = _norm_mod(h, g2_ref[...], mod_ref[0, 3:4, :], mod_ref[0, 4:5, :]).astype(BF16)
    acc = None
    for c0, c1 in chunks:
        a = _dot(u, w1_ref[:, c0:c1])
        hid = (a * jax.nn.sigmoid(a) * _dot(u, w3_ref[:, c0:c1])).astype(BF16)
        part = _dot(hid, w2_ref[c0:c1, :])
        acc = part if acc is None else acc + part
    h = h + mod_ref[0, 5:6, :] * acc
    if final:
        h = h * lax.rsqrt(jnp.mean(h * h, axis=-1, keepdims=True) + EPS) * gf_ref[...]
    o_ref[0] = h


def _out_moe_kernel(h_ref, yc_ref, ym_ref, mod_ref, g2_ref, woc_ref, wom_ref, router_ref,
                    hm_ref, u_ref, ti_ref, tg_ref, *, n_exp):
    h = _mix_residual(h_ref, yc_ref, ym_ref, mod_ref, woc_ref, wom_ref)
    hm_ref[0] = h
    u = _norm_mod(h, g2_ref[...], mod_ref[0, 3:4, :], mod_ref[0, 4:5, :])
    u_ref[0] = u
    u_hi = u.astype(BF16)
    u_lo = (u - u_hi.astype(F32)).astype(BF16)
    parts = _dot(u_hi, router_ref[...]) + _dot(u_lo, router_ref[...])
    logits = parts[:, :LANES] + parts[:, LANES:]
    lane = lax.broadcasted_iota(jnp.int32, logits.shape, 1)
    logits = jnp.where(lane < n_exp, logits, -jnp.inf)
    v1 = jnp.max(logits, axis=-1, keepdims=True)
    i1 = jnp.min(jnp.where(logits == v1, lane, LANES), axis=-1, keepdims=True)
    rest = jnp.where(lane == i1, -jnp.inf, logits)
    v2 = jnp.max(rest, axis=-1, keepdims=True)
    i2 = jnp.min(jnp.where(rest == v2, lane, LANES), axis=-1, keepdims=True)
    e2 = jnp.exp(v2 - v1)
    inv = 1.0 / (1.0 + e2)
    first = lax.broadcasted_iota(jnp.int32, ti_ref.shape[1:], 1) == 0
    ti_ref[0] = jnp.where(first, i1, i2)
    tg_ref[0] = jnp.where(first, inv, e2 * inv)


def _out_common(h, yc, ym, mod, mod_row, norm_g, wts):
    b, t, d = h.shape
    tm = min(ROW_TILE, t)
    const = lambda shape: pl.BlockSpec(shape, lambda i, j: (0,) * len(shape))
    tile = lambda w: pl.BlockSpec((1, tm, w), lambda i, j: (i, j, 0))
    in_specs = [
        tile(d), tile(yc.shape[2]), tile(ym.shape[2]),
        pl.BlockSpec((1, 6, d), (lambda i, j: (i, 0, 0)) if mod_row is None
                     else (lambda i, j: (mod_row, 0, 0))),
        const((1, d)), const(wts["woc"].shape), const(wts["wom"].shape),
    ]
    args = [h, yc, ym, mod, norm_g, wts["woc"], wts["wom"]]
    return b, t, d, tm, const, tile, in_specs, args


def _out_dense(h, yc, ym, mod, mod_row, norm_g, wts, ffn, final_g):
    b, t, d, tm, const, tile, in_specs, args = _out_common(h, yc, ym, mod, mod_row, norm_g, wts)
    w1, w3, w2 = ffn
    in_specs += [const(w1.shape), const(w3.shape), const(w2.shape)]
    args += [w1, w3, w2]
    if final_g is not None:
        in_specs.append(const((1, d)))
        args.append(final_g)
    return pl.pallas_call(
        functools.partial(_out_dense_kernel, chunks=_ff_chunks(w1.shape[1]), final=final_g is not None),
        grid=(b, t // tm),
        in_specs=in_specs,
        out_specs=tile(d),
        out_shape=jax.ShapeDtypeStruct((b, t, d), F32),
        compiler_params=_params("parallel", "parallel"),
        name="out_dense_ffn",
    )(*args)


def _out_moe(h, yc, ym, mod, mod_row, norm_g, wts, router_pad, n_exp):
    b, t, d, tm, const, tile, in_specs, args = _out_common(h, yc, ym, mod, mod_row, norm_g, wts)
    in_specs.append(const(router_pad.shape))
    args.append(router_pad)
    return pl.pallas_call(
        functools.partial(_out_moe_kernel, n_exp=n_exp),
        grid=(b, t // tm),
        in_specs=in_specs,
        out_specs=[tile(d), tile(d), tile(TOP_K), tile(TOP_K)],
        out_shape=[jax.ShapeDtypeStruct((b, t, d), F32), jax.ShapeDtypeStruct((b, t, d), F32),
                   jax.ShapeDtypeStruct((b, t, TOP_K), jnp.int32), jax.ShapeDtypeStruct((b, t, TOP_K), F32)],
        compiler_params=_params("parallel", "parallel"),
        name="out_router",
    )(*args)


ISSUE_UNROLL = 8


def _expert_kernel(be_ref, nu_ref, tok_ref, toknext_ref, u_hbm, w1_ref, w3_ref, w2_ref, y_ref, xbuf, sem,
                   *, chunks):
    i = pl.program_id(0)
    n = pl.num_programs(0)
    blk = xbuf.shape[1]
    slot = i % 2

    def issue(idx_ref, to_slot):
        def body(r, c):
            pltpu.make_async_copy(u_hbm.at[pl.ds(idx_ref[0, 0, r], 1), :],
                                  xbuf.at[to_slot, pl.ds(r, 1), :], sem.at[to_slot]).start()
            return c
        lax.fori_loop(0, blk, body, 0, unroll=ISSUE_UNROLL)

    @pl.when(i == 0)
    def _():
        issue(tok_ref, 0)

    @pl.when(i + 1 < n)
    def _():
        issue(toknext_ref, 1 - slot)

    pltpu.make_async_copy(u_hbm.at[pl.ds(0, blk), :], xbuf.at[slot], sem.at[slot]).wait()

    @pl.when(i < nu_ref[0])
    def _():
        x = xbuf[slot].astype(BF16)
        acc = None
        for c0, c1 in chunks:
            a = _dot(x, w1_ref[0, :, c0:c1])
            hid = (a * jax.nn.sigmoid(a) * _dot(x, w3_ref[0, :, c0:c1])).astype(BF16)
            part = _dot(hid, w2_ref[0, c0:c1, :])
            acc = part if acc is None else acc + part
        y_ref[...] = acc

    @pl.when(i >= nu_ref[0])
    def _():
        y_ref[...] = jnp.zeros(y_ref.shape, F32)


def _experts(u_flat, tok_buf, blk_e, n_used, w1, w3, w2):
    n_tok, d = u_flat.shape
    f = w1.shape[2]
    nblk = tok_buf.shape[0] // MOE_BLOCK
    tok3 = tok_buf.reshape(nblk, 1, MOE_BLOCK)
    smem = lambda fn: pl.BlockSpec((1, 1, MOE_BLOCK), fn, memory_space=pltpu.SMEM)
    grid_spec = pltpu.PrefetchScalarGridSpec(
        num_scalar_prefetch=2,
        grid=(nblk,),
        in_specs=[
            smem(lambda i, be, nu: (i, 0, 0)),
            smem(lambda i, be, nu: (jnp.minimum(i + 1, nblk - 1), 0, 0)),
            pl.BlockSpec(memory_space=pl.ANY),
            pl.BlockSpec((1, d, f), lambda i, be, nu: (be[i], 0, 0)),
            pl.BlockSpec((1, d, f), lambda i, be, nu: (be[i], 0, 0)),
            pl.BlockSpec((1, f, d), lambda i, be, nu: (be[i], 0, 0)),
        ],
        out_specs=pl.BlockSpec((MOE_BLOCK, d), lambda i, be, nu: (i, 0)),
        scratch_shapes=[pltpu.VMEM((2, MOE_BLOCK, d), F32), pltpu.SemaphoreType.DMA((2,))],
    )
    return pl.pallas_call(
        functools.partial(_expert_kernel, chunks=_ff_chunks(f)),
        grid_spec=grid_spec,
        out_shape=jax.ShapeDtypeStruct((nblk * MOE_BLOCK, d), F32),
        compiler_params=_params("arbitrary"),
        name="moe_experts",
    )(blk_e, n_used, tok3, tok3, u_flat, w1, w3, w2)


def _combine_kernel(dst_ref, dnext_ref, h_ref, gate_ref, mod_ref, *rest, final):
    if final:
        gf_ref, y_hbm, o_ref, ybuf, sem = rest
    else:
        y_hbm, o_ref, ybuf, sem = rest
    i = pl.program_id(0)
    n = pl.num_programs(0)
    tc = h_ref.shape[0]

    def issue(idx_ref, slot):
        def body(r, c):
            for k in range(TOP_K):
                pltpu.make_async_copy(y_hbm.at[pl.ds(idx_ref[0, 0, TOP_K * r + k], 1), :],
                                      ybuf.at[slot, k, pl.ds(r, 1), :], sem.at[slot]).start()
            return c
        lax.fori_loop(0, tc, body, 0)

    @pl.when(i == 0)
    def _():
        issue(dst_ref, 0)

    slot = i % 2

    @pl.when(i + 1 < n)
    def _():
        issue(dnext_ref, 1 - slot)

    for k in range(TOP_K):
        pltpu.make_async_copy(y_hbm.at[pl.ds(0, tc), :], ybuf.at[slot, k], sem.at[slot]).wait()

    gate = gate_ref[...]
    y = gate[:, 0:1] * ybuf[slot, 0] + gate[:, 1:2] * ybuf[slot, 1]
    h = h_ref[...] + mod_ref[0, 5:6, :] * y
    if final:
        h = h * lax.rsqrt(jnp.mean(h * h, axis=-1, keepdims=True) + EPS) * gf_ref[...]
    o_ref[...] = h


def _combine(h_flat, y, dest, gates, mod, mod_row, tokens_per_batch, final_g):
    n_tok, d = h_flat.shape
    tc = COMBINE_TILE
    steps = n_tok // tc
    per_b = tokens_per_batch // tc
    dst2 = dest.reshape(steps, 1, TOP_K * tc)
    smem = lambda fn: pl.BlockSpec((1, 1, TOP_K * tc), fn, memory_space=pltpu.SMEM)
    in_specs = [
        smem(lambda i: (i, 0, 0)),
        smem(lambda i: (jnp.minimum(i + 1, steps - 1), 0, 0)),
        pl.BlockSpec((tc, d), lambda i: (i, 0)),
        pl.BlockSpec((tc, TOP_K), lambda i: (i, 0)),
        pl.BlockSpec((1, 6, d), (lambda i: (i // per_b, 0, 0)) if mod_row is None
                     else (lambda i: (mod_row, 0, 0))),
    ]
    args = [dst2, dst2, h_flat, gates, mod]
    if final_g is not None:
        in_specs.append(pl.BlockSpec((1, d), lambda i: (0, 0)))
        args.append(final_g)
    in_specs.append(pl.BlockSpec(memory_space=pl.ANY))
    args.append(y)
    return pl.pallas_call(
        functools.partial(_combine_kernel, final=final_g is not None),
        grid=(steps,),
        in_specs=in_specs,
        out_specs=pl.BlockSpec((tc, d), lambda i: (i, 0)),
        out_shape=jax.ShapeDtypeStruct((n_tok, d), F32),
        scratch_shapes=[pltpu.VMEM((2, TOP_K, tc, d), F32), pltpu.SemaphoreType.DMA((2,))],
        compiler_params=_params("arbitrary"),
        name="moe_combine",
    )(*args)


def _routing(top_i, n_exp):
    e = top_i.reshape(-1)
    a = e.shape[0]
    experts = jnp.arange(n_exp, dtype=jnp.int32)
    onehot = (e[:, None] == experts[None, :]).astype(jnp.int32)
    csum = jnp.cumsum(onehot, axis=0)
    rank = jnp.sum(onehot * csum, axis=1) - 1
    counts = csum[-1]
    padded = (counts + MOE_BLOCK - 1) // MOE_BLOCK * MOE_BLOCK
    pend = jnp.cumsum(padded)
    pstart = pend - padded
    start = jnp.cumsum(counts) - counts
    dest = pstart[e] + rank
    cap = -(-a // MOE_BLOCK) * MOE_BLOCK + n_exp * MOE_BLOCK
    nblk = cap // MOE_BLOCK
    blk_first = jnp.arange(nblk, dtype=jnp.int32) * MOE_BLOCK
    blk_e = jnp.minimum(jnp.sum((pend[None, :] <= blk_first[:, None]).astype(jnp.int32), axis=1), n_exp - 1)
    n_used = (pend[-1] // MOE_BLOCK).astype(jnp.int32).reshape(1)
    order = jnp.argsort(e).astype(jnp.int32)
    row = jnp.arange(cap, dtype=jnp.int32)
    row_e = jnp.repeat(blk_e, MOE_BLOCK)
    within = row - pstart[row_e]
    valid = (within < counts[row_e]) & (row < pend[-1])
    tok_buf = jnp.where(valid, order[jnp.clip(start[row_e] + within, 0, a - 1)] // TOP_K, 0)
    return dest.astype(jnp.int32), tok_buf.astype(jnp.int32), blk_e.astype(jnp.int32), n_used


def _moe_mixer(h_mid, u, top_i, top_g, mod, mod_row, moe_w, n_exp, final_g):
    b, t, d = h_mid.shape
    n_tok = b * t
    dest, tok_buf, blk_e, n_used = _routing(top_i, n_exp)
    y = _experts(u.reshape(n_tok, d), tok_buf, blk_e, n_used, *moe_w)
    out = _combine(h_mid.reshape(n_tok, d), y, dest, top_g.reshape(n_tok, TOP_K), mod, mod_row, t, final_g)
    return out.reshape(b, t, d)


def _layer_weights(w_in_l, conv_w_l, b_gates_l, w_out_l):
    cd = conv_w_l.shape[1]
    n_gate = b_gates_l.size
    md = (w_in_l.shape[1] - 3 * cd - n_gate) // 4
    q0 = 3 * cd
    g0 = q0 + 3 * md
    wg = jnp.pad(w_in_l[:, g0:g0 + n_gate], ((0, 0), (0, LANES - n_gate)))
    return {
        "wc": w_in_l[:, :q0].astype(BF16),
        "wq": w_in_l[:, q0:q0 + md].astype(BF16),
        "wk": w_in_l[:, q0 + md:q0 + 2 * md].astype(BF16),
        "wv": w_in_l[:, q0 + 2 * md:g0].astype(BF16),
        "wg": wg.astype(BF16),
        "bg": b_gates_l.reshape(1, n_gate),
        "wo": w_in_l[:, g0 + n_gate:].astype(BF16),
        "cw": conv_w_l,
        "woc": w_out_l[:cd].astype(BF16),
        "wom": w_out_l[cd:].astype(BF16),
    }


def kernel(x, c, ctx, c_ctx, norm1_g, norm2_g, w_ada, b_ada, w_in, conv_w, b_gates, mlstm_norm_g, w_out,
           ffn_w1, ffn_w3, ffn_w2, moe_router, moe_w1, moe_w3, moe_w2, final_norm_g):
    b, t, d = x.shape
    tc = ctx.shape[1]
    depth = w_in.shape[0]
    n_exp = moe_router.shape[-1]
    assert t % GRID_W == 0 and t % CHUNK == 0 and tc % CHUNK == 0

    ctx_row = b
    n_rows = -(-(b + 1) // SUBLANES) * SUBLANES
    c_all = jnp.concatenate([c, c_ctx[None, :], jnp.zeros((n_rows - b - 1, d), F32)], axis=0)
    mod_all = _ada(c_all, w_ada, b_ada).reshape(depth, n_rows, 6, d)
    final_g = final_norm_g.reshape(1, d)

    h, hc = x, ctx
    for layer in range(depth):
        last = layer == depth - 1
        j = layer // 2
        mod = mod_all[layer]
        wts = _layer_weights(w_in[layer], conv_w[layer], b_gates[layer], w_out[layer])
        n1 = norm1_g[layer].reshape(1, d)
        n2 = norm2_g[layer].reshape(1, d)

        ql, kl, vl, gl, ycl, ogl = _in_proj(h, mod, None, n1, wts, GRID_W, True)
        res_c = _in_proj(hc, mod, ctx_row, n1, wts, tc, not last)
        qc, kc, vc, gc = res_c[:4]
        ogc = None if last else res_c[5]
        ymc, yml = _mlstm(gc, gl, (qc, kc, vc), (ql, kl, vl), ogc, ogl,
                          mlstm_norm_g[layer].reshape(1, -1), not last)

        def mixer(hh, yc_, ym_, mod_row, fin):
            if layer % 2 == 0:
                ffn = (ffn_w1[j].astype(BF16), ffn_w3[j].astype(BF16), ffn_w2[j].astype(BF16))
                return _out_dense(hh, yc_, ym_, mod, mod_row, n2, wts, ffn, fin)
            r_full = jnp.pad(moe_router[j], ((0, 0), (0, LANES - n_exp)))
            r_hi = r_full.astype(BF16)
            r_lo = (r_full - r_hi.astype(F32)).astype(BF16)
            router_pad = jnp.concatenate([r_hi, r_lo], axis=1)
            h_mid, u2, top_i, top_g = _out_moe(hh, yc_, ym_, mod, mod_row, n2, wts, router_pad, n_exp)
            moe_w = (moe_w1[j].astype(BF16), moe_w3[j].astype(BF16), moe_w2[j].astype(BF16))
            return _moe_mixer(h_mid, u2, top_i, top_g, mod, mod_row, moe_w, n_exp, fin)

        h = mixer(h, ycl, yml, None, final_g if last else None)
        if not last:
            hc = mixer(hc, res_c[4], ymc, ctx_row, None)
    return h
```

```python
import functools

import jax
import jax.numpy as jnp
from jax import lax
from jax.experimental import pallas as pl
from jax.experimental.pallas import tpu as pltpu

GRID_W = 64
CHUNK = 128
EPS = 1e-6
TOP_K = 2
ROW_TILE = 512
MOE_BLOCK = 256
COMBINE_TILE = 256
LANES = 128
SUBLANES = 8
VMEM_LIMIT = 56 * 1024 * 1024

F32 = jnp.float32
BF16 = jnp.bfloat16
HIGHEST = lax.Precision.HIGHEST


def _dot(a, b):
    return jnp.dot(a, b, preferred_element_type=F32)


def _dot_nt(a, b):
    return lax.dot_general(a, b, (((1,), (1,)), ((), ())), preferred_element_type=F32)


def _dot_tn(a, b):
    return lax.dot_general(a, b, (((0,), (0,)), ((), ())), preferred_element_type=F32)


def _rows_to_tiles(ref, x):
    n, d = x.shape
    per = d // LANES
    for s in range(per):
        ref[pl.ds(s, n, stride=per), :] = x[:, s * LANES:(s + 1) * LANES]


def _tiles_to_rows(ref, n, per):
    return jnp.concatenate([ref[pl.ds(s, n, stride=per), :] for s in range(per)], axis=1)


def _params(*sem):
    return pltpu.CompilerParams(dimension_semantics=sem, vmem_limit_bytes=VMEM_LIMIT)


def _ff_chunks(d_ff, max_chunk=1024):
    out, c0 = [], 0
    while c0 < d_ff:
        c1 = min(c0 + max_chunk, d_ff)
        out.append((c0, c1))
        c0 = c1
    return out


def _ada_kernel(c_ref, w_ref, b_ref, o_ref):
    s = c_ref[...]
    s = s * jax.nn.sigmoid(s)
    o_ref[0] = jnp.dot(s, w_ref[0], precision=HIGHEST, preferred_element_type=F32) + b_ref[0]


def _ada(c_all, w_ada, b_ada):
    depth, d, d6 = w_ada.shape
    rows = c_all.shape[0]
    tn = d6 // 4
    return pl.pallas_call(
        _ada_kernel,
        grid=(depth, d6 // tn),
        in_specs=[
            pl.BlockSpec((rows, d), lambda l, j: (0, 0)),
            pl.BlockSpec((1, d, tn), lambda l, j: (l, 0, j)),
            pl.BlockSpec((1, 1, tn), lambda l, j: (l, 0, j)),
        ],
        out_specs=pl.BlockSpec((1, rows, tn), lambda l, j: (l, 0, j)),
        out_shape=jax.ShapeDtypeStruct((depth, rows, d6), F32),
        compiler_params=_params("parallel", "parallel"),
        name="adaln_mod",
    )(c_all, w_ada, b_ada.reshape(depth, 1, d6))


def _norm_mod(x, g, shift, scale):
    y = x * lax.rsqrt(jnp.mean(x * x, axis=-1, keepdims=True) + EPS)
    return (y * g) * (1.0 + scale) + shift


def _in_kernel(x_ref, mod_ref, g_ref, wq_ref, wk_ref, wv_ref, wg_ref, bg_ref, *rest,
               row_w, full, kscale, n_gate):
    if full:
        wc_ref, wo_ref, cw_ref, q_ref, k_ref, v_ref, gt_ref, yc_ref, og_ref = rest
    else:
        q_ref, k_ref, v_ref, gt_ref = rest
    x = x_ref[0]
    u = _norm_mod(x, g_ref[...], mod_ref[0, 0:1, :], mod_ref[0, 1:2, :]).astype(BF16)
    q_ref[0] = _dot_nt(wq_ref[...], u).astype(BF16)
    k_ref[0] = (_dot(u, wk_ref[...]) * kscale).astype(BF16)
    v_ref[0] = _dot_nt(wv_ref[...], u).astype(BF16)
    gt_ref[0] = _dot(u, wg_ref[...])[:, :n_gate] + bg_ref[...]
    if full:
        cd = cw_ref.shape[1]
        c3 = _dot(u, wc_ref[...])
        z = c3[:, cd:2 * cd] * c3[:, 2 * cd:]
        tm = z.shape[0]
        t = lax.broadcasted_iota(jnp.int32, (tm, 1), 0) % row_w
        zprev = jnp.where(t == 0, 0.0, pltpu.roll(z, 1, axis=0))
        znext = jnp.where(t == row_w - 1, 0.0, pltpu.roll(z, tm - 1, axis=0))
        conv = zprev * cw_ref[0:1, :] + z * cw_ref[1:2, :] + znext * cw_ref[2:3, :]
        yc_ref[0] = (c3[:, :cd] * conv).astype(BF16)
        og_ref[0] = jax.nn.sigmoid(_dot_nt(wo_ref[...], u)).astype(BF16)


def _in_proj(h, mod, mod_row, norm_g, wts, row_w, full):
    b, t, d = h.shape
    tm = min(ROW_TILE, t)
    md = wts["wk"].shape[1]
    n_gate = wts["bg"].shape[1]
    head_dim = md // (n_gate // 4)
    const = lambda shape: pl.BlockSpec(shape, lambda i, j: (0,) * len(shape))
    tile = lambda w: pl.BlockSpec((1, tm, w), lambda i, j: (i, j, 0))
    tile_t = pl.BlockSpec((1, md, tm), lambda i, j: (i, 0, j))
    feat_major = jax.ShapeDtypeStruct((b, md, t), BF16)
    in_specs = [
        tile(d),
        pl.BlockSpec((1, 6, d), (lambda i, j: (i, 0, 0)) if mod_row is None
                     else (lambda i, j: (mod_row, 0, 0))),
        const((1, d)),
        const(wts["wq"].shape), const(wts["wk"].shape), const(wts["wv"].shape),
        const(wts["wg"].shape), const(wts["bg"].shape),
    ]
    args = [h, mod, norm_g, wts["wq"], wts["wk"], wts["wv"], wts["wg"], wts["bg"]]
    out_specs = [tile_t, tile(md), tile_t, tile(n_gate)]
    out_shape = [feat_major, jax.ShapeDtypeStruct((b, t, md), BF16), feat_major,
                 jax.ShapeDtypeStruct((b, t, n_gate), F32)]
    if full:
        cd = wts["cw"].shape[1]
        in_specs += [const(wts["wc"].shape), const(wts["wo"].shape), const(wts["cw"].shape)]
        args += [wts["wc"], wts["wo"], wts["cw"]]
        out_specs += [tile(cd), tile_t]
        out_shape += [jax.ShapeDtypeStruct((b, t, cd), BF16), feat_major]
    return pl.pallas_call(
        functools.partial(_in_kernel, row_w=row_w, full=full, kscale=head_dim ** -0.5, n_gate=n_gate),
        grid=(b, t // tm),
        in_specs=in_specs,
        out_specs=out_specs,
        out_shape=out_shape,
        compiler_params=_params("parallel", "parallel"),
        name="in_proj",
    )(*args)


def _lane_scan(x, op, ident, reverse):
    n = x.shape[1]
    lane = lax.broadcasted_iota(jnp.int32, x.shape, 1)
    s = 1
    while s < n:
        if reverse:
            shifted = jnp.where(lane < n - s, pltpu.roll(x, n - s, axis=1), ident)
        else:
            shifted = jnp.where(lane >= s, pltpu.roll(x, s, axis=1), ident)
        x = op(x, shifted)
        s *= 2
    return x


def _mlstm_kernel(gr_ref, kc_ref, qc_ref, vc_ref, kl_ref, ql_ref, vl_ref, ogl_ref, gain_ref,
                  *rest, need_ctx):
    if need_ctx:
        ogc_ref, yc_ref, yl_ref, rt_ref, ra_ref, rp_ref, rw_ref, hc_ref, hl_ref = rest
    else:
        yl_ref, rt_ref, ra_ref, rp_ref, rw_ref, hc_ref, hl_ref = rest
        ogc_ref = yc_ref = None
    L = CHUNK
    hd = kl_ref.shape[2]
    ncc = kc_ref.shape[1] // L
    ncl = kl_ref.shape[1] // L
    nct = ncc + ncl

    head = pl.program_id(1)
    n_heads = gr_ref.shape[1]

    @pl.when(head == 0)
    def _():
        for d in range(2):
            rev = d == 1
            last = 0 if rev else L - 1
            for hh in range(n_heads):
                ig = gr_ref[0, hh, 2 * d]
                lf = jax.nn.log_sigmoid(gr_ref[0, hh, 2 * d + 1])
                a = _lane_scan(lf, jnp.add, 0.0, rev)
                r = ig - a
                p = _lane_scan(r, jnp.maximum, -jnp.inf, rev)
                ra_ref[d, hh, 0:nct, :] = a
                rp_ref[d, hh, 0:nct, :] = p
                rw_ref[d, hh, 0:nct, :] = jnp.exp(r - p[:, last:last + 1])
                r_pad = jnp.concatenate([r, jnp.zeros((L - nct, L), F32)], axis=0)
                rt_ref[d, hh] = r_pad.T

    spos = lax.broadcasted_iota(jnp.int32, (L, L), 0)
    tpos = lax.broadcasted_iota(jnp.int32, (L, L), 1)
    ones_row = jnp.where(lax.broadcasted_iota(jnp.int32, (hd, L), 0) == 0, 1.0, 0.0).astype(BF16)
    gain = gain_ref[...]

    def chunk_step(d, seq, j, g, caug, m, mode):
        k_ref, q_ref, v_ref, h_ref, og_ref, y_ref = seq
        off = j * L
        last = 0 if d == 1 else L - 1
        k = k_ref[0, pl.ds(off, L), :]
        vaug = jnp.concatenate([v_ref[0, :, pl.ds(off, L)], ones_row], axis=0)
        a_row = ra_ref[d, head, g:g + 1, :]
        p_row = rp_ref[d, head, g:g + 1, :]
        b_tot = a_row[:, last:last + 1]
        p_last = p_row[:, last:last + 1]
        if mode is not None:
            q = q_ref[0, :, pl.ds(off, L)]
            r_col = rt_ref[d, head, :, g:g + 1]
            mask = (spos >= tpos) if d == 1 else (spos <= tpos)
            w = jnp.where(mask, jnp.exp(r_col - p_row), 0.0)
            intra = _dot(vaug, (_dot(k, q) * w).astype(BF16))
            inter = _dot(caug.astype(BF16), q)
            m_row = jnp.maximum(p_row, m)
            nd = jnp.exp(p_row - m_row) * intra + jnp.exp(m - m_row) * inter
            den = jnp.maximum(jnp.abs(nd[hd:hd + 1, :]), jnp.exp(-(a_row + m_row)))
            hout = nd[:hd, :] * (1.0 / den)
            if mode == "store":
                h_ref[:, pl.ds(off, L)] = hout
            else:
                hh = h_ref[:, pl.ds(off, L)] + hout
                hn = hh * lax.rsqrt(jnp.mean(hh * hh, axis=0, keepdims=True) + EPS)
                og = og_ref[0, :, pl.ds(off, L)].astype(F32)
                y_ref[0, :, pl.ds(off, L)] = (hn * gain * og).astype(BF16)
        m_last = jnp.maximum(p_last, m)
        vw = (vaug.astype(F32) * rw_ref[d, head, g:g + 1, :]).astype(BF16)
        caug = jnp.exp(m - m_last) * caug + jnp.exp(p_last - m_last) * _dot(vw, k)
        return caug, b_tot + m_last

    def pair_step(seq, n, g0, i, st, mode):
        cf, mf, cb, mb = st
        cf, mf = chunk_step(0, seq, i, g0 + i, cf, mf, mode)
        cb, mb = chunk_step(1, seq, n - 1 - i, g0 + n - 1 - i, cb, mb, mode)
        return cf, mf, cb, mb

    ctx_seq = (kc_ref, qc_ref, vc_ref, hc_ref, ogc_ref, yc_ref)
    lat_seq = (kl_ref, ql_ref, vl_ref, hl_ref, ogl_ref, yl_ref)
    zc, zm = jnp.zeros((2 * hd, hd), F32), jnp.zeros((1, 1), F32)
    st = (zc, zm, zc, zm)
    for i in range(ncc):
        mode = None if not need_ctx else ("store" if i < ncc // 2 else "final")
        st = pair_step(ctx_seq, ncc, 0, i, st, mode)
    for i in range(ncl):
        st = pair_step(lat_seq, ncl, ncc, i, st, "store" if i < ncl // 2 else "final")


def _mlstm(gates_c, gates_l, kqv_c, kqv_l, og_c, og_l, gain, need_ctx):
    b, tc, md = kqv_c[0].shape
    tl = kqv_l[0].shape[1]
    n_gate = gates_c.shape[2]
    heads = n_gate // 4
    hd = md // heads
    L = CHUNK
    nct = (tc + tl) // L
    assert tc // L % 2 == 0 and tl // L % 2 == 0 and nct <= L
    nrow = -(-nct // SUBLANES) * SUBLANES
    g = jnp.concatenate([gates_c, gates_l], axis=1).reshape(b, nct, L, 4, heads)
    g = g.transpose(0, 4, 3, 1, 2)
    tok = lambda t: pl.BlockSpec((1, t, hd), lambda i, h: (i, 0, h))
    feat = lambda t: pl.BlockSpec((1, hd, t), lambda i, h: (i, h, 0))
    in_specs = [pl.BlockSpec((1, heads, 4, nct, L), lambda i, h: (i, 0, 0, 0, 0)),
                tok(tc), feat(tc), feat(tc), tok(tl), feat(tl), feat(tl), feat(tl),
                pl.BlockSpec((hd, 1), lambda i, h: (h, 0))]
    args = [g, *kqv_c, *kqv_l, og_l, gain.reshape(md, 1)]
    out_specs = [feat(tl)]
    out_shape = [jax.ShapeDtypeStruct((b, md, tl), BF16)]
    if need_ctx:
        in_specs.append(feat(tc))
        args.append(og_c)
        out_specs = [feat(tc)] + out_specs
        out_shape = [jax.ShapeDtypeStruct((b, md, tc), BF16)] + out_shape
    res = pl.pallas_call(
        functools.partial(_mlstm_kernel, need_ctx=need_ctx),
        grid=(b, heads),
        in_specs=in_specs,
        out_specs=out_specs,
        out_shape=out_shape,
        scratch_shapes=[
            pltpu.VMEM((2, heads, L, L), F32),
            pltpu.VMEM((2, heads, nrow, L), F32),
            pltpu.VMEM((2, heads, nrow, L), F32),
            pltpu.VMEM((2, heads, nrow, L), F32),
            pltpu.VMEM((hd, tc), F32),
            pltpu.VMEM((hd, tl), F32),
        ],
        compiler_params=_params("parallel", "arbitrary"),
        name="mlstm_scan",
    )(*args)
    return (res[0], res[1]) if need_ctx else (None, res[0])


def _mix_residual(h_ref, yc_ref, ym_ref, mod_ref, woc_ref, wom_ref):
    mix = _dot(yc_ref[0], woc_ref[...]) + _dot_tn(ym_ref[0], wom_ref[...])
    return h_ref[0] + mod_ref[0, 2:3, :] * mix


def _out_dense_kernel(h_ref, yc_ref, ym_ref, mod_ref, g2_ref, woc_ref, wom_ref, w1_ref, w3_ref, w2_ref,
                      *rest, chunks, final):
    if final:
        gf_ref, o_ref = rest
    else:
        (o_ref,) = rest
    h = _mix_residual(h_ref, yc_ref, ym_ref, mod_ref, woc_ref, wom_ref)
    u = _norm_mod(h, g2_ref[...], mod_ref[0, 3:4, :], mod_ref[0, 4:5, :]).astype(BF16)
    acc = None
    for c0, c1 in chunks:
        a = _dot(u, w1_ref[:, c0:c1])
        hid = (a * jax.nn.sigmoid(a) * _dot(u, w3_ref[:, c0:c1])).astype(BF16)
        part = _dot(hid, w2_ref[c0:c1, :])
        acc = part if acc is None else acc + part
    h = h + mod_ref[0, 5:6, :] * acc
    if final:
        h = h * lax.rsqrt(jnp.mean(h * h, axis=-1, keepdims=True) + EPS) * gf_ref[...]
    o_ref[0] = h


def _out_moe_kernel(h_ref, yc_ref, ym_ref, mod_ref, g2_ref, woc_ref, wom_ref, router_ref,
                    hm_ref, u_ref, ti_ref, tg_ref, *, n_exp):
    h = _mix_residual(h_ref, yc_ref, ym_ref, mod_ref, woc_ref, wom_ref)
    hm_ref[0] = h
    u = _norm_mod(h, g2_ref[...], mod_ref[0, 3:4, :], mod_ref[0, 4:5, :])
    _rows_to_tiles(u_ref, u)
    u_hi = u.astype(BF16)
    u_lo = (u - u_hi.astype(F32)).astype(BF16)
    parts = _dot(u_hi, router_ref[...]) + _dot(u_lo, router_ref[...])
    logits = parts[:, :LANES] + parts[:, LANES:]
    lane = lax.broadcasted_iota(jnp.int32, logits.shape, 1)
    logits = jnp.where(lane < n_exp, logits, -jnp.inf)
    v1 = jnp.max(logits, axis=-1, keepdims=True)
    i1 = jnp.min(jnp.where(logits == v1, lane, LANES), axis=-1, keepdims=True)
    rest = jnp.where(lane == i1, -jnp.inf, logits)
    v2 = jnp.max(rest, axis=-1, keepdims=True)
    i2 = jnp.min(jnp.where(rest == v2, lane, LANES), axis=-1, keepdims=True)
    e2 = jnp.exp(v2 - v1)
    inv = 1.0 / (1.0 + e2)
    first = lax.broadcasted_iota(jnp.int32, ti_ref.shape[1:], 1) == 0
    ti_ref[0] = jnp.where(first, i1, i2)
    tg_ref[0] = jnp.where(first, inv, e2 * inv)


def _out_common(h, yc, ym, mod, mod_row, norm_g, wts):
    b, t, d = h.shape
    tm = min(ROW_TILE, t)
    const = lambda shape: pl.BlockSpec(shape, lambda i, j: (0,) * len(shape))
    tile = lambda w: pl.BlockSpec((1, tm, w), lambda i, j: (i, j, 0))
    in_specs = [
        tile(d), tile(yc.shape[2]), pl.BlockSpec((1, ym.shape[1], tm), lambda i, j: (i, 0, j)),
        pl.BlockSpec((1, 6, d), (lambda i, j: (i, 0, 0)) if mod_row is None
                     else (lambda i, j: (mod_row, 0, 0))),
        const((1, d)), const(wts["woc"].shape), const(wts["wom"].shape),
    ]
    args = [h, yc, ym, mod, norm_g, wts["woc"], wts["wom"]]
    return b, t, d, tm, const, tile, in_specs, args


def _out_dense(h, yc, ym, mod, mod_row, norm_g, wts, ffn, final_g):
    b, t, d, tm, const, tile, in_specs, args = _out_common(h, yc, ym, mod, mod_row, norm_g, wts)
    w1, w3, w2 = ffn
    in_specs += [const(w1.shape), const(w3.shape), const(w2.shape)]
    args += [w1, w3, w2]
    if final_g is not None:
        in_specs.append(const((1, d)))
        args.append(final_g)
    return pl.pallas_call(
        functools.partial(_out_dense_kernel, chunks=_ff_chunks(w1.shape[1]), final=final_g is not None),
        grid=(b, t // tm),
        in_specs=in_specs,
        out_specs=tile(d),
        out_shape=jax.ShapeDtypeStruct((b, t, d), F32),
        compiler_params=_params("parallel", "parallel"),
        name="out_dense_ffn",
    )(*args)


def _out_moe(h, yc, ym, mod, mod_row, norm_g, wts, router_pad, n_exp):
    b, t, d, tm, const, tile, in_specs, args = _out_common(h, yc, ym, mod, mod_row, norm_g, wts)
    in_specs.append(const(router_pad.shape))
    args.append(router_pad)
    per = d // LANES
    return pl.pallas_call(
        functools.partial(_out_moe_kernel, n_exp=n_exp),
        grid=(b, t // tm),
        in_specs=in_specs,
        out_specs=[tile(d), pl.BlockSpec((tm * per, LANES), lambda i, j: (i * (t // tm) + j, 0)),
                   tile(TOP_K), tile(TOP_K)],
        out_shape=[jax.ShapeDtypeStruct((b, t, d), F32), jax.ShapeDtypeStruct((b * t * per, LANES), F32),
                   jax.ShapeDtypeStruct((b, t, TOP_K), jnp.int32), jax.ShapeDtypeStruct((b, t, TOP_K), F32)],
        compiler_params=_params("parallel", "parallel"),
        name="out_router",
    )(*args)


ISSUE_UNROLL = 8


def _expert_kernel(be_ref, nu_ref, tok_ref, toknext_ref, u_hbm, w1_ref, w3_ref, w2_ref, y_ref, xbuf, sem,
                   *, chunks, per):
    i = pl.program_id(0)
    n_used = nu_ref[0]
    blk = xbuf.shape[1] // per
    slot = i % 2

    def row_copy(idx_ref, r, to_slot):
        src = pl.multiple_of(idx_ref[0, 0, r], per)
        return pltpu.make_async_copy(u_hbm.at[pl.ds(src, per), :],
                                     xbuf.at[to_slot, pl.ds(r * per, per), :], sem.at[to_slot])

    @pl.when(i == 0)
    def _():
        def body(r, c):
            row_copy(tok_ref, r, 0).start()
            return c
        lax.fori_loop(0, blk, body, 0, unroll=ISSUE_UNROLL)

    @pl.when(i <= n_used)
    def _():
        pltpu.make_async_copy(u_hbm.at[pl.ds(0, blk * per), :], xbuf.at[slot], sem.at[slot]).wait()

    @pl.when(i < n_used)
    def _():
        for r in range(blk):
            row_copy(toknext_ref, r, 1 - slot).start()
        x = _tiles_to_rows(xbuf.at[slot], blk, per).astype(BF16)
        acc = None
        for c0, c1 in chunks:
            a = _dot(x, w1_ref[0, :, c0:c1])
            hid = (a * jax.nn.sigmoid(a) * _dot(x, w3_ref[0, :, c0:c1])).astype(BF16)
            part = _dot(hid, w2_ref[0, c0:c1, :])
            acc = part if acc is None else acc + part
        _rows_to_tiles(y_ref, acc)

    @pl.when(i >= n_used)
    def _():
        y_ref[...] = jnp.zeros(y_ref.shape, F32)


def _experts(u_tiles, tok_buf, blk_e, n_used, w1, w3, w2):
    d, f = w1.shape[1], w1.shape[2]
    per = d // LANES
    nblk = tok_buf.shape[0] // MOE_BLOCK
    tok3 = (tok_buf * per).reshape(nblk, 1, MOE_BLOCK)
    smem = lambda fn: pl.BlockSpec((1, 1, MOE_BLOCK), fn, memory_space=pltpu.SMEM)
    grid_spec = pltpu.PrefetchScalarGridSpec(
        num_scalar_prefetch=2,
        grid=(nblk,),
        in_specs=[
            smem(lambda i, be, nu: (i, 0, 0)),
            smem(lambda i, be, nu: (jnp.minimum(i + 1, nblk - 1), 0, 0)),
            pl.BlockSpec(memory_space=pl.ANY),
            pl.BlockSpec((1, d, f), lambda i, be, nu: (be[i], 0, 0)),
            pl.BlockSpec((1, d, f), lambda i, be, nu: (be[i], 0, 0)),
            pl.BlockSpec((1, f, d), lambda i, be, nu: (be[i], 0, 0)),
        ],
        out_specs=pl.BlockSpec((MOE_BLOCK * per, LANES), lambda i, be, nu: (i, 0)),
        scratch_shapes=[pltpu.VMEM((2, MOE_BLOCK * per, LANES), F32), pltpu.SemaphoreType.DMA((2,))],
    )
    return pl.pallas_call(
        functools.partial(_expert_kernel, chunks=_ff_chunks(f), per=per),
        grid_spec=grid_spec,
        out_shape=jax.ShapeDtypeStruct((nblk * MOE_BLOCK * per, LANES), F32),
        compiler_params=_params("arbitrary"),
        name="moe_experts",
    )(blk_e, n_used, tok3, tok3, u_tiles, w1, w3, w2)


def _combine_kernel(dst_ref, dnext_ref, h_ref, gate_ref, mod_ref, *rest, final):
    if final:
        gf_ref, y_hbm, o_ref, ybuf, sem = rest
    else:
        y_hbm, o_ref, ybuf, sem = rest
    i = pl.program_id(0)
    n = pl.num_programs(0)
    tc, d = h_ref.shape
    per = d // LANES

    def issue(idx_ref, slot):
        def body(r, c):
            for k in range(TOP_K):
                src = pl.multiple_of(idx_ref[0, 0, TOP_K * r + k], per)
                pltpu.make_async_copy(y_hbm.at[pl.ds(src, per), :],
                                      ybuf.at[slot, k, pl.ds(pl.multiple_of(r * per, per), per), :],
                                      sem.at[slot]).start()
            return c
        lax.fori_loop(0, tc, body, 0, unroll=ISSUE_UNROLL)

    @pl.when(i == 0)
    def _():
        issue(dst_ref, 0)

    slot = i % 2

    @pl.when(i + 1 < n)
    def _():
        issue(dnext_ref, 1 - slot)

    for k in range(TOP_K):
        pltpu.make_async_copy(y_hbm.at[pl.ds(0, tc * per), :], ybuf.at[slot, k], sem.at[slot]).wait()

    gate = gate_ref[...]
    y = (gate[:, 0:1] * _tiles_to_rows(ybuf.at[slot, 0], tc, per)
         + gate[:, 1:2] * _tiles_to_rows(ybuf.at[slot, 1], tc, per))
    h = h_ref[...] + mod_ref[0, 5:6, :] * y
    if final:
        h = h * lax.rsqrt(jnp.mean(h * h, axis=-1, keepdims=True) + EPS) * gf_ref[...]
    o_ref[...] = h


def _combine(h_flat, y, dest, gates, mod, mod_row, tokens_per_batch, final_g):
    n_tok, d = h_flat.shape
    tc = COMBINE_TILE
    steps = n_tok // tc
    per_b = tokens_per_batch // tc
    per = d // LANES
    dst2 = (dest * per).reshape(steps, 1, TOP_K * tc)
    smem = lambda fn: pl.BlockSpec((1, 1, TOP_K * tc), fn, memory_space=pltpu.SMEM)
    in_specs = [
        smem(lambda i: (i, 0, 0)),
        smem(lambda i: (jnp.minimum(i + 1, steps - 1), 0, 0)),
        pl.BlockSpec((tc, d), lambda i: (i, 0)),
        pl.BlockSpec((tc, TOP_K), lambda i: (i, 0)),
        pl.BlockSpec((1, 6, d), (lambda i: (i // per_b, 0, 0)) if mod_row is None
                     else (lambda i: (mod_row, 0, 0))),
    ]
    args = [dst2, dst2, h_flat, gates, mod]
    if final_g is not None:
        in_specs.append(pl.BlockSpec((1, d), lambda i: (0, 0)))
        args.append(final_g)
    in_specs.append(pl.BlockSpec(memory_space=pl.ANY))
    args.append(y)
    return pl.pallas_call(
        functools.partial(_combine_kernel, final=final_g is not None),
        grid=(steps,),
        in_specs=in_specs,
        out_specs=pl.BlockSpec((tc, d), lambda i: (i, 0)),
        out_shape=jax.ShapeDtypeStruct((n_tok, d), F32),
        scratch_shapes=[pltpu.VMEM((2, TOP_K, tc * per, LANES), F32), pltpu.SemaphoreType.DMA((2,))],
        compiler_params=_params("arbitrary"),
        name="moe_combine",
    )(*args)


def _routing(top_i, n_exp):
    e = top_i.reshape(-1)
    a = e.shape[0]
    experts = jnp.arange(n_exp, dtype=jnp.int32)
    onehot = (e[:, None] == experts[None, :]).astype(jnp.int32)
    csum = jnp.cumsum(onehot, axis=0)
    rank = jnp.sum(onehot * csum, axis=1) - 1
    counts = csum[-1]
    padded = (counts + MOE_BLOCK - 1) // MOE_BLOCK * MOE_BLOCK
    pend = jnp.cumsum(padded)
    pstart = pend - padded
    start = jnp.cumsum(counts) - counts
    dest = pstart[e] + rank
    cap = -(-a // MOE_BLOCK) * MOE_BLOCK + n_exp * MOE_BLOCK
    nblk = cap // MOE_BLOCK
    blk_first = jnp.arange(nblk, dtype=jnp.int32) * MOE_BLOCK
    blk_e = jnp.minimum(jnp.sum((pend[None, :] <= blk_first[:, None]).astype(jnp.int32), axis=1), n_exp - 1)
    n_used = (pend[-1] // MOE_BLOCK).astype(jnp.int32).reshape(1)
    order = jnp.argsort(e).astype(jnp.int32)
    row = jnp.arange(cap, dtype=jnp.int32)
    row_e = jnp.repeat(blk_e, MOE_BLOCK)
    within = row - pstart[row_e]
    valid = (within < counts[row_e]) & (row < pend[-1])
    tok_buf = jnp.where(valid, order[jnp.clip(start[row_e] + within, 0, a - 1)] // TOP_K, 0)
    return dest.astype(jnp.int32), tok_buf.astype(jnp.int32), blk_e.astype(jnp.int32), n_used


def _moe_mixer(h_mid, u, top_i, top_g, mod, mod_row, moe_w, n_exp, final_g):
    b, t, d = h_mid.shape
    n_tok = b * t
    dest, tok_buf, blk_e, n_used = _routing(top_i, n_exp)
    y = _experts(u, tok_buf, blk_e, n_used, *moe_w)
    out = _combine(h_mid.reshape(n_tok, d), y, dest, top_g.reshape(n_tok, TOP_K), mod, mod_row, t, final_g)
    return out.reshape(b, t, d)


def _layer_weights(w_in_l, conv_w_l, b_gates_l, w_out_l):
    cd = conv_w_l.shape[1]
    n_gate = b_gates_l.size
    md = (w_in_l.shape[1] - 3 * cd - n_gate) // 4
    q0 = 3 * cd
    g0 = q0 + 3 * md
    wg = jnp.pad(w_in_l[:, g0:g0 + n_gate], ((0, 0), (0, LANES - n_gate)))
    return {
        "wc": w_in_l[:, :q0].astype(BF16),
        "wq": w_in_l[:, q0:q0 + md].T.astype(BF16),
        "wk": w_in_l[:, q0 + md:q0 + 2 * md].astype(BF16),
        "wv": w_in_l[:, q0 + 2 * md:g0].T.astype(BF16),
        "wg": wg.astype(BF16),
        "bg": b_gates_l.reshape(1, n_gate),
        "wo": w_in_l[:, g0 + n_gate:].T.astype(BF16),
        "cw": conv_w_l,
        "woc": w_out_l[:cd].astype(BF16),
        "wom": w_out_l[cd:].astype(BF16),
    }


def kernel(x, c, ctx, c_ctx, norm1_g, norm2_g, w_ada, b_ada, w_in, conv_w, b_gates, mlstm_norm_g, w_out,
           ffn_w1, ffn_w3, ffn_w2, moe_router, moe_w1, moe_w3, moe_w2, final_norm_g):
    b, t, d = x.shape
    tc = ctx.shape[1]
    depth = w_in.shape[0]
    n_exp = moe_router.shape[-1]
    assert t % GRID_W == 0 and t % CHUNK == 0 and tc % CHUNK == 0 and d % (SUBLANES * LANES) == 0

    ctx_row = b
    n_rows = -(-(b + 1) // SUBLANES) * SUBLANES
    c_all = jnp.concatenate([c, c_ctx[None, :], jnp.zeros((n_rows - b - 1, d), F32)], axis=0)
    mod_all = _ada(c_all, w_ada, b_ada).reshape(depth, n_rows, 6, d)
    final_g = final_norm_g.reshape(1, d)

    h, hc = x, ctx
    for layer in range(depth):
        last = layer == depth - 1
        j = layer // 2
        mod = mod_all[layer]
        wts = _layer_weights(w_in[layer], conv_w[layer], b_gates[layer], w_out[layer])
        n1 = norm1_g[layer].reshape(1, d)
        n2 = norm2_g[layer].reshape(1, d)

        ql, kl, vl, gl, ycl, ogl = _in_proj(h, mod, None, n1, wts, GRID_W, True)
        res_c = _in_proj(hc, mod, ctx_row, n1, wts, tc, not last)
        qc, kc, vc, gc = res_c[:4]
        ogc = None if last else res_c[5]
        ymc, yml = _mlstm(gc, gl, (kc, qc, vc), (kl, ql, vl), ogc, ogl,
                          mlstm_norm_g[layer].reshape(1, -1), not last)

        def mixer(hh, yc_, ym_, mod_row, fin):
            if layer % 2 == 0:
                ffn = (ffn_w1[j].astype(BF16), ffn_w3[j].astype(BF16), ffn_w2[j].astype(BF16))
                return _out_dense(hh, yc_, ym_, mod, mod_row, n2, wts, ffn, fin)
            r_full = jnp.pad(moe_router[j], ((0, 0), (0, LANES - n_exp)))
            r_hi = r_full.astype(BF16)
            r_lo = (r_full - r_hi.astype(F32)).astype(BF16)
            router_pad = jnp.concatenate([r_hi, r_lo], axis=1)
            h_mid, u2, top_i, top_g = _out_moe(hh, yc_, ym_, mod, mod_row, n2, wts, router_pad, n_exp)
            moe_w = (moe_w1[j].astype(BF16), moe_w3[j].astype(BF16), moe_w2[j].astype(BF16))
            return _moe_mixer(h_mid, u2, top_i, top_g, mod, mod_row, moe_w, n_exp, fin)

        h = mixer(h, ycl, yml, None, final_g if last else None)
        if not last:
            hc = mixer(hc, res_c[4], ymc, ctx_row, None)
    return h
```

```python
import functools

import jax
import jax.numpy as jnp
from jax import lax
from jax.experimental import pallas as pl
from jax.experimental.pallas import tpu as pltpu

GRID_W = 64
CHUNK = 128
EPS = 1e-6
TOP_K = 2
ROW_TILE = 512
MOE_BLOCK = 512
COMBINE_TILE = 512
SUB_TILE = 256
LANES = 128
SUBLANES = 8
VMEM_LIMIT = 56 * 1024 * 1024

F32 = jnp.float32
BF16 = jnp.bfloat16
HIGHEST = lax.Precision.HIGHEST


def _dot(a, b):
    return jnp.dot(a, b, preferred_element_type=F32)


def _dot_nt(a, b):
    return lax.dot_general(a, b, (((1,), (1,)), ((), ())), preferred_element_type=F32)


def _dot_tn(a, b):
    return lax.dot_general(a, b, (((0,), (0,)), ((), ())), preferred_element_type=F32)


def _rows_to_tiles(ref, x, row0=0):
    n, d = x.shape
    per = d // LANES
    for s in range(per):
        ref[pl.ds(row0 * per + s, n, stride=per), :] = x[:, s * LANES:(s + 1) * LANES]


def _tiles_to_rows(ref, n, per):
    return jnp.concatenate([ref[pl.ds(s, n, stride=per), :] for s in range(per)], axis=1)


def _sub_tiles(tm):
    sub = min(SUB_TILE, tm)
    return [slice(r0, r0 + sub) for r0 in range(0, tm, sub)]


def _params(*sem):
    return pltpu.CompilerParams(dimension_semantics=sem, vmem_limit_bytes=VMEM_LIMIT)


def _ff_chunks(d_ff, max_chunk=1024):
    out, c0 = [], 0
    while c0 < d_ff:
        c1 = min(c0 + max_chunk, d_ff)
        out.append((c0, c1))
        c0 = c1
    return out


def _ada_kernel(c_ref, w_ref, b_ref, o_ref):
    s = c_ref[...]
    s = s * jax.nn.sigmoid(s)
    o_ref[0] = jnp.dot(s, w_ref[0], precision=HIGHEST, preferred_element_type=F32) + b_ref[0]


def _ada(c_all, w_ada, b_ada):
    depth, d, d6 = w_ada.shape
    rows = c_all.shape[0]
    tn = d6 // 4
    return pl.pallas_call(
        _ada_kernel,
        grid=(depth, d6 // tn),
        in_specs=[
            pl.BlockSpec((rows, d), lambda l, j: (0, 0)),
            pl.BlockSpec((1, d, tn), lambda l, j: (l, 0, j)),
            pl.BlockSpec((1, 1, tn), lambda l, j: (l, 0, j)),
        ],
        out_specs=pl.BlockSpec((1, rows, tn), lambda l, j: (l, 0, j)),
        out_shape=jax.ShapeDtypeStruct((depth, rows, d6), F32),
        compiler_params=_params("parallel", "parallel"),
        name="adaln_mod",
    )(c_all, w_ada, b_ada.reshape(depth, 1, d6))


def _norm_mod(x, g, shift, scale):
    y = x * lax.rsqrt(jnp.mean(x * x, axis=-1, keepdims=True) + EPS)
    return (y * g) * (1.0 + scale) + shift


def _in_kernel(x_ref, mod_ref, g_ref, wq_ref, wk_ref, wv_ref, wg_ref, bg_ref, *rest,
               row_w, full, kscale):
    if full:
        wc_ref, wo_ref, cw_ref, q_ref, k_ref, v_ref, gt_ref, yc_ref, og_ref = rest
    else:
        q_ref, k_ref, v_ref, gt_ref = rest
    for rows in _sub_tiles(x_ref.shape[1]):
        u = _norm_mod(x_ref[0, rows, :], g_ref[...], mod_ref[0, 0:1, :], mod_ref[0, 1:2, :]).astype(BF16)
        q_ref[0, :, rows] = _dot_nt(wq_ref[...], u).astype(BF16)
        k_ref[0, rows, :] = (_dot(u, wk_ref[...]) * kscale).astype(BF16)
        v_ref[0, :, rows] = _dot_nt(wv_ref[...], u).astype(BF16)
        gt_ref[0, :, rows] = _dot_nt(wg_ref[...], u) + bg_ref[...]
        if full:
            cd = cw_ref.shape[1]
            c3 = _dot(u, wc_ref[...])
            z = c3[:, cd:2 * cd] * c3[:, 2 * cd:]
            n = z.shape[0]
            t = lax.broadcasted_iota(jnp.int32, (n, 1), 0) % row_w
            zprev = jnp.where(t == 0, 0.0, pltpu.roll(z, 1, axis=0))
            znext = jnp.where(t == row_w - 1, 0.0, pltpu.roll(z, n - 1, axis=0))
            conv = zprev * cw_ref[0:1, :] + z * cw_ref[1:2, :] + znext * cw_ref[2:3, :]
            yc_ref[0, rows, :] = (c3[:, :cd] * conv).astype(BF16)
            og_ref[0, :, rows] = jax.nn.sigmoid(_dot_nt(wo_ref[...], u)).astype(BF16)


def _in_proj(h, mod, mod_row, norm_g, wts, row_w, full):
    b, t, d = h.shape
    tm = min(ROW_TILE, t)
    md = wts["wk"].shape[1]
    n_gate = wts["bg"].shape[0]
    head_dim = md // (n_gate // 4)
    const = lambda shape: pl.BlockSpec(shape, lambda i, j: (0,) * len(shape))
    tile = lambda w: pl.BlockSpec((1, tm, w), lambda i, j: (i, j, 0))
    tile_t = pl.BlockSpec((1, md, tm), lambda i, j: (i, 0, j))
    feat_major = jax.ShapeDtypeStruct((b, md, t), BF16)
    in_specs = [
        tile(d),
        pl.BlockSpec((1, 6, d), (lambda i, j: (i, 0, 0)) if mod_row is None
                     else (lambda i, j: (mod_row, 0, 0))),
        const((1, d)),
        const(wts["wq"].shape), const(wts["wk"].shape), const(wts["wv"].shape),
        const(wts["wg"].shape), const(wts["bg"].shape),
    ]
    args = [h, mod, norm_g, wts["wq"], wts["wk"], wts["wv"], wts["wg"], wts["bg"]]
    out_specs = [tile_t, tile(md), tile_t, pl.BlockSpec((1, n_gate, tm), lambda i, j: (i, 0, j))]
    out_shape = [feat_major, jax.ShapeDtypeStruct((b, t, md), BF16), feat_major,
                 jax.ShapeDtypeStruct((b, n_gate, t), F32)]
    if full:
        cd = wts["cw"].shape[1]
        in_specs += [const(wts["wc"].shape), const(wts["wo"].shape), const(wts["cw"].shape)]
        args += [wts["wc"], wts["wo"], wts["cw"]]
        out_specs += [tile(cd), tile_t]
        out_shape += [jax.ShapeDtypeStruct((b, t, cd), BF16), feat_major]
    return pl.pallas_call(
        functools.partial(_in_kernel, row_w=row_w, full=full, kscale=head_dim ** -0.5),
        grid=(b, t // tm),
        in_specs=in_specs,
        out_specs=out_specs,
        out_shape=out_shape,
        compiler_params=_params("parallel", "parallel"),
        name="in_proj",
    )(*args)


def _lane_scan(x, op, ident, reverse):
    n = x.shape[1]
    lane = lax.broadcasted_iota(jnp.int32, x.shape, 1)
    s = 1
    while s < n:
        if reverse:
            shifted = jnp.where(lane < n - s, pltpu.roll(x, n - s, axis=1), ident)
        else:
            shifted = jnp.where(lane >= s, pltpu.roll(x, s, axis=1), ident)
        x = op(x, shifted)
        s *= 2
    return x


def _mlstm_kernel(gr_ref, kc_ref, qc_ref, vc_ref, kl_ref, ql_ref, vl_ref, ogl_ref, gain_ref,
                  *rest, need_ctx):
    if need_ctx:
        ogc_ref, yc_ref, yl_ref, rt_ref, ra_ref, rp_ref, rw_ref, hc_ref, hl_ref = rest
    else:
        yl_ref, rt_ref, ra_ref, rp_ref, rw_ref, hc_ref, hl_ref = rest
        ogc_ref = yc_ref = None
    L = CHUNK
    hd = kl_ref.shape[2]
    ncc = kc_ref.shape[1] // L
    ncl = kl_ref.shape[1] // L
    nct = ncc + ncl

    head = pl.program_id(1)
    n_heads = gr_ref.shape[1]

    @pl.when(head == 0)
    def _():
        for d in range(2):
            rev = d == 1
            last = 0 if rev else L - 1
            for hh in range(n_heads):
                ig = gr_ref[0, hh, 2 * d]
                lf = jax.nn.log_sigmoid(gr_ref[0, hh, 2 * d + 1])
                a = _lane_scan(lf, jnp.add, 0.0, rev)
                r = ig - a
                p = _lane_scan(r, jnp.maximum, -jnp.inf, rev)
                ra_ref[d, hh, 0:nct, :] = a
                rp_ref[d, hh, 0:nct, :] = p
                rw_ref[d, hh, 0:nct, :] = jnp.exp(r - p[:, last:last + 1])
                r_pad = jnp.concatenate([r, jnp.zeros((L - nct, L), F32)], axis=0)
                rt_ref[d, hh] = r_pad.T

    spos = lax.broadcasted_iota(jnp.int32, (L, L), 0)
    tpos = lax.broadcasted_iota(jnp.int32, (L, L), 1)
    ones_row = jnp.where(lax.broadcasted_iota(jnp.int32, (hd, L), 0) == 0, 1.0, 0.0).astype(BF16)
    gain = gain_ref[...]

    def chunk_step(d, seq, j, g, caug, m, mode):
        k_ref, q_ref, v_ref, h_ref, og_ref, y_ref = seq
        off = j * L
        last = 0 if d == 1 else L - 1
        k = k_ref[0, pl.ds(off, L), :]
        vaug = jnp.concatenate([v_ref[0, :, pl.ds(off, L)], ones_row], axis=0)
        a_row = ra_ref[d, head, g:g + 1, :]
        p_row = rp_ref[d, head, g:g + 1, :]
        b_tot = a_row[:, last:last + 1]
        p_last = p_row[:, last:last + 1]
        if mode is not None:
            q = q_ref[0, :, pl.ds(off, L)]
            r_col = rt_ref[d, head, :, g:g + 1]
            mask = (spos >= tpos) if d == 1 else (spos <= tpos)
            w = jnp.where(mask, jnp.exp(r_col - p_row), 0.0)
            intra = _dot(vaug, (_dot(k, q) * w).astype(BF16))
            inter = _dot(caug.astype(BF16), q)
            m_row = jnp.maximum(p_row, m)
            nd = jnp.exp(p_row - m_row) * intra + jnp.exp(m - m_row) * inter
            den = jnp.maximum(jnp.abs(nd[hd:hd + 1, :]), jnp.exp(-(a_row + m_row)))
            hout = nd[:hd, :] * (1.0 / den)
            if mode == "store":
                h_ref[:, pl.ds(off, L)] = hout
            else:
                hh = h_ref[:, pl.ds(off, L)] + hout
                hn = hh * lax.rsqrt(jnp.mean(hh * hh, axis=0, keepdims=True) + EPS)
                og = og_ref[0, :, pl.ds(off, L)].astype(F32)
                y_ref[0, :, pl.ds(off, L)] = (hn * gain * og).astype(BF16)
        m_last = jnp.maximum(p_last, m)
        vw = (vaug.astype(F32) * rw_ref[d, head, g:g + 1, :]).astype(BF16)
        caug = jnp.exp(m - m_last) * caug + jnp.exp(p_last - m_last) * _dot(vw, k)
        return caug, b_tot + m_last

    def pair_step(seq, n, g0, i, st, mode):
        cf, mf, cb, mb = st
        cf, mf = chunk_step(0, seq, i, g0 + i, cf, mf, mode)
        cb, mb = chunk_step(1, seq, n - 1 - i, g0 + n - 1 - i, cb, mb, mode)
        return cf, mf, cb, mb

    ctx_seq = (kc_ref, qc_ref, vc_ref, hc_ref, ogc_ref, yc_ref)
    lat_seq = (kl_ref, ql_ref, vl_ref, hl_ref, ogl_ref, yl_ref)
    zc, zm = jnp.zeros((2 * hd, hd), F32), jnp.zeros((1, 1), F32)
    st = (zc, zm, zc, zm)
    for i in range(ncc):
        mode = None if not need_ctx else ("store" if i < ncc // 2 else "final")
        st = pair_step(ctx_seq, ncc, 0, i, st, mode)
    for i in range(ncl):
        st = pair_step(lat_seq, ncl, ncc, i, st, "store" if i < ncl // 2 else "final")


def _mlstm(gates_c, gates_l, kqv_c, kqv_l, og_c, og_l, gain, need_ctx):
    b, tc, md = kqv_c[0].shape
    tl = kqv_l[0].shape[1]
    n_gate = gates_c.shape[1]
    heads = n_gate // 4
    hd = md // heads
    L = CHUNK
    nct = (tc + tl) // L
    assert tc // L % 2 == 0 and tl // L % 2 == 0 and nct <= L
    nrow = -(-nct // SUBLANES) * SUBLANES
    g = jnp.concatenate([gates_c, gates_l], axis=2).reshape(b, 4, heads, nct, L)
    g = g.transpose(0, 2, 1, 3, 4)
    tok = lambda t: pl.BlockSpec((1, t, hd), lambda i, h: (i, 0, h))
    feat = lambda t: pl.BlockSpec((1, hd, t), lambda i, h: (i, h, 0))
    in_specs = [pl.BlockSpec((1, heads, 4, nct, L), lambda i, h: (i, 0, 0, 0, 0)),
                tok(tc), feat(tc), feat(tc), tok(tl), feat(tl), feat(tl), feat(tl),
                pl.BlockSpec((hd, 1), lambda i, h: (h, 0))]
    args = [g, *kqv_c, *kqv_l, og_l, gain.reshape(md, 1)]
    out_specs = [feat(tl)]
    out_shape = [jax.ShapeDtypeStruct((b, md, tl), BF16)]
    if need_ctx:
        in_specs.append(feat(tc))
        args.append(og_c)
        out_specs = [feat(tc)] + out_specs
        out_shape = [jax.ShapeDtypeStruct((b, md, tc), BF16)] + out_shape
    res = pl.pallas_call(
        functools.partial(_mlstm_kernel, need_ctx=need_ctx),
        grid=(b, heads),
        in_specs=in_specs,
        out_specs=out_specs,
        out_shape=out_shape,
        scratch_shapes=[
            pltpu.VMEM((2, heads, L, L), F32),
            pltpu.VMEM((2, heads, nrow, L), F32),
            pltpu.VMEM((2, heads, nrow, L), F32),
            pltpu.VMEM((2, heads, nrow, L), F32),
            pltpu.VMEM((hd, tc), F32),
            pltpu.VMEM((hd, tl), F32),
        ],
        compiler_params=_params("parallel", "arbitrary"),
        name="mlstm_scan",
    )(*args)
    return (res[0], res[1]) if need_ctx else (None, res[0])


def _mix_residual(h_ref, yc_ref, ym_ref, mod_ref, woc_ref, wom_ref, rows):
    mix = _dot(yc_ref[0, rows, :], woc_ref[...]) + _dot_tn(ym_ref[0, :, rows], wom_ref[...])
    return h_ref[0, rows, :] + mod_ref[0, 2:3, :] * mix


def _out_dense_kernel(h_ref, yc_ref, ym_ref, mod_ref, g2_ref, woc_ref, wom_ref, w1_ref, w3_ref, w2_ref,
                      *rest, chunks, final):
    if final:
        gf_ref, o_ref = rest
    else:
        (o_ref,) = rest
    h = _mix_residual(h_ref, yc_ref, ym_ref, mod_ref, woc_ref, wom_ref, slice(None))
    u = _norm_mod(h, g2_ref[...], mod_ref[0, 3:4, :], mod_ref[0, 4:5, :]).astype(BF16)
    acc = None
    for c0, c1 in chunks:
        a = _dot(u, w1_ref[:, c0:c1])
        hid = (a * jax.nn.sigmoid(a) * _dot(u, w3_ref[:, c0:c1])).astype(BF16)
        part = _dot(hid, w2_ref[c0:c1, :])
        acc = part if acc is None else acc + part
    h = h + mod_ref[0, 5:6, :] * acc
    if final:
        h = h * lax.rsqrt(jnp.mean(h * h, axis=-1, keepdims=True) + EPS) * gf_ref[...]
    o_ref[0] = h


def _out_moe_kernel(h_ref, yc_ref, ym_ref, mod_ref, g2_ref, woc_ref, wom_ref, router_ref,
                    hm_ref, u_ref, ti_ref, tg_ref, *, n_exp):
    for rows in _sub_tiles(h_ref.shape[1]):
        _out_moe_rows(h_ref, yc_ref, ym_ref, mod_ref, g2_ref, woc_ref, wom_ref, router_ref,
                      hm_ref, u_ref, ti_ref, tg_ref, rows, n_exp)


def _out_moe_rows(h_ref, yc_ref, ym_ref, mod_ref, g2_ref, woc_ref, wom_ref, router_ref,
                  hm_ref, u_ref, ti_ref, tg_ref, rows, n_exp):
    h = _mix_residual(h_ref, yc_ref, ym_ref, mod_ref, woc_ref, wom_ref, rows)
    hm_ref[0, rows, :] = h
    u = _norm_mod(h, g2_ref[...], mod_ref[0, 3:4, :], mod_ref[0, 4:5, :])
    _rows_to_tiles(u_ref, u, rows.start)
    u_hi = u.astype(BF16)
    u_lo = (u - u_hi.astype(F32)).astype(BF16)
    parts = _dot(u_hi, router_ref[...]) + _dot(u_lo, router_ref[...])
    logits = parts[:, :LANES] + parts[:, LANES:]
    n_pad = -(-n_exp // SUBLANES) * SUBLANES
    lt = logits.T[:n_pad, :]
    row = lax.broadcasted_iota(jnp.int32, lt.shape, 0)
    lt = jnp.where(row < n_exp, lt, -jnp.inf)
    v1 = jnp.max(lt, axis=0, keepdims=True)
    i1 = jnp.min(jnp.where(lt == v1, row, n_pad), axis=0, keepdims=True)
    rest = jnp.where(row == i1, -jnp.inf, lt)
    v2 = jnp.max(rest, axis=0, keepdims=True)
    i2 = jnp.min(jnp.where(rest == v2, row, n_pad), axis=0, keepdims=True)
    e2 = jnp.exp(v2 - v1)
    inv = 1.0 / (1.0 + e2)
    ti_ref[0, :, rows] = jnp.concatenate([i1, i2], axis=0)
    tg_ref[0, :, rows] = jnp.concatenate([inv, e2 * inv], axis=0)


def _out_common(h, yc, ym, mod, mod_row, norm_g, wts):
    b, t, d = h.shape
    tm = min(ROW_TILE, t)
    const = lambda shape: pl.BlockSpec(shape, lambda i, j: (0,) * len(shape))
    tile = lambda w: pl.BlockSpec((1, tm, w), lambda i, j: (i, j, 0))
    in_specs = [
        tile(d), tile(yc.shape[2]), pl.BlockSpec((1, ym.shape[1], tm), lambda i, j: (i, 0, j)),
        pl.BlockSpec((1, 6, d), (lambda i, j: (i, 0, 0)) if mod_row is None
                     else (lambda i, j: (mod_row, 0, 0))),
        const((1, d)), const(wts["woc"].shape), const(wts["wom"].shape),
    ]
    args = [h, yc, ym, mod, norm_g, wts["woc"], wts["wom"]]
    return b, t, d, tm, const, tile, in_specs, args


def _out_dense(h, yc, ym, mod, mod_row, norm_g, wts, ffn, final_g):
    b, t, d, tm, const, tile, in_specs, args = _out_common(h, yc, ym, mod, mod_row, norm_g, wts)
    w1, w3, w2 = ffn
    in_specs += [const(w1.shape), const(w3.shape), const(w2.shape)]
    args += [w1, w3, w2]
    if final_g is not None:
        in_specs.append(const((1, d)))
        args.append(final_g)
    return pl.pallas_call(
        functools.partial(_out_dense_kernel, chunks=_ff_chunks(w1.shape[1]), final=final_g is not None),
        grid=(b, t // tm),
        in_specs=in_specs,
        out_specs=tile(d),
        out_shape=jax.ShapeDtypeStruct((b, t, d), F32),
        compiler_params=_params("parallel", "parallel"),
        name="out_dense_ffn",
    )(*args)


def _out_moe(h, yc, ym, mod, mod_row, norm_g, wts, router_pad, n_exp):
    b, t, d, tm, const, tile, in_specs, args = _out_common(h, yc, ym, mod, mod_row, norm_g, wts)
    in_specs.append(const(router_pad.shape))
    args.append(router_pad)
    per = d // LANES
    top2 = pl.BlockSpec((1, TOP_K, tm), lambda i, j: (i, 0, j))
    return pl.pallas_call(
        functools.partial(_out_moe_kernel, n_exp=n_exp),
        grid=(b, t // tm),
        in_specs=in_specs,
        out_specs=[tile(d), pl.BlockSpec((tm * per, LANES), lambda i, j: (i * (t // tm) + j, 0)),
                   top2, top2],
        out_shape=[jax.ShapeDtypeStruct((b, t, d), F32), jax.ShapeDtypeStruct((b * t * per, LANES), F32),
                   jax.ShapeDtypeStruct((b, TOP_K, t), jnp.int32), jax.ShapeDtypeStruct((b, TOP_K, t), F32)],
        compiler_params=_params("parallel", "parallel"),
        name="out_router",
    )(*args)


ISSUE_UNROLL = 8


def _expert_kernel(be_ref, nu_ref, tok_ref, toknext_ref, u_hbm, w1_ref, w3_ref, w2_ref, y_ref, xbuf, sem,
                   *, chunks, per):
    i = pl.program_id(0)
    n_used = nu_ref[0]
    blk = xbuf.shape[1] // per
    slot = i % 2

    def row_copy(idx_ref, r, to_slot):
        src = pl.multiple_of(idx_ref[0, 0, r], per)
        return pltpu.make_async_copy(u_hbm.at[pl.ds(src, per), :],
                                     xbuf.at[to_slot, pl.ds(r * per, per), :], sem.at[to_slot])

    @pl.when(i == 0)
    def _():
        def body(r, c):
            row_copy(tok_ref, r, 0).start()
            return c
        lax.fori_loop(0, blk, body, 0, unroll=ISSUE_UNROLL)

    @pl.when(i <= n_used)
    def _():
        pltpu.make_async_copy(u_hbm.at[pl.ds(0, blk * per), :], xbuf.at[slot], sem.at[slot]).wait()

    @pl.when(i < n_used)
    def _():
        for r in range(blk):
            row_copy(toknext_ref, r, 1 - slot).start()
        x = _tiles_to_rows(xbuf.at[slot], blk, per).astype(BF16)
        acc = None
        for c0, c1 in chunks:
            a = _dot(x, w1_ref[0, :, c0:c1])
            hid = (a * jax.nn.sigmoid(a) * _dot(x, w3_ref[0, :, c0:c1])).astype(BF16)
            part = _dot(hid, w2_ref[0, c0:c1, :])
            acc = part if acc is None else acc + part
        _rows_to_tiles(y_ref, acc)

    @pl.when(i >= n_used)
    def _():
        y_ref[...] = jnp.zeros(y_ref.shape, F32)


def _experts(u_tiles, tok_buf, blk_e, n_used, w1, w3, w2):
    d, f = w1.shape[1], w1.shape[2]
    per = d // LANES
    nblk = tok_buf.shape[0] // MOE_BLOCK
    tok3 = (tok_buf * per).reshape(nblk, 1, MOE_BLOCK)
    smem = lambda fn: pl.BlockSpec((1, 1, MOE_BLOCK), fn, memory_space=pltpu.SMEM)
    grid_spec = pltpu.PrefetchScalarGridSpec(
        num_scalar_prefetch=2,
        grid=(nblk,),
        in_specs=[
            smem(lambda i, be, nu: (i, 0, 0)),
            smem(lambda i, be, nu: (jnp.minimum(i + 1, nblk - 1), 0, 0)),
            pl.BlockSpec(memory_space=pl.ANY),
            pl.BlockSpec((1, d, f), lambda i, be, nu: (be[i], 0, 0)),
            pl.BlockSpec((1, d, f), lambda i, be, nu: (be[i], 0, 0)),
            pl.BlockSpec((1, f, d), lambda i, be, nu: (be[i], 0, 0)),
        ],
        out_specs=pl.BlockSpec((MOE_BLOCK * per, LANES), lambda i, be, nu: (i, 0)),
        scratch_shapes=[pltpu.VMEM((2, MOE_BLOCK * per, LANES), F32), pltpu.SemaphoreType.DMA((2,))],
    )
    return pl.pallas_call(
        functools.partial(_expert_kernel, chunks=_ff_chunks(f), per=per),
        grid_spec=grid_spec,
        out_shape=jax.ShapeDtypeStruct((nblk * MOE_BLOCK * per, LANES), F32),
        compiler_params=_params("arbitrary"),
        name="moe_experts",
    )(blk_e, n_used, tok3, tok3, u_tiles, w1, w3, w2)


def _combine_kernel(dst_ref, dnext_ref, h_ref, gate_ref, mod_ref, *rest, final):
    if final:
        gf_ref, y_hbm, o_ref, ybuf, sem = rest
    else:
        y_hbm, o_ref, ybuf, sem = rest
    i = pl.program_id(0)
    n = pl.num_programs(0)
    tc, d = h_ref.shape
    per = d // LANES

    def issue(idx_ref, slot):
        def body(r, c):
            for k in range(TOP_K):
                src = pl.multiple_of(idx_ref[0, 0, TOP_K * r + k], per)
                pltpu.make_async_copy(y_hbm.at[pl.ds(src, per), :],
                                      ybuf.at[slot, k, pl.ds(pl.multiple_of(r * per, per), per), :],
                                      sem.at[slot]).start()
            return c
        lax.fori_loop(0, tc, body, 0, unroll=ISSUE_UNROLL)

    @pl.when(i == 0)
    def _():
        issue(dst_ref, 0)

    slot = i % 2

    @pl.when(i + 1 < n)
    def _():
        issue(dnext_ref, 1 - slot)

    for k in range(TOP_K):
        pltpu.make_async_copy(y_hbm.at[pl.ds(0, tc * per), :], ybuf.at[slot, k], sem.at[slot]).wait()

    gate = gate_ref[...]
    y = (gate[:, 0:1] * _tiles_to_rows(ybuf.at[slot, 0], tc, per)
         + gate[:, 1:2] * _tiles_to_rows(ybuf.at[slot, 1], tc, per))
    h = h_ref[...] + mod_ref[0, 5:6, :] * y
    if final:
        h = h * lax.rsqrt(jnp.mean(h * h, axis=-1, keepdims=True) + EPS) * gf_ref[...]
    o_ref[...] = h


def _combine(h_flat, y, dest, gates, mod, mod_row, tokens_per_batch, final_g):
    n_tok, d = h_flat.shape
    tc = COMBINE_TILE
    steps = n_tok // tc
    per_b = tokens_per_batch // tc
    per = d // LANES
    dst2 = (dest * per).reshape(steps, 1, TOP_K * tc)
    smem = lambda fn: pl.BlockSpec((1, 1, TOP_K * tc), fn, memory_space=pltpu.SMEM)
    in_specs = [
        smem(lambda i: (i, 0, 0)),
        smem(lambda i: (jnp.minimum(i + 1, steps - 1), 0, 0)),
        pl.BlockSpec((tc, d), lambda i: (i, 0)),
        pl.BlockSpec((tc, TOP_K), lambda i: (i, 0)),
        pl.BlockSpec((1, 6, d), (lambda i: (i // per_b, 0, 0)) if mod_row is None
                     else (lambda i: (mod_row, 0, 0))),
    ]
    args = [dst2, dst2, h_flat, gates, mod]
    if final_g is not None:
        in_specs.append(pl.BlockSpec((1, d), lambda i: (0, 0)))
        args.append(final_g)
    in_specs.append(pl.BlockSpec(memory_space=pl.ANY))
    args.append(y)
    return pl.pallas_call(
        functools.partial(_combine_kernel, final=final_g is not None),
        grid=(steps,),
        in_specs=in_specs,
        out_specs=pl.BlockSpec((tc, d), lambda i: (i, 0)),
        out_shape=jax.ShapeDtypeStruct((n_tok, d), F32),
        scratch_shapes=[pltpu.VMEM((2, TOP_K, tc * per, LANES), F32), pltpu.SemaphoreType.DMA((2,))],
        compiler_params=_params("arbitrary"),
        name="moe_combine",
    )(*args)


def _routing(top_i, n_exp):
    e = top_i.reshape(-1)
    a = e.shape[0]
    experts = jnp.arange(n_exp, dtype=jnp.int32)
    onehot = (e[:, None] == experts[None, :]).astype(jnp.int32)
    csum = jnp.cumsum(onehot, axis=0)
    rank = jnp.sum(onehot * csum, axis=1) - 1
    counts = csum[-1]
    padded = (counts + MOE_BLOCK - 1) // MOE_BLOCK * MOE_BLOCK
    pend = jnp.cumsum(padded)
    pstart = pend - padded
    start = jnp.cumsum(counts) - counts
    dest = pstart[e] + rank
    cap = -(-a // MOE_BLOCK) * MOE_BLOCK + n_exp * MOE_BLOCK
    nblk = cap // MOE_BLOCK
    blk_first = jnp.arange(nblk, dtype=jnp.int32) * MOE_BLOCK
    blk_e = jnp.minimum(jnp.sum((pend[None, :] <= blk_first[:, None]).astype(jnp.int32), axis=1), n_exp - 1)
    n_used = (pend[-1] // MOE_BLOCK).astype(jnp.int32).reshape(1)
    order = jnp.argsort(e).astype(jnp.int32)
    row = jnp.arange(cap, dtype=jnp.int32)
    row_e = jnp.repeat(blk_e, MOE_BLOCK)
    within = row - pstart[row_e]
    valid = (within < counts[row_e]) & (row < pend[-1])
    tok_buf = jnp.where(valid, order[jnp.clip(start[row_e] + within, 0, a - 1)] // TOP_K, 0)
    return dest.astype(jnp.int32), tok_buf.astype(jnp.int32), blk_e.astype(jnp.int32), n_used


def _moe_mixer(h_mid, u, top_i, top_g, mod, mod_row, moe_w, n_exp, final_g):
    b, t, d = h_mid.shape
    n_tok = b * t
    dest, tok_buf, blk_e, n_used = _routing(top_i, n_exp)
    y = _experts(u, tok_buf, blk_e, n_used, *moe_w)
    out = _combine(h_mid.reshape(n_tok, d), y, dest, top_g.reshape(n_tok, TOP_K), mod, mod_row, t, final_g)
    return out.reshape(b, t, d)


def _layer_weights(w_in_l, conv_w_l, b_gates_l, w_out_l):
    cd = conv_w_l.shape[1]
    n_gate = b_gates_l.size
    md = (w_in_l.shape[1] - 3 * cd - n_gate) // 4
    q0 = 3 * cd
    g0 = q0 + 3 * md
    return {
        "wc": w_in_l[:, :q0].astype(BF16),
        "wq": w_in_l[:, q0:q0 + md].T.astype(BF16),
        "wk": w_in_l[:, q0 + md:q0 + 2 * md].astype(BF16),
        "wv": w_in_l[:, q0 + 2 * md:g0].T.astype(BF16),
        "wg": w_in_l[:, g0:g0 + n_gate].T.astype(BF16),
        "bg": b_gates_l.reshape(n_gate, 1),
        "wo": w_in_l[:, g0 + n_gate:].T.astype(BF16),
        "cw": conv_w_l,
        "woc": w_out_l[:cd].astype(BF16),
        "wom": w_out_l[cd:].astype(BF16),
    }


def kernel(x, c, ctx, c_ctx, norm1_g, norm2_g, w_ada, b_ada, w_in, conv_w, b_gates, mlstm_norm_g, w_out,
           ffn_w1, ffn_w3, ffn_w2, moe_router, moe_w1, moe_w3, moe_w2, final_norm_g):
    b, t, d = x.shape
    tc = ctx.shape[1]
    depth = w_in.shape[0]
    n_exp = moe_router.shape[-1]
    assert t % GRID_W == 0 and t % CHUNK == 0 and tc % CHUNK == 0 and d % (SUBLANES * LANES) == 0

    ctx_row = b
    n_rows = -(-(b + 1) // SUBLANES) * SUBLANES
    c_all = jnp.concatenate([c, c_ctx[None, :], jnp.zeros((n_rows - b - 1, d), F32)], axis=0)
    mod_all = _ada(c_all, w_ada, b_ada).reshape(depth, n_rows, 6, d)
    final_g = final_norm_g.reshape(1, d)

    h, hc = x, ctx
    for layer in range(depth):
        last = layer == depth - 1
        j = layer // 2
        mod = mod_all[layer]
        wts = _layer_weights(w_in[layer], conv_w[layer], b_gates[layer], w_out[layer])
        n1 = norm1_g[layer].reshape(1, d)
        n2 = norm2_g[layer].reshape(1, d)

        ql, kl, vl, gl, ycl, ogl = _in_proj(h, mod, None, n1, wts, GRID_W, True)
        res_c = _in_proj(hc, mod, ctx_row, n1, wts, tc, not last)
        qc, kc, vc, gc = res_c[:4]
        ogc = None if last else res_c[5]
        ymc, yml = _mlstm(gc, gl, (kc, qc, vc), (kl, ql, vl), ogc, ogl,
                          mlstm_norm_g[layer].reshape(1, -1), not last)

        def mixer(hh, yc_, ym_, mod_row, fin):
            if layer % 2 == 0:
                ffn = (ffn_w1[j].astype(BF16), ffn_w3[j].astype(BF16), ffn_w2[j].astype(BF16))
                return _out_dense(hh, yc_, ym_, mod, mod_row, n2, wts, ffn, fin)
            r_full = jnp.pad(moe_router[j], ((0, 0), (0, LANES - n_exp)))
            r_hi = r_full.astype(BF16)
            r_lo = (r_full - r_hi.astype(F32)).astype(BF16)
            router_pad = jnp.concatenate([r_hi, r_lo], axis=1)
            h_mid, u2, top_i, top_g = _out_moe(hh, yc_, ym_, mod, mod_row, n2, wts, router_pad, n_exp)
            top_i, top_g = top_i.transpose(0, 2, 1), top_g.transpose(0, 2, 1)
            moe_w = (moe_w1[j].astype(BF16), moe_w3[j].astype(BF16), moe_w2[j].astype(BF16))
            return _moe_mixer(h_mid, u2, top_i, top_g, mod, mod_row, moe_w, n_exp, fin)

        h = mixer(h, ycl, yml, None, final_g if last else None)
        if not last:
            hc = mixer(hc, res_c[4], ymc, ctx_row, None)
    return h
```

```python
import functools

import jax
import jax.numpy as jnp
from jax import lax
from jax.experimental import pallas as pl
from jax.experimental.pallas import tpu as pltpu

GRID_W = 64
CHUNK = 128
EPS = 1e-6
TOP_K = 2
ROW_TILE = 512
MOE_BLOCK = 512
COMBINE_TILE = 512
SUB_TILE = 256
LANES = 128
SUBLANES = 8
AUG_ROWS = 32
VMEM_LIMIT = 56 * 1024 * 1024

F32 = jnp.float32
BF16 = jnp.bfloat16
HIGHEST = lax.Precision.HIGHEST


def _dot(a, b):
    return jnp.dot(a, b, preferred_element_type=F32)


def _dot_nt(a, b):
    return lax.dot_general(a, b, (((1,), (1,)), ((), ())), preferred_element_type=F32)


def _dot_tn(a, b):
    return lax.dot_general(a, b, (((0,), (0,)), ((), ())), preferred_element_type=F32)


def _rows_to_tiles(ref, x, row0=0):
    n, d = x.shape
    per = d // LANES
    for s in range(per):
        ref[pl.ds(row0 * per + s, n, stride=per), :] = x[:, s * LANES:(s + 1) * LANES]


def _tiles_to_rows(ref, n, per):
    return jnp.concatenate([ref[pl.ds(s, n, stride=per), :] for s in range(per)], axis=1)


def _sub_tiles(tm):
    sub = min(SUB_TILE, tm)
    return [slice(r0, r0 + sub) for r0 in range(0, tm, sub)]


def _params(*sem):
    return pltpu.CompilerParams(dimension_semantics=sem, vmem_limit_bytes=VMEM_LIMIT)


def _ff_chunks(d_ff, max_chunk=1024):
    out, c0 = [], 0
    while c0 < d_ff:
        c1 = min(c0 + max_chunk, d_ff)
        out.append((c0, c1))
        c0 = c1
    return out


def _ada_kernel(c_ref, w_ref, b_ref, o_ref):
    s = c_ref[...]
    s = s * jax.nn.sigmoid(s)
    o_ref[0] = jnp.dot(s, w_ref[0], precision=HIGHEST, preferred_element_type=F32) + b_ref[0]


def _ada(c_all, w_ada, b_ada):
    depth, d, d6 = w_ada.shape
    rows = c_all.shape[0]
    tn = d6 // 4
    return pl.pallas_call(
        _ada_kernel,
        grid=(depth, d6 // tn),
        in_specs=[
            pl.BlockSpec((rows, d), lambda l, j: (0, 0)),
            pl.BlockSpec((1, d, tn), lambda l, j: (l, 0, j)),
            pl.BlockSpec((1, 1, tn), lambda l, j: (l, 0, j)),
        ],
        out_specs=pl.BlockSpec((1, rows, tn), lambda l, j: (l, 0, j)),
        out_shape=jax.ShapeDtypeStruct((depth, rows, d6), F32),
        compiler_params=_params("parallel", "parallel"),
        name="adaln_mod",
    )(c_all, w_ada, b_ada.reshape(depth, 1, d6))


def _norm_mod(x, g, shift, scale):
    y = x * lax.rsqrt(jnp.mean(x * x, axis=-1, keepdims=True) + EPS)
    return (y * g) * (1.0 + scale) + shift


def _in_kernel(x_ref, mod_ref, g_ref, wq_ref, wk_ref, wv_ref, wg_ref, bg_ref, *rest,
               row_w, full, kscale):
    if full:
        wc_ref, wo_ref, cw_ref, q_ref, k_ref, v_ref, gt_ref, yc_ref, og_ref = rest
    else:
        q_ref, k_ref, v_ref, gt_ref = rest
    for rows in _sub_tiles(x_ref.shape[1]):
        u = _norm_mod(x_ref[0, rows, :], g_ref[...], mod_ref[0, 0:1, :], mod_ref[0, 1:2, :]).astype(BF16)
        q_ref[0, :, rows] = _dot_nt(wq_ref[...], u).astype(BF16)
        k_ref[0, rows, :] = (_dot(u, wk_ref[...]) * kscale).astype(BF16)
        v_ref[0, :, rows] = _dot_nt(wv_ref[...], u).astype(BF16)
        gt_ref[0, :, rows] = _dot_nt(wg_ref[...], u) + bg_ref[...]
        if full:
            cd = cw_ref.shape[1]
            c3 = _dot(u, wc_ref[...])
            z = c3[:, cd:2 * cd] * c3[:, 2 * cd:]
            n = z.shape[0]
            t = lax.broadcasted_iota(jnp.int32, (n, 1), 0) % row_w
            zprev = jnp.where(t == 0, 0.0, pltpu.roll(z, 1, axis=0))
            znext = jnp.where(t == row_w - 1, 0.0, pltpu.roll(z, n - 1, axis=0))
            conv = zprev * cw_ref[0:1, :] + z * cw_ref[1:2, :] + znext * cw_ref[2:3, :]
            yc_ref[0, rows, :] = (c3[:, :cd] * conv).astype(BF16)
            og_ref[0, :, rows] = jax.nn.sigmoid(_dot_nt(wo_ref[...], u)).astype(BF16)


def _in_proj(h, mod, mod_row, norm_g, wts, row_w, full):
    b, t, d = h.shape
    tm = min(ROW_TILE, t)
    md = wts["wk"].shape[1]
    n_gate = wts["bg"].shape[0]
    head_dim = md // (n_gate // 4)
    const = lambda shape: pl.BlockSpec(shape, lambda i, j: (0,) * len(shape))
    tile = lambda w: pl.BlockSpec((1, tm, w), lambda i, j: (i, j, 0))
    tile_t = pl.BlockSpec((1, md, tm), lambda i, j: (i, 0, j))
    feat_major = jax.ShapeDtypeStruct((b, md, t), BF16)
    in_specs = [
        tile(d),
        pl.BlockSpec((1, 6, d), (lambda i, j: (i, 0, 0)) if mod_row is None
                     else (lambda i, j: (mod_row, 0, 0))),
        const((1, d)),
        const(wts["wq"].shape), const(wts["wk"].shape), const(wts["wv"].shape),
        const(wts["wg"].shape), const(wts["bg"].shape),
    ]
    args = [h, mod, norm_g, wts["wq"], wts["wk"], wts["wv"], wts["wg"], wts["bg"]]
    out_specs = [tile_t, tile(md), tile_t, pl.BlockSpec((1, n_gate, tm), lambda i, j: (i, 0, j))]
    out_shape = [feat_major, jax.ShapeDtypeStruct((b, t, md), BF16), feat_major,
                 jax.ShapeDtypeStruct((b, n_gate, t), F32)]
    if full:
        cd = wts["cw"].shape[1]
        in_specs += [const(wts["wc"].shape), const(wts["wo"].shape), const(wts["cw"].shape)]
        args += [wts["wc"], wts["wo"], wts["cw"]]
        out_specs += [tile(cd), tile_t]
        out_shape += [jax.ShapeDtypeStruct((b, t, cd), BF16), feat_major]
    return pl.pallas_call(
        functools.partial(_in_kernel, row_w=row_w, full=full, kscale=head_dim ** -0.5),
        grid=(b, t // tm),
        in_specs=in_specs,
        out_specs=out_specs,
        out_shape=out_shape,
        compiler_params=_params("parallel", "parallel"),
        name="in_proj",
    )(*args)


def _lane_scan(x, op, ident, reverse):
    n = x.shape[1]
    lane = lax.broadcasted_iota(jnp.int32, x.shape, 1)
    s = 1
    while s < n:
        if reverse:
            shifted = jnp.where(lane < n - s, pltpu.roll(x, n - s, axis=1), ident)
        else:
            shifted = jnp.where(lane >= s, pltpu.roll(x, s, axis=1), ident)
        x = op(x, shifted)
        s *= 2
    return x


def _mlstm_kernel(gr_ref, kc_ref, qc_ref, vc_ref, kl_ref, ql_ref, vl_ref, ogl_ref, gain_ref,
                  *rest, need_ctx):
    if need_ctx:
        ogc_ref, yc_ref, yl_ref, rt_ref, ra_ref, rp_ref, rw_ref, hc_ref, hl_ref = rest
    else:
        yl_ref, rt_ref, ra_ref, rp_ref, rw_ref, hc_ref, hl_ref = rest
        ogc_ref = yc_ref = None
    L = CHUNK
    hd = kl_ref.shape[2]
    ncc = kc_ref.shape[1] // L
    ncl = kl_ref.shape[1] // L
    nct = ncc + ncl

    head = pl.program_id(1)
    n_heads = gr_ref.shape[1]

    @pl.when(head == 0)
    def _():
        for d in range(2):
            rev = d == 1
            last = 0 if rev else L - 1
            for hh in range(n_heads):
                ig = gr_ref[0, hh, 2 * d]
                lf = jax.nn.log_sigmoid(gr_ref[0, hh, 2 * d + 1])
                a = _lane_scan(lf, jnp.add, 0.0, rev)
                r = ig - a
                p = _lane_scan(r, jnp.maximum, -jnp.inf, rev)
                ra_ref[d, hh, 0:nct, :] = a
                rp_ref[d, hh, 0:nct, :] = p
                rw_ref[d, hh, 0:nct, :] = jnp.exp(r - p[:, last:last + 1])
                r_pad = jnp.concatenate([r, jnp.zeros((L - nct, L), F32)], axis=0)
                rt_ref[d, hh] = r_pad.T

    spos = lax.broadcasted_iota(jnp.int32, (L, L), 0)
    tpos = lax.broadcasted_iota(jnp.int32, (L, L), 1)
    ones_row = jnp.where(lax.broadcasted_iota(jnp.int32, (AUG_ROWS, L), 0) == 0, 1.0, 0.0).astype(BF16)
    gain = gain_ref[...]

    def chunk_step(d, seq, j, g, caug, m, mode):
        k_ref, q_ref, v_ref, h_ref, og_ref, y_ref = seq
        off = j * L
        last = 0 if d == 1 else L - 1
        k = k_ref[0, pl.ds(off, L), :]
        vaug = jnp.concatenate([v_ref[0, :, pl.ds(off, L)], ones_row], axis=0)
        a_row = ra_ref[d, head, g:g + 1, :]
        p_row = rp_ref[d, head, g:g + 1, :]
        b_tot = a_row[:, last:last + 1]
        p_last = p_row[:, last:last + 1]
        if mode is not None:
            q = q_ref[0, :, pl.ds(off, L)]
            r_col = rt_ref[d, head, :, g:g + 1]
            mask = (spos >= tpos) if d == 1 else (spos <= tpos)
            w = jnp.where(mask, jnp.exp(r_col - p_row), 0.0)
            m_row = jnp.maximum(p_row, m)
            sw = (_dot(k, q) * w * jnp.exp(p_row - m_row)).astype(BF16)
            qs = (q.astype(F32) * jnp.exp(m - m_row)).astype(BF16)
            nd = _dot(jnp.concatenate([vaug, caug.astype(BF16)], axis=1),
                      jnp.concatenate([sw, qs], axis=0))
            den = jnp.maximum(jnp.abs(nd[hd:hd + 1, :]), jnp.exp(-(a_row + m_row)))
            hout = nd[:hd, :] * (1.0 / den)
            if mode == "store":
                h_ref[:, pl.ds(off, L)] = hout
            else:
                hh = h_ref[:, pl.ds(off, L)] + hout
                hn = hh * lax.rsqrt(jnp.mean(hh * hh, axis=0, keepdims=True) + EPS)
                og = og_ref[0, :, pl.ds(off, L)].astype(F32)
                y_ref[0, :, pl.ds(off, L)] = (hn * gain * og).astype(BF16)
        m_last = jnp.maximum(p_last, m)
        vw = (vaug.astype(F32) * rw_ref[d, head, g:g + 1, :]).astype(BF16)
        caug = jnp.exp(m - m_last) * caug + jnp.exp(p_last - m_last) * _dot(vw, k)
        return caug, b_tot + m_last

    def pair_step(seq, n, g0, i, st, mode):
        cf, mf, cb, mb = st
        cf, mf = chunk_step(0, seq, i, g0 + i, cf, mf, mode)
        cb, mb = chunk_step(1, seq, n - 1 - i, g0 + n - 1 - i, cb, mb, mode)
        return cf, mf, cb, mb

    ctx_seq = (kc_ref, qc_ref, vc_ref, hc_ref, ogc_ref, yc_ref)
    lat_seq = (kl_ref, ql_ref, vl_ref, hl_ref, ogl_ref, yl_ref)
    zc, zm = jnp.zeros((hd + AUG_ROWS, hd), F32), jnp.zeros((1, 1), F32)
    st = (zc, zm, zc, zm)
    for i in range(ncc):
        mode = None if not need_ctx else ("store" if i < ncc // 2 else "final")
        st = pair_step(ctx_seq, ncc, 0, i, st, mode)
    for i in range(ncl):
        st = pair_step(lat_seq, ncl, ncc, i, st, "store" if i < ncl // 2 else "final")


def _mlstm(gates_c, gates_l, kqv_c, kqv_l, og_c, og_l, gain, need_ctx):
    b, tc, md = kqv_c[0].shape
    tl = kqv_l[0].shape[1]
    n_gate = gates_c.shape[1]
    heads = n_gate // 4
    hd = md // heads
    L = CHUNK
    nct = (tc + tl) // L
    assert tc // L % 2 == 0 and tl // L % 2 == 0 and nct <= L
    nrow = -(-nct // SUBLANES) * SUBLANES
    g = jnp.concatenate([gates_c, gates_l], axis=2).reshape(b, 4, heads, nct, L)
    g = g.transpose(0, 2, 1, 3, 4)
    tok = lambda t: pl.BlockSpec((1, t, hd), lambda i, h: (i, 0, h))
    feat = lambda t: pl.BlockSpec((1, hd, t), lambda i, h: (i, h, 0))
    in_specs = [pl.BlockSpec((1, heads, 4, nct, L), lambda i, h: (i, 0, 0, 0, 0)),
                tok(tc), feat(tc), feat(tc), tok(tl), feat(tl), feat(tl), feat(tl),
                pl.BlockSpec((hd, 1), lambda i, h: (h, 0))]
    args = [g, *kqv_c, *kqv_l, og_l, gain.reshape(md, 1)]
    out_specs = [feat(tl)]
    out_shape = [jax.ShapeDtypeStruct((b, md, tl), BF16)]
    if need_ctx:
        in_specs.append(feat(tc))
        args.append(og_c)
        out_specs = [feat(tc)] + out_specs
        out_shape = [jax.ShapeDtypeStruct((b, md, tc), BF16)] + out_shape
    res = pl.pallas_call(
        functools.partial(_mlstm_kernel, need_ctx=need_ctx),
        grid=(b, heads),
        in_specs=in_specs,
        out_specs=out_specs,
        out_shape=out_shape,
        scratch_shapes=[
            pltpu.VMEM((2, heads, L, L), F32),
            pltpu.VMEM((2, heads, nrow, L), F32),
            pltpu.VMEM((2, heads, nrow, L), F32),
            pltpu.VMEM((2, heads, nrow, L), F32),
            pltpu.VMEM((hd, tc), F32),
            pltpu.VMEM((hd, tl), F32),
        ],
        compiler_params=_params("parallel", "arbitrary"),
        name="mlstm_scan",
    )(*args)
    return (res[0], res[1]) if need_ctx else (None, res[0])


def _mix_residual(h_ref, yc_ref, ym_ref, mod_ref, woc_ref, wom_ref, rows):
    mix = _dot(yc_ref[0, rows, :], woc_ref[...]) + _dot_tn(ym_ref[0, :, rows], wom_ref[...])
    return h_ref[0, rows, :] + mod_ref[0, 2:3, :] * mix


def _out_dense_kernel(h_ref, yc_ref, ym_ref, mod_ref, g2_ref, woc_ref, wom_ref, w1_ref, w3_ref, w2_ref,
                      *rest, chunks, final):
    if final:
        gf_ref, o_ref = rest
    else:
        (o_ref,) = rest
    h = _mix_residual(h_ref, yc_ref, ym_ref, mod_ref, woc_ref, wom_ref, slice(None))
    u = _norm_mod(h, g2_ref[...], mod_ref[0, 3:4, :], mod_ref[0, 4:5, :]).astype(BF16)
    acc = None
    for c0, c1 in chunks:
        a = _dot(u, w1_ref[:, c0:c1])
        hid = (a * jax.nn.sigmoid(a) * _dot(u, w3_ref[:, c0:c1])).astype(BF16)
        part = _dot(hid, w2_ref[c0:c1, :])
        acc = part if acc is None else acc + part
    h = h + mod_ref[0, 5:6, :] * acc
    if final:
        h = h * lax.rsqrt(jnp.mean(h * h, axis=-1, keepdims=True) + EPS) * gf_ref[...]
    o_ref[0] = h


def _out_moe_kernel(h_ref, yc_ref, ym_ref, mod_ref, g2_ref, woc_ref, wom_ref, router_ref,
                    hm_ref, u_ref, ti_ref, tg_ref, *, n_exp):
    for rows in _sub_tiles(h_ref.shape[1]):
        _out_moe_rows(h_ref, yc_ref, ym_ref, mod_ref, g2_ref, woc_ref, wom_ref, router_ref,
                      hm_ref, u_ref, ti_ref, tg_ref, rows, n_exp)


def _out_moe_rows(h_ref, yc_ref, ym_ref, mod_ref, g2_ref, woc_ref, wom_ref, router_ref,
                  hm_ref, u_ref, ti_ref, tg_ref, rows, n_exp):
    h = _mix_residual(h_ref, yc_ref, ym_ref, mod_ref, woc_ref, wom_ref, rows)
    hm_ref[0, rows, :] = h
    u = _norm_mod(h, g2_ref[...], mod_ref[0, 3:4, :], mod_ref[0, 4:5, :])
    _rows_to_tiles(u_ref, u, rows.start)
    u_hi = u.astype(BF16)
    u_lo = (u - u_hi.astype(F32)).astype(BF16)
    parts = _dot(u_hi, router_ref[...]) + _dot(u_lo, router_ref[...])
    logits = parts[:, :LANES] + parts[:, LANES:]
    n_pad = -(-n_exp // SUBLANES) * SUBLANES
    lt = logits.T[:n_pad, :]
    row = lax.broadcasted_iota(jnp.int32, lt.shape, 0)
    lt = jnp.where(row < n_exp, lt, -jnp.inf)
    v1 = jnp.max(lt, axis=0, keepdims=True)
    i1 = jnp.min(jnp.where(lt == v1, row, n_pad), axis=0, keepdims=True)
    rest = jnp.where(row == i1, -jnp.inf, lt)
    v2 = jnp.max(rest, axis=0, keepdims=True)
    i2 = jnp.min(jnp.where(rest == v2, row, n_pad), axis=0, keepdims=True)
    e2 = jnp.exp(v2 - v1)
    inv = 1.0 / (1.0 + e2)
    ti_ref[0, :, rows] = jnp.concatenate([i1, i2], axis=0)
    tg_ref[0, :, rows] = jnp.concatenate([inv, e2 * inv], axis=0)


def _out_common(h, yc, ym, mod, mod_row, norm_g, wts):
    b, t, d = h.shape
    tm = min(ROW_TILE, t)
    const = lambda shape: pl.BlockSpec(shape, lambda i, j: (0,) * len(shape))
    tile = lambda w: pl.BlockSpec((1, tm, w), lambda i, j: (i, j, 0))
    in_specs = [
        tile(d), tile(yc.shape[2]), pl.BlockSpec((1, ym.shape[1], tm), lambda i, j: (i, 0, j)),
        pl.BlockSpec((1, 6, d), (lambda i, j: (i, 0, 0)) if mod_row is None
                     else (lambda i, j: (mod_row, 0, 0))),
        const((1, d)), const(wts["woc"].shape), const(wts["wom"].shape),
    ]
    args = [h, yc, ym, mod, norm_g, wts["woc"], wts["wom"]]
    return b, t, d, tm, const, tile, in_specs, args


def _out_dense(h, yc, ym, mod, mod_row, norm_g, wts, ffn, final_g):
    b, t, d, tm, const, tile, in_specs, args = _out_common(h, yc, ym, mod, mod_row, norm_g, wts)
    w1, w3, w2 = ffn
    in_specs += [const(w1.shape), const(w3.shape), const(w2.shape)]
    args += [w1, w3, w2]
    if final_g is not None:
        in_specs.append(const((1, d)))
        args.append(final_g)
    return pl.pallas_call(
        functools.partial(_out_dense_kernel, chunks=_ff_chunks(w1.shape[1]), final=final_g is not None),
        grid=(b, t // tm),
        in_specs=in_specs,
        out_specs=tile(d),
        out_shape=jax.ShapeDtypeStruct((b, t, d), F32),
        compiler_params=_params("parallel", "parallel"),
        name="out_dense_ffn",
    )(*args)


def _out_moe(h, yc, ym, mod, mod_row, norm_g, wts, router_pad, n_exp):
    b, t, d, tm, const, tile, in_specs, args = _out_common(h, yc, ym, mod, mod_row, norm_g, wts)
    in_specs.append(const(router_pad.shape))
    args.append(router_pad)
    per = d // LANES
    top2 = pl.BlockSpec((1, TOP_K, tm), lambda i, j: (i, 0, j))
    return pl.pallas_call(
        functools.partial(_out_moe_kernel, n_exp=n_exp),
        grid=(b, t // tm),
        in_specs=in_specs,
        out_specs=[tile(d), pl.BlockSpec((tm * per, LANES), lambda i, j: (i * (t // tm) + j, 0)),
                   top2, top2],
        out_shape=[jax.ShapeDtypeStruct((b, t, d), F32), jax.ShapeDtypeStruct((b * t * per, LANES), F32),
                   jax.ShapeDtypeStruct((b, TOP_K, t), jnp.int32), jax.ShapeDtypeStruct((b, TOP_K, t), F32)],
        compiler_params=_params("parallel", "parallel"),
        name="out_router",
    )(*args)


ISSUE_UNROLL = 8


def _expert_kernel(be_ref, nu_ref, tok_ref, toknext_ref, u_hbm, w1_ref, w3_ref, w2_ref, y_ref, xbuf, sem,
                   *, chunks, per):
    i = pl.program_id(0)
    n_used = nu_ref[0]
    blk = xbuf.shape[1] // per
    slot = i % 2

    def row_copy(idx_ref, r, to_slot):
        src = pl.multiple_of(idx_ref[0, 0, r], per)
        return pltpu.make_async_copy(u_hbm.at[pl.ds(src, per), :],
                                     xbuf.at[to_slot, pl.ds(r * per, per), :], sem.at[to_slot])

    @pl.when(i == 0)
    def _():
        def body(r, c):
            row_copy(tok_ref, r, 0).start()
            return c
        lax.fori_loop(0, blk, body, 0, unroll=ISSUE_UNROLL)

    @pl.when(i <= n_used)
    def _():
        pltpu.make_async_copy(u_hbm.at[pl.ds(0, blk * per), :], xbuf.at[slot], sem.at[slot]).wait()

    @pl.when(i < n_used)
    def _():
        for r in range(blk):
            row_copy(toknext_ref, r, 1 - slot).start()
        x = _tiles_to_rows(xbuf.at[slot], blk, per).astype(BF16)
        acc = None
        for c0, c1 in chunks:
            a = _dot(x, w1_ref[0, :, c0:c1])
            hid = (a * jax.nn.sigmoid(a) * _dot(x, w3_ref[0, :, c0:c1])).astype(BF16)
            part = _dot(hid, w2_ref[0, c0:c1, :])
            acc = part if acc is None else acc + part
        _rows_to_tiles(y_ref, acc)

    @pl.when(i >= n_used)
    def _():
        y_ref[...] = jnp.zeros(y_ref.shape, F32)


def _experts(u_tiles, tok_buf, blk_e, n_used, w1, w3, w2):
    d, f = w1.shape[1], w1.shape[2]
    per = d // LANES
    nblk = tok_buf.shape[0] // MOE_BLOCK
    tok3 = (tok_buf * per).reshape(nblk, 1, MOE_BLOCK)
    smem = lambda fn: pl.BlockSpec((1, 1, MOE_BLOCK), fn, memory_space=pltpu.SMEM)
    grid_spec = pltpu.PrefetchScalarGridSpec(
        num_scalar_prefetch=2,
        grid=(nblk,),
        in_specs=[
            smem(lambda i, be, nu: (i, 0, 0)),
            smem(lambda i, be, nu: (jnp.minimum(i + 1, nblk - 1), 0, 0)),
            pl.BlockSpec(memory_space=pl.ANY),
            pl.BlockSpec((1, d, f), lambda i, be, nu: (be[i], 0, 0)),
            pl.BlockSpec((1, d, f), lambda i, be, nu: (be[i], 0, 0)),
            pl.BlockSpec((1, f, d), lambda i, be, nu: (be[i], 0, 0)),
        ],
        out_specs=pl.BlockSpec((MOE_BLOCK * per, LANES), lambda i, be, nu: (i, 0)),
        scratch_shapes=[pltpu.VMEM((2, MOE_BLOCK * per, LANES), F32), pltpu.SemaphoreType.DMA((2,))],
    )
    return pl.pallas_call(
        functools.partial(_expert_kernel, chunks=_ff_chunks(f), per=per),
        grid_spec=grid_spec,
        out_shape=jax.ShapeDtypeStruct((nblk * MOE_BLOCK * per, LANES), F32),
        compiler_params=_params("arbitrary"),
        name="moe_experts",
    )(blk_e, n_used, tok3, tok3, u_tiles, w1, w3, w2)


def _combine_kernel(dst_ref, dnext_ref, h_ref, gate_ref, mod_ref, *rest, final):
    if final:
        gf_ref, y_hbm, o_ref, ybuf, sem = rest
    else:
        y_hbm, o_ref, ybuf, sem = rest
    i = pl.program_id(0)
    n = pl.num_programs(0)
    tc, d = h_ref.shape
    per = d // LANES

    def issue(idx_ref, slot):
        def body(r, c):
            for k in range(TOP_K):
                src = pl.multiple_of(idx_ref[0, 0, TOP_K * r + k], per)
                pltpu.make_async_copy(y_hbm.at[pl.ds(src, per), :],
                                      ybuf.at[slot, k, pl.ds(pl.multiple_of(r * per, per), per), :],
                                      sem.at[slot]).start()
            return c
        lax.fori_loop(0, tc, body, 0, unroll=ISSUE_UNROLL)

    @pl.when(i == 0)
    def _():
        issue(dst_ref, 0)

    slot = i % 2

    @pl.when(i + 1 < n)
    def _():
        issue(dnext_ref, 1 - slot)

    for k in range(TOP_K):
        pltpu.make_async_copy(y_hbm.at[pl.ds(0, tc * per), :], ybuf.at[slot, k], sem.at[slot]).wait()

    gate = gate_ref[...]
    y = (gate[:, 0:1] * _tiles_to_rows(ybuf.at[slot, 0], tc, per)
         + gate[:, 1:2] * _tiles_to_rows(ybuf.at[slot, 1], tc, per))
    h = h_ref[...] + mod_ref[0, 5:6, :] * y
    if final:
        h = h * lax.rsqrt(jnp.mean(h * h, axis=-1, keepdims=True) + EPS) * gf_ref[...]
    o_ref[...] = h


def _combine(h_flat, y, dest, gates, mod, mod_row, tokens_per_batch, final_g):
    n_tok, d = h_flat.shape
    tc = COMBINE_TILE
    steps = n_tok // tc
    per_b = tokens_per_batch // tc
    per = d // LANES
    dst2 = (dest * per).reshape(steps, 1, TOP_K * tc)
    smem = lambda fn: pl.BlockSpec((1, 1, TOP_K * tc), fn, memory_space=pltpu.SMEM)
    in_specs = [
        smem(lambda i: (i, 0, 0)),
        smem(lambda i: (jnp.minimum(i + 1, steps - 1), 0, 0)),
        pl.BlockSpec((tc, d), lambda i: (i, 0)),
        pl.BlockSpec((tc, TOP_K), lambda i: (i, 0)),
        pl.BlockSpec((1, 6, d), (lambda i: (i // per_b, 0, 0)) if mod_row is None
                     else (lambda i: (mod_row, 0, 0))),
    ]
    args = [dst2, dst2, h_flat, gates, mod]
    if final_g is not None:
        in_specs.append(pl.BlockSpec((1, d), lambda i: (0, 0)))
        args.append(final_g)
    in_specs.append(pl.BlockSpec(memory_space=pl.ANY))
    args.append(y)
    return pl.pallas_call(
        functools.partial(_combine_kernel, final=final_g is not None),
        grid=(steps,),
        in_specs=in_specs,
        out_specs=pl.BlockSpec((tc, d), lambda i: (i, 0)),
        out_shape=jax.ShapeDtypeStruct((n_tok, d), F32),
        scratch_shapes=[pltpu.VMEM((2, TOP_K, tc * per, LANES), F32), pltpu.SemaphoreType.DMA((2,))],
        compiler_params=_params("arbitrary"),
        name="moe_combine",
    )(*args)


def _routing(top_i, n_exp):
    e = top_i.reshape(-1)
    a = e.shape[0]
    experts = jnp.arange(n_exp, dtype=jnp.int32)
    onehot = (e[:, None] == experts[None, :]).astype(jnp.int32)
    csum = jnp.cumsum(onehot, axis=0)
    rank = jnp.sum(onehot * csum, axis=1) - 1
    counts = csum[-1]
    padded = (counts + MOE_BLOCK - 1) // MOE_BLOCK * MOE_BLOCK
    pend = jnp.cumsum(padded)
    pstart = pend - padded
    start = jnp.cumsum(counts) - counts
    dest = pstart[e] + rank
    cap = -(-a // MOE_BLOCK) * MOE_BLOCK + n_exp * MOE_BLOCK
    nblk = cap // MOE_BLOCK
    blk_first = jnp.arange(nblk, dtype=jnp.int32) * MOE_BLOCK
    blk_e = jnp.minimum(jnp.sum((pend[None, :] <= blk_first[:, None]).astype(jnp.int32), axis=1), n_exp - 1)
    n_used = (pend[-1] // MOE_BLOCK).astype(jnp.int32).reshape(1)
    order = jnp.sort(e * a + jnp.arange(a, dtype=jnp.int32)) % a
    row = jnp.arange(cap, dtype=jnp.int32)
    row_e = jnp.repeat(blk_e, MOE_BLOCK)
    within = row - pstart[row_e]
    valid = (within < counts[row_e]) & (row < pend[-1])
    tok_buf = jnp.where(valid, order[jnp.clip(start[row_e] + within, 0, a - 1)] // TOP_K, 0)
    return dest.astype(jnp.int32), tok_buf.astype(jnp.int32), blk_e.astype(jnp.int32), n_used


def _moe_mixer(h_mid, u, top_i, top_g, mod, mod_row, moe_w, n_exp, final_g):
    b, t, d = h_mid.shape
    n_tok = b * t
    dest, tok_buf, blk_e, n_used = _routing(top_i, n_exp)
    y = _experts(u, tok_buf, blk_e, n_used, *moe_w)
    out = _combine(h_mid.reshape(n_tok, d), y, dest, top_g.reshape(n_tok, TOP_K), mod, mod_row, t, final_g)
    return out.reshape(b, t, d)


def _layer_weights(w_in_l, conv_w_l, b_gates_l, w_out_l):
    cd = conv_w_l.shape[1]
    n_gate = b_gates_l.size
    md = (w_in_l.shape[1] - 3 * cd - n_gate) // 4
    q0 = 3 * cd
    g0 = q0 + 3 * md
    return {
        "wc": w_in_l[:, :q0].astype(BF16),
        "wq": w_in_l[:, q0:q0 + md].T.astype(BF16),
        "wk": w_in_l[:, q0 + md:q0 + 2 * md].astype(BF16),
        "wv": w_in_l[:, q0 + 2 * md:g0].T.astype(BF16),
        "wg": w_in_l[:, g0:g0 + n_gate].T.astype(BF16),
        "bg": b_gates_l.reshape(n_gate, 1),
        "wo": w_in_l[:, g0 + n_gate:].T.astype(BF16),
        "cw": conv_w_l,
        "woc": w_out_l[:cd].astype(BF16),
        "wom": w_out_l[cd:].astype(BF16),
    }


def kernel(x, c, ctx, c_ctx, norm1_g, norm2_g, w_ada, b_ada, w_in, conv_w, b_gates, mlstm_norm_g, w_out,
           ffn_w1, ffn_w3, ffn_w2, moe_router, moe_w1, moe_w3, moe_w2, final_norm_g):
    b, t, d = x.shape
    tc = ctx.shape[1]
    depth = w_in.shape[0]
    n_exp = moe_router.shape[-1]
    assert t % GRID_W == 0 and t % CHUNK == 0 and tc % CHUNK == 0 and d % (SUBLANES * LANES) == 0

    ctx_row = b
    n_rows = -(-(b + 1) // SUBLANES) * SUBLANES
    c_all = jnp.concatenate([c, c_ctx[None, :], jnp.zeros((n_rows - b - 1, d), F32)], axis=0)
    mod_all = _ada(c_all, w_ada, b_ada).reshape(depth, n_rows, 6, d)
    final_g = final_norm_g.reshape(1, d)

    h, hc = x, ctx
    for layer in range(depth):
        last = layer == depth - 1
        j = layer // 2
        mod = mod_all[layer]
        wts = _layer_weights(w_in[layer], conv_w[layer], b_gates[layer], w_out[layer])
        n1 = norm1_g[layer].reshape(1, d)
        n2 = norm2_g[layer].reshape(1, d)

        ql, kl, vl, gl, ycl, ogl = _in_proj(h, mod, None, n1, wts, GRID_W, True)
        res_c = _in_proj(hc, mod, ctx_row, n1, wts, tc, not last)
        qc, kc, vc, gc = res_c[:4]
        ogc = None if last else res_c[5]
        ymc, yml = _mlstm(gc, gl, (kc, qc, vc), (kl, ql, vl), ogc, ogl,
                          mlstm_norm_g[layer].reshape(1, -1), not last)

        def mixer(hh, yc_, ym_, mod_row, fin):
            if layer % 2 == 0:
                ffn = (ffn_w1[j].astype(BF16), ffn_w3[j].astype(BF16), ffn_w2[j].astype(BF16))
                return _out_dense(hh, yc_, ym_, mod, mod_row, n2, wts, ffn, fin)
            r_full = jnp.pad(moe_router[j], ((0, 0), (0, LANES - n_exp)))
            r_hi = r_full.astype(BF16)
            r_lo = (r_full - r_hi.astype(F32)).astype(BF16)
            router_pad = jnp.concatenate([r_hi, r_lo], axis=1)
            h_mid, u2, top_i, top_g = _out_moe(hh, yc_, ym_, mod, mod_row, n2, wts, router_pad, n_exp)
            top_i, top_g = top_i.transpose(0, 2, 1), top_g.transpose(0, 2, 1)
            moe_w = (moe_w1[j].astype(BF16), moe_w3[j].astype(BF16), moe_w2[j].astype(BF16))
            return _moe_mixer(h_mid, u2, top_i, top_g, mod, mod_row, moe_w, n_exp, fin)

        h = mixer(h, ycl, yml, None, final_g if last else None)
        if not last:
            hc = mixer(hc, res_c[4], ymc, ctx_row, None)
    return h
```

```python
import functools

import jax
import jax.numpy as jnp
from jax import lax
from jax.experimental import pallas as pl
from jax.experimental.pallas import tpu as pltpu

GRID_W = 64
CHUNK = 128
EPS = 1e-6
TOP_K = 2
ROW_TILE = 512
MOE_BLOCK = 512
COMBINE_TILE = 512
SUB_TILE = 256
LANES = 128
SUBLANES = 8
AUG_ROWS = 32
VMEM_LIMIT = 56 * 1024 * 1024

F32 = jnp.float32
BF16 = jnp.bfloat16
HIGHEST = lax.Precision.HIGHEST


def _dot(a, b):
    return jnp.dot(a, b, preferred_element_type=F32)


def _dot_nt(a, b):
    return lax.dot_general(a, b, (((1,), (1,)), ((), ())), preferred_element_type=F32)


def _dot_tn(a, b):
    return lax.dot_general(a, b, (((0,), (0,)), ((), ())), preferred_element_type=F32)


def _rows_to_tiles(ref, x, row0=0):
    n, d = x.shape
    per = d // LANES
    for s in range(per):
        ref[pl.ds(row0 * per + s, n, stride=per), :] = x[:, s * LANES:(s + 1) * LANES]


def _tiles_to_rows(ref, n, per):
    return jnp.concatenate([ref[pl.ds(s, n, stride=per), :] for s in range(per)], axis=1)


def _sub_tiles(tm):
    sub = min(SUB_TILE, tm)
    return [slice(r0, r0 + sub) for r0 in range(0, tm, sub)]


def _params(*sem):
    return pltpu.CompilerParams(dimension_semantics=sem, vmem_limit_bytes=VMEM_LIMIT)


def _ff_chunks(d_ff, max_chunk=1024):
    out, c0 = [], 0
    while c0 < d_ff:
        c1 = min(c0 + max_chunk, d_ff)
        out.append((c0, c1))
        c0 = c1
    return out


def _ada_kernel(c_ref, w_ref, b_ref, o_ref):
    s = c_ref[...]
    s = s * jax.nn.sigmoid(s)
    o_ref[0] = jnp.dot(s, w_ref[0], precision=HIGHEST, preferred_element_type=F32) + b_ref[0]


def _ada(c_all, w_ada, b_ada):
    depth, d, d6 = w_ada.shape
    rows = c_all.shape[0]
    tn = d6 // 4
    return pl.pallas_call(
        _ada_kernel,
        grid=(depth, d6 // tn),
        in_specs=[
            pl.BlockSpec((rows, d), lambda l, j: (0, 0)),
            pl.BlockSpec((1, d, tn), lambda l, j: (l, 0, j)),
            pl.BlockSpec((1, 1, tn), lambda l, j: (l, 0, j)),
        ],
        out_specs=pl.BlockSpec((1, rows, tn), lambda l, j: (l, 0, j)),
        out_shape=jax.ShapeDtypeStruct((depth, rows, d6), F32),
        compiler_params=_params("parallel", "parallel"),
        name="adaln_mod",
    )(c_all, w_ada, b_ada.reshape(depth, 1, d6))


def _norm_mod(x, g, shift, scale):
    y = x * lax.rsqrt(jnp.mean(x * x, axis=-1, keepdims=True) + EPS)
    return (y * g) * (1.0 + scale) + shift


def _in_kernel(x_ref, mod_ref, g_ref, wq_ref, wk_ref, wv_ref, wg_ref, bg_ref, *rest,
               row_w, full, kscale):
    if full:
        wc_ref, wo_ref, cw_ref, q_ref, k_ref, v_ref, gt_ref, yc_ref, og_ref = rest
    else:
        q_ref, k_ref, v_ref, gt_ref = rest
    for rows in _sub_tiles(x_ref.shape[1]):
        u = _norm_mod(x_ref[0, rows, :], g_ref[...], mod_ref[0, 0:1, :], mod_ref[0, 1:2, :]).astype(BF16)
        q_ref[0, :, rows] = _dot_nt(wq_ref[...], u).astype(BF16)
        k_ref[0, rows, :] = (_dot(u, wk_ref[...]) * kscale).astype(BF16)
        v_ref[0, :, rows] = _dot_nt(wv_ref[...], u).astype(BF16)
        gt_ref[0, :, rows] = _dot_nt(wg_ref[...], u) + bg_ref[...]
        if full:
            cd = cw_ref.shape[1]
            c3 = _dot(u, wc_ref[...])
            z = c3[:, cd:2 * cd] * c3[:, 2 * cd:]
            n = z.shape[0]
            t = lax.broadcasted_iota(jnp.int32, (n, 1), 0) % row_w
            zprev = jnp.where(t == 0, 0.0, pltpu.roll(z, 1, axis=0))
            znext = jnp.where(t == row_w - 1, 0.0, pltpu.roll(z, n - 1, axis=0))
            conv = zprev * cw_ref[0:1, :] + z * cw_ref[1:2, :] + znext * cw_ref[2:3, :]
            yc_ref[0, rows, :] = (c3[:, :cd] * conv).astype(BF16)
            og_ref[0, :, rows] = jax.nn.sigmoid(_dot_nt(wo_ref[...], u)).astype(BF16)


def _in_proj(h, mod, mod_row, norm_g, wts, row_w, full):
    b, t, d = h.shape
    tm = min(ROW_TILE, t)
    md = wts["wk"].shape[1]
    n_gate = wts["bg"].shape[0]
    head_dim = md // (n_gate // 4)
    const = lambda shape: pl.BlockSpec(shape, lambda i, j: (0,) * len(shape))
    tile = lambda w: pl.BlockSpec((1, tm, w), lambda i, j: (i, j, 0))
    tile_t = pl.BlockSpec((1, md, tm), lambda i, j: (i, 0, j))
    feat_major = jax.ShapeDtypeStruct((b, md, t), BF16)
    in_specs = [
        tile(d),
        pl.BlockSpec((1, 6, d), (lambda i, j: (i, 0, 0)) if mod_row is None
                     else (lambda i, j: (mod_row, 0, 0))),
        const((1, d)),
        const(wts["wq"].shape), const(wts["wk"].shape), const(wts["wv"].shape),
        const(wts["wg"].shape), const(wts["bg"].shape),
    ]
    args = [h, mod, norm_g, wts["wq"], wts["wk"], wts["wv"], wts["wg"], wts["bg"]]
    out_specs = [tile_t, tile(md), tile_t, pl.BlockSpec((1, n_gate, tm), lambda i, j: (i, 0, j))]
    out_shape = [feat_major, jax.ShapeDtypeStruct((b, t, md), BF16), feat_major,
                 jax.ShapeDtypeStruct((b, n_gate, t), F32)]
    if full:
        cd = wts["cw"].shape[1]
        in_specs += [const(wts["wc"].shape), const(wts["wo"].shape), const(wts["cw"].shape)]
        args += [wts["wc"], wts["wo"], wts["cw"]]
        out_specs += [tile(cd), tile_t]
        out_shape += [jax.ShapeDtypeStruct((b, t, cd), BF16), feat_major]
    return pl.pallas_call(
        functools.partial(_in_kernel, row_w=row_w, full=full, kscale=head_dim ** -0.5),
        grid=(b, t // tm),
        in_specs=in_specs,
        out_specs=out_specs,
        out_shape=out_shape,
        compiler_params=_params("parallel", "parallel"),
        name="in_proj",
    )(*args)


def _lane_scan(x, op, ident, reverse):
    n = x.shape[1]
    lane = lax.broadcasted_iota(jnp.int32, x.shape, 1)
    s = 1
    while s < n:
        if reverse:
            shifted = jnp.where(lane < n - s, pltpu.roll(x, n - s, axis=1), ident)
        else:
            shifted = jnp.where(lane >= s, pltpu.roll(x, s, axis=1), ident)
        x = op(x, shifted)
        s *= 2
    return x


def _mlstm_kernel(gr_ref, kc_ref, qc_ref, vc_ref, kl_ref, ql_ref, vl_ref, ogl_ref, gain_ref,
                  *rest, need_ctx):
    if need_ctx:
        ogc_ref, yc_ref, yl_ref, rt_ref, ra_ref, rp_ref, rw_ref, s_ref, upd_ref, st_ref = rest
    else:
        yl_ref, rt_ref, ra_ref, rp_ref, rw_ref, s_ref, upd_ref, st_ref = rest
        ogc_ref = yc_ref = None
    L = CHUNK
    hd = kl_ref.shape[2]
    ncc = kc_ref.shape[1] // L
    ncl = kl_ref.shape[1] // L
    nct = ncc + ncl

    head = pl.program_id(1)
    n_heads = gr_ref.shape[1]

    @pl.when(head == 0)
    def _():
        for d in range(2):
            rev = d == 1
            last = 0 if rev else L - 1
            for hh in range(n_heads):
                ig = gr_ref[0, hh, 2 * d]
                lf = jax.nn.log_sigmoid(gr_ref[0, hh, 2 * d + 1])
                a = _lane_scan(lf, jnp.add, 0.0, rev)
                r = ig - a
                p = _lane_scan(r, jnp.maximum, -jnp.inf, rev)
                ra_ref[d, hh, 0:nct, :] = a
                rp_ref[d, hh, 0:nct, :] = p
                rw_ref[d, hh, 0:nct, :] = jnp.exp(r - p[:, last:last + 1])
                r_pad = jnp.concatenate([r, jnp.zeros((L - nct, L), F32)], axis=0)
                rt_ref[d, hh] = r_pad.T

    spos = lax.broadcasted_iota(jnp.int32, (L, L), 0)
    tpos = lax.broadcasted_iota(jnp.int32, (L, L), 1)
    ones_row = jnp.where(lax.broadcasted_iota(jnp.int32, (AUG_ROWS, L), 0) == 0, 1.0, 0.0).astype(BF16)
    gain = gain_ref[...]

    ctx_seq = (kc_ref, qc_ref, vc_ref, ogc_ref, yc_ref, ncc, 0, need_ctx)
    lat_seq = (kl_ref, ql_ref, vl_ref, ogl_ref, yl_ref, ncl, ncc, True)

    for k_ref, q_ref, v_ref, _, _, n, g0, want in (ctx_seq, lat_seq):
        for j in range(n):
            g = g0 + j
            k = k_ref[0, pl.ds(j * L, L), :]
            if want:
                s_ref[g] = _dot(k, q_ref[0, :, pl.ds(j * L, L)])
            vaug = jnp.concatenate([v_ref[0, :, pl.ds(j * L, L)], ones_row], axis=0).astype(F32)
            for d in range(2):
                vw = (vaug * rw_ref[d, head, g:g + 1, :]).astype(BF16)
                upd_ref[d, g] = _dot(vw, k)

    m_in = {}
    for d in range(2):
        last = 0 if d == 1 else L - 1
        caug, m = jnp.zeros((hd + AUG_ROWS, hd), F32), jnp.zeros((1, 1), F32)
        for _, _, _, _, _, n, g0, _ in (ctx_seq, lat_seq):
            for i in range(n):
                g = g0 + (n - 1 - i if d == 1 else i)
                st_ref[d, g] = caug.astype(BF16)
                m_in[d, g] = m
                b_tot = ra_ref[d, head, g:g + 1, last:last + 1]
                p_last = rp_ref[d, head, g:g + 1, last:last + 1]
                m_last = jnp.maximum(p_last, m)
                caug = jnp.exp(m - m_last) * caug + jnp.exp(p_last - m_last) * upd_ref[d, g]
                m = b_tot + m_last

    for k_ref, q_ref, v_ref, og_ref, y_ref, n, g0, want in (ctx_seq, lat_seq):
        for j in range(n if want else 0):
            g = g0 + j
            lanes = pl.ds(j * L, L)
            q = q_ref[0, :, lanes]
            vaug = jnp.concatenate([v_ref[0, :, lanes], ones_row], axis=0)
            rhs, scale = [], []
            for d in range(2):
                a_row = ra_ref[d, head, g:g + 1, :]
                p_row = rp_ref[d, head, g:g + 1, :]
                m = m_in[d, g]
                mask = (spos >= tpos) if d == 1 else (spos <= tpos)
                w = jnp.where(mask, jnp.exp(rt_ref[d, head, :, g:g + 1] - p_row), 0.0)
                m_row = jnp.maximum(p_row, m)
                sw = (s_ref[g] * w * jnp.exp(p_row - m_row)).astype(BF16)
                qs = (q.astype(F32) * jnp.exp(m - m_row)).astype(BF16)
                rhs.append((sw, qs))
                scale.append(jnp.exp(-(a_row + m_row)))
            zero = jnp.zeros((hd, L), BF16)
            nd = _dot(jnp.concatenate([vaug, st_ref[0, g], st_ref[1, g]], axis=1),
                      jnp.concatenate([jnp.concatenate([rhs[0][0], rhs[1][0]], axis=1),
                                       jnp.concatenate([rhs[0][1], zero], axis=1),
                                       jnp.concatenate([zero, rhs[1][1]], axis=1)], axis=0))
            hh = None
            for d in range(2):
                part = nd[:, d * L:(d + 1) * L]
                den = jnp.maximum(jnp.abs(part[hd:hd + 1, :]), scale[d])
                hout = part[:hd, :] * (1.0 / den)
                hh = hout if hh is None else hh + hout
            hn = hh * lax.rsqrt(jnp.mean(hh * hh, axis=0, keepdims=True) + EPS)
            y_ref[0, :, lanes] = (hn * gain * og_ref[0, :, lanes].astype(F32)).astype(BF16)


def _mlstm(gates_c, gates_l, kqv_c, kqv_l, og_c, og_l, gain, need_ctx):
    b, tc, md = kqv_c[0].shape
    tl = kqv_l[0].shape[1]
    n_gate = gates_c.shape[1]
    heads = n_gate // 4
    hd = md // heads
    L = CHUNK
    nct = (tc + tl) // L
    assert nct <= L
    nrow = -(-nct // SUBLANES) * SUBLANES
    g = jnp.concatenate([gates_c, gates_l], axis=2).reshape(b, 4, heads, nct, L)
    g = g.transpose(0, 2, 1, 3, 4)
    tok = lambda t: pl.BlockSpec((1, t, hd), lambda i, h: (i, 0, h))
    feat = lambda t: pl.BlockSpec((1, hd, t), lambda i, h: (i, h, 0))
    in_specs = [pl.BlockSpec((1, heads, 4, nct, L), lambda i, h: (i, 0, 0, 0, 0)),
                tok(tc), feat(tc), feat(tc), tok(tl), feat(tl), feat(tl), feat(tl),
                pl.BlockSpec((hd, 1), lambda i, h: (h, 0))]
    args = [g, *kqv_c, *kqv_l, og_l, gain.reshape(md, 1)]
    out_specs = [feat(tl)]
    out_shape = [jax.ShapeDtypeStruct((b, md, tl), BF16)]
    if need_ctx:
        in_specs.append(feat(tc))
        args.append(og_c)
        out_specs = [feat(tc)] + out_specs
        out_shape = [jax.ShapeDtypeStruct((b, md, tc), BF16)] + out_shape
    res = pl.pallas_call(
        functools.partial(_mlstm_kernel, need_ctx=need_ctx),
        grid=(b, heads),
        in_specs=in_specs,
        out_specs=out_specs,
        out_shape=out_shape,
        scratch_shapes=[
            pltpu.VMEM((2, heads, L, L), F32),
            pltpu.VMEM((2, heads, nrow, L), F32),
            pltpu.VMEM((2, heads, nrow, L), F32),
            pltpu.VMEM((2, heads, nrow, L), F32),
            pltpu.VMEM((nct, L, L), F32),
            pltpu.VMEM((2, nct, hd + AUG_ROWS, hd), F32),
            pltpu.VMEM((2, nct, hd + AUG_ROWS, hd), BF16),
        ],
        compiler_params=_params("parallel", "arbitrary"),
        name="mlstm_scan",
    )(*args)
    return (res[0], res[1]) if need_ctx else (None, res[0])


def _mix_residual(h_ref, yc_ref, ym_ref, mod_ref, woc_ref, wom_ref, rows):
    mix = _dot(yc_ref[0, rows, :], woc_ref[...]) + _dot_tn(ym_ref[0, :, rows], wom_ref[...])
    return h_ref[0, rows, :] + mod_ref[0, 2:3, :] * mix


def _out_dense_kernel(h_ref, yc_ref, ym_ref, mod_ref, g2_ref, woc_ref, wom_ref, w1_ref, w3_ref, w2_ref,
                      *rest, chunks, final):
    if final:
        gf_ref, o_ref = rest
    else:
        (o_ref,) = rest
    h = _mix_residual(h_ref, yc_ref, ym_ref, mod_ref, woc_ref, wom_ref, slice(None))
    u = _norm_mod(h, g2_ref[...], mod_ref[0, 3:4, :], mod_ref[0, 4:5, :]).astype(BF16)
    acc = None
    for c0, c1 in chunks:
        a = _dot(u, w1_ref[:, c0:c1])
        hid = (a * jax.nn.sigmoid(a) * _dot(u, w3_ref[:, c0:c1])).astype(BF16)
        part = _dot(hid, w2_ref[c0:c1, :])
        acc = part if acc is None else acc + part
    h = h + mod_ref[0, 5:6, :] * acc
    if final:
        h = h * lax.rsqrt(jnp.mean(h * h, axis=-1, keepdims=True) + EPS) * gf_ref[...]
    o_ref[0] = h


def _out_moe_kernel(h_ref, yc_ref, ym_ref, mod_ref, g2_ref, woc_ref, wom_ref, router_ref,
                    hm_ref, u_ref, ti_ref, tg_ref, *, n_exp):
    for rows in _sub_tiles(h_ref.shape[1]):
        _out_moe_rows(h_ref, yc_ref, ym_ref, mod_ref, g2_ref, woc_ref, wom_ref, router_ref,
                      hm_ref, u_ref, ti_ref, tg_ref, rows, n_exp)


def _out_moe_rows(h_ref, yc_ref, ym_ref, mod_ref, g2_ref, woc_ref, wom_ref, router_ref,
                  hm_ref, u_ref, ti_ref, tg_ref, rows, n_exp):
    h = _mix_residual(h_ref, yc_ref, ym_ref, mod_ref, woc_ref, wom_ref, rows)
    hm_ref[0, rows, :] = h
    u = _norm_mod(h, g2_ref[...], mod_ref[0, 3:4, :], mod_ref[0, 4:5, :])
    _rows_to_tiles(u_ref, u, rows.start)
    u_hi = u.astype(BF16)
    u_lo = (u - u_hi.astype(F32)).astype(BF16)
    parts = _dot(u_hi, router_ref[...]) + _dot(u_lo, router_ref[...])
    logits = parts[:, :LANES] + parts[:, LANES:]
    n_pad = -(-n_exp // SUBLANES) * SUBLANES
    lt = logits.T[:n_pad, :]
    row = lax.broadcasted_iota(jnp.int32, lt.shape, 0)
    lt = jnp.where(row < n_exp, lt, -jnp.inf)
    v1 = jnp.max(lt, axis=0, keepdims=True)
    i1 = jnp.min(jnp.where(lt == v1, row, n_pad), axis=0, keepdims=True)
    rest = jnp.where(row == i1, -jnp.inf, lt)
    v2 = jnp.max(rest, axis=0, keepdims=True)
    i2 = jnp.min(jnp.where(rest == v2, row, n_pad), axis=0, keepdims=True)
    e2 = jnp.exp(v2 - v1)
    inv = 1.0 / (1.0 + e2)
    ti_ref[0, :, rows] = jnp.concatenate([i1, i2], axis=0)
    tg_ref[0, :, rows] = jnp.concatenate([inv, e2 * inv], axis=0)


def _out_common(h, yc, ym, mod, mod_row, norm_g, wts):
    b, t, d = h.shape
    tm = min(ROW_TILE, t)
    const = lambda shape: pl.BlockSpec(shape, lambda i, j: (0,) * len(shape))
    tile = lambda w: pl.BlockSpec((1, tm, w), lambda i, j: (i, j, 0))
    in_specs = [
        tile(d), tile(yc.shape[2]), pl.BlockSpec((1, ym.shape[1], tm), lambda i, j: (i, 0, j)),
        pl.BlockSpec((1, 6, d), (lambda i, j: (i, 0, 0)) if mod_row is None
                     else (lambda i, j: (mod_row, 0, 0))),
        const((1, d)), const(wts["woc"].shape), const(wts["wom"].shape),
    ]
    args = [h, yc, ym, mod, norm_g, wts["woc"], wts["wom"]]
    return b, t, d, tm, const, tile, in_specs, args


def _out_dense(h, yc, ym, mod, mod_row, norm_g, wts, ffn, final_g):
    b, t, d, tm, const, tile, in_specs, args = _out_common(h, yc, ym, mod, mod_row, norm_g, wts)
    w1, w3, w2 = ffn
    in_specs += [const(w1.shape), const(w3.shape), const(w2.shape)]
    args += [w1, w3, w2]
    if final_g is not None:
        in_specs.append(const((1, d)))
        args.append(final_g)
    return pl.pallas_call(
        functools.partial(_out_dense_kernel, chunks=_ff_chunks(w1.shape[1]), final=final_g is not None),
        grid=(b, t // tm),
        in_specs=in_specs,
        out_specs=tile(d),
        out_shape=jax.ShapeDtypeStruct((b, t, d), F32),
        compiler_params=_params("parallel", "parallel"),
        name="out_dense_ffn",
    )(*args)


def _out_moe(h, yc, ym, mod, mod_row, norm_g, wts, router_pad, n_exp):
    b, t, d, tm, const, tile, in_specs, args = _out_common(h, yc, ym, mod, mod_row, norm_g, wts)
    in_specs.append(const(router_pad.shape))
    args.append(router_pad)
    per = d // LANES
    top2 = pl.BlockSpec((1, TOP_K, tm), lambda i, j: (i, 0, j))
    return pl.pallas_call(
        functools.partial(_out_moe_kernel, n_exp=n_exp),
        grid=(b, t // tm),
        in_specs=in_specs,
        out_specs=[tile(d), pl.BlockSpec((tm * per, LANES), lambda i, j: (i * (t // tm) + j, 0)),
                   top2, top2],
        out_shape=[jax.ShapeDtypeStruct((b, t, d), F32), jax.ShapeDtypeStruct((b * t * per, LANES), F32),
                   jax.ShapeDtypeStruct((b, TOP_K, t), jnp.int32), jax.ShapeDtypeStruct((b, TOP_K, t), F32)],
        compiler_params=_params("parallel", "parallel"),
        name="out_router",
    )(*args)


ISSUE_UNROLL = 8


def _expert_kernel(be_ref, nu_ref, tok_ref, toknext_ref, u_hbm, w1_ref, w3_ref, w2_ref, y_ref, xbuf, sem,
                   *, chunks, per):
    i = pl.program_id(0)
    n_used = nu_ref[0]
    blk = xbuf.shape[1] // per
    slot = i % 2

    def row_copy(idx_ref, r, to_slot):
        src = pl.multiple_of(idx_ref[0, 0, r], per)
        return pltpu.make_async_copy(u_hbm.at[pl.ds(src, per), :],
                                     xbuf.at[to_slot, pl.ds(r * per, per), :], sem.at[to_slot])

    @pl.when(i == 0)
    def _():
        def body(r, c):
            row_copy(tok_ref, r, 0).start()
            return c
        lax.fori_loop(0, blk, body, 0, unroll=ISSUE_UNROLL)

    @pl.when(i <= n_used)
    def _():
        pltpu.make_async_copy(u_hbm.at[pl.ds(0, blk * per), :], xbuf.at[slot], sem.at[slot]).wait()

    @pl.when(i < n_used)
    def _():
        for r in range(blk):
            row_copy(toknext_ref, r, 1 - slot).start()
        x = _tiles_to_rows(xbuf.at[slot], blk, per).astype(BF16)
        acc = None
        for c0, c1 in chunks:
            a = _dot(x, w1_ref[0, :, c0:c1])
            hid = (a * jax.nn.sigmoid(a) * _dot(x, w3_ref[0, :, c0:c1])).astype(BF16)
            part = _dot(hid, w2_ref[0, c0:c1, :])
            acc = part if acc is None else acc + part
        _rows_to_tiles(y_ref, acc)

    @pl.when(i >= n_used)
    def _():
        y_ref[...] = jnp.zeros(y_ref.shape, F32)


def _experts(u_tiles, tok_buf, blk_e, n_used, w1, w3, w2):
    d, f = w1.shape[1], w1.shape[2]
    per = d // LANES
    nblk = tok_buf.shape[0] // MOE_BLOCK
    tok3 = (tok_buf * per).reshape(nblk, 1, MOE_BLOCK)
    smem = lambda fn: pl.BlockSpec((1, 1, MOE_BLOCK), fn, memory_space=pltpu.SMEM)
    grid_spec = pltpu.PrefetchScalarGridSpec(
        num_scalar_prefetch=2,
        grid=(nblk,),
        in_specs=[
            smem(lambda i, be, nu: (i, 0, 0)),
            smem(lambda i, be, nu: (jnp.minimum(i + 1, nblk - 1), 0, 0)),
            pl.BlockSpec(memory_space=pl.ANY),
            pl.BlockSpec((1, d, f), lambda i, be, nu: (be[i], 0, 0)),
            pl.BlockSpec((1, d, f), lambda i, be, nu: (be[i], 0, 0)),
            pl.BlockSpec((1, f, d), lambda i, be, nu: (be[i], 0, 0)),
        ],
        out_specs=pl.BlockSpec((MOE_BLOCK * per, LANES), lambda i, be, nu: (i, 0)),
        scratch_shapes=[pltpu.VMEM((2, MOE_BLOCK * per, LANES), F32), pltpu.SemaphoreType.DMA((2,))],
    )
    return pl.pallas_call(
        functools.partial(_expert_kernel, chunks=_ff_chunks(f), per=per),
        grid_spec=grid_spec,
        out_shape=jax.ShapeDtypeStruct((nblk * MOE_BLOCK * per, LANES), F32),
        compiler_params=_params("arbitrary"),
        name="moe_experts",
    )(blk_e, n_used, tok3, tok3, u_tiles, w1, w3, w2)


def _combine_kernel(dst_ref, dnext_ref, h_ref, gate_ref, mod_ref, *rest, final):
    if final:
        gf_ref, y_hbm, o_ref, ybuf, sem = rest
    else:
        y_hbm, o_ref, ybuf, sem = rest
    i = pl.program_id(0)
    n = pl.num_programs(0)
    tc, d = h_ref.shape
    per = d // LANES

    def issue(idx_ref, slot):
        def body(r, c):
            for k in range(TOP_K):
                src = pl.multiple_of(idx_ref[0, 0, TOP_K * r + k], per)
                pltpu.make_async_copy(y_hbm.at[pl.ds(src, per), :],
                                      ybuf.at[slot, k, pl.ds(pl.multiple_of(r * per, per), per), :],
                                      sem.at[slot]).start()
            return c
        lax.fori_loop(0, tc, body, 0, unroll=ISSUE_UNROLL)

    @pl.when(i == 0)
    def _():
        issue(dst_ref, 0)

    slot = i % 2

    @pl.when(i + 1 < n)
    def _():
        issue(dnext_ref, 1 - slot)

    for k in range(TOP_K):
        pltpu.make_async_copy(y_hbm.at[pl.ds(0, tc * per), :], ybuf.at[slot, k], sem.at[slot]).wait()

    gate = gate_ref[...]
    y = (gate[:, 0:1] * _tiles_to_rows(ybuf.at[slot, 0], tc, per)
         + gate[:, 1:2] * _tiles_to_rows(ybuf.at[slot, 1], tc, per))
    h = h_ref[...] + mod_ref[0, 5:6, :] * y
    if final:
        h = h * lax.rsqrt(jnp.mean(h * h, axis=-1, keepdims=True) + EPS) * gf_ref[...]
    o_ref[...] = h


def _combine(h_flat, y, dest, gates, mod, mod_row, tokens_per_batch, final_g):
    n_tok, d = h_flat.shape
    tc = COMBINE_TILE
    steps = n_tok // tc
    per_b = tokens_per_batch // tc
    per = d // LANES
    dst2 = (dest * per).reshape(steps, 1, TOP_K * tc)
    smem = lambda fn: pl.BlockSpec((1, 1, TOP_K * tc), fn, memory_space=pltpu.SMEM)
    in_specs = [
        smem(lambda i: (i, 0, 0)),
        smem(lambda i: (jnp.minimum(i + 1, steps - 1), 0, 0)),
        pl.BlockSpec((tc, d), lambda i: (i, 0)),
        pl.BlockSpec((tc, TOP_K), lambda i: (i, 0)),
        pl.BlockSpec((1, 6, d), (lambda i: (i // per_b, 0, 0)) if mod_row is None
                     else (lambda i: (mod_row, 0, 0))),
    ]
    args = [dst2, dst2, h_flat, gates, mod]
    if final_g is not None:
        in_specs.append(pl.BlockSpec((1, d), lambda i: (0, 0)))
        args.append(final_g)
    in_specs.append(pl.BlockSpec(memory_space=pl.ANY))
    args.append(y)
    return pl.pallas_call(
        functools.partial(_combine_kernel, final=final_g is not None),
        grid=(steps,),
        in_specs=in_specs,
        out_specs=pl.BlockSpec((tc, d), lambda i: (i, 0)),
        out_shape=jax.ShapeDtypeStruct((n_tok, d), F32),
        scratch_shapes=[pltpu.VMEM((2, TOP_K, tc * per, LANES), F32), pltpu.SemaphoreType.DMA((2,))],
        compiler_params=_params("arbitrary"),
        name="moe_combine",
    )(*args)


def _routing(top_i, n_exp):
    e = top_i.reshape(-1)
    a = e.shape[0]
    experts = jnp.arange(n_exp, dtype=jnp.int32)
    onehot = (e[:, None] == experts[None, :]).astype(jnp.int32)
    csum = jnp.cumsum(onehot, axis=0)
    rank = jnp.sum(onehot * csum, axis=1) - 1
    counts = csum[-1]
    padded = (counts + MOE_BLOCK - 1) // MOE_BLOCK * MOE_BLOCK
    pend = jnp.cumsum(padded)
    pstart = pend - padded
    start = jnp.cumsum(counts) - counts
    dest = pstart[e] + rank
    cap = -(-a // MOE_BLOCK) * MOE_BLOCK + n_exp * MOE_BLOCK
    nblk = cap // MOE_BLOCK
    blk_first = jnp.arange(nblk, dtype=jnp.int32) * MOE_BLOCK
    blk_e = jnp.minimum(jnp.sum((pend[None, :] <= blk_first[:, None]).astype(jnp.int32), axis=1), n_exp - 1)
    n_used = (pend[-1] // MOE_BLOCK).astype(jnp.int32).reshape(1)
    order = jnp.sort(e * a + jnp.arange(a, dtype=jnp.int32)) % a
    row = jnp.arange(cap, dtype=jnp.int32)
    row_e = jnp.repeat(blk_e, MOE_BLOCK)
    within = row - pstart[row_e]
    valid = (within < counts[row_e]) & (row < pend[-1])
    tok_buf = jnp.where(valid, order[jnp.clip(start[row_e] + within, 0, a - 1)] // TOP_K, 0)
    return dest.astype(jnp.int32), tok_buf.astype(jnp.int32), blk_e.astype(jnp.int32), n_used


def _moe_mixer(h_mid, u, top_i, top_g, mod, mod_row, moe_w, n_exp, final_g):
    b, t, d = h_mid.shape
    n_tok = b * t
    dest, tok_buf, blk_e, n_used = _routing(top_i, n_exp)
    y = _experts(u, tok_buf, blk_e, n_used, *moe_w)
    out = _combine(h_mid.reshape(n_tok, d), y, dest, top_g.reshape(n_tok, TOP_K), mod, mod_row, t, final_g)
    return out.reshape(b, t, d)


def _layer_weights(w_in_l, conv_w_l, b_gates_l, w_out_l):
    cd = conv_w_l.shape[1]
    n_gate = b_gates_l.size
    md = (w_in_l.shape[1] - 3 * cd - n_gate) // 4
    q0 = 3 * cd
    g0 = q0 + 3 * md
    return {
        "wc": w_in_l[:, :q0].astype(BF16),
        "wq": w_in_l[:, q0:q0 + md].T.astype(BF16),
        "wk": w_in_l[:, q0 + md:q0 + 2 * md].astype(BF16),
        "wv": w_in_l[:, q0 + 2 * md:g0].T.astype(BF16),
        "wg": w_in_l[:, g0:g0 + n_gate].T.astype(BF16),
        "bg": b_gates_l.reshape(n_gate, 1),
        "wo": w_in_l[:, g0 + n_gate:].T.astype(BF16),
        "cw": conv_w_l,
        "woc": w_out_l[:cd].astype(BF16),
        "wom": w_out_l[cd:].astype(BF16),
    }


def kernel(x, c, ctx, c_ctx, norm1_g, norm2_g, w_ada, b_ada, w_in, conv_w, b_gates, mlstm_norm_g, w_out,
           ffn_w1, ffn_w3, ffn_w2, moe_router, moe_w1, moe_w3, moe_w2, final_norm_g):
    b, t, d = x.shape
    tc = ctx.shape[1]
    depth = w_in.shape[0]
    n_exp = moe_router.shape[-1]
    assert t % GRID_W == 0 and t % CHUNK == 0 and tc % CHUNK == 0 and d % (SUBLANES * LANES) == 0

    ctx_row = b
    n_rows = -(-(b + 1) // SUBLANES) * SUBLANES
    c_all = jnp.concatenate([c, c_ctx[None, :], jnp.zeros((n_rows - b - 1, d), F32)], axis=0)
    mod_all = _ada(c_all, w_ada, b_ada).reshape(depth, n_rows, 6, d)
    final_g = final_norm_g.reshape(1, d)

    h, hc = x, ctx
    for layer in range(depth):
        last = layer == depth - 1
        j = layer // 2
        mod = mod_all[layer]
        wts = _layer_weights(w_in[layer], conv_w[layer], b_gates[layer], w_out[layer])
        n1 = norm1_g[layer].reshape(1, d)
        n2 = norm2_g[layer].reshape(1, d)

        ql, kl, vl, gl, ycl, ogl = _in_proj(h, mod, None, n1, wts, GRID_W, True)
        res_c = _in_proj(hc, mod, ctx_row, n1, wts, tc, not last)
        qc, kc, vc, gc = res_c[:4]
        ogc = None if last else res_c[5]
        ymc, yml = _mlstm(gc, gl, (kc, qc, vc), (kl, ql, vl), ogc, ogl,
                          mlstm_norm_g[layer].reshape(1, -1), not last)

        def mixer(hh, yc_, ym_, mod_row, fin):
            if layer % 2 == 0:
                ffn = (ffn_w1[j].astype(BF16), ffn_w3[j].astype(BF16), ffn_w2[j].astype(BF16))
                return _out_dense(hh, yc_, ym_, mod, mod_row, n2, wts, ffn, fin)
            r_full = jnp.pad(moe_router[j], ((0, 0), (0, LANES - n_exp)))
            r_hi = r_full.astype(BF16)
            r_lo = (r_full - r_hi.astype(F32)).astype(BF16)
            router_pad = jnp.concatenate([r_hi, r_lo], axis=1)
            h_mid, u2, top_i, top_g = _out_moe(hh, yc_, ym_, mod, mod_row, n2, wts, router_pad, n_exp)
            top_i, top_g = top_i.transpose(0, 2, 1), top_g.transpose(0, 2, 1)
            moe_w = (moe_w1[j].astype(BF16), moe_w3[j].astype(BF16), moe_w2[j].astype(BF16))
            return _moe_mixer(h_mid, u2, top_i, top_g, mod, mod_row, moe_w, n_exp, fin)

        h = mixer(h, ycl, yml, None, final_g if last else None)
        if not last:
            hc = mixer(hc, res_c[4], ymc, ctx_row, None)
    return h
```

```python
import functools

import jax
import jax.numpy as jnp
from jax import lax
from jax.experimental import pallas as pl
from jax.experimental.pallas import tpu as pltpu

GRID_W = 64
CHUNK = 128
EPS = 1e-6
TOP_K = 2
ROW_TILE = 512
MOE_BLOCK = 512
COMBINE_TILE = 512
SUB_TILE = 256
LANES = 128
SUBLANES = 8
AUG_ROWS = 32
VMEM_LIMIT = 56 * 1024 * 1024

F32 = jnp.float32
BF16 = jnp.bfloat16
HIGHEST = lax.Precision.HIGHEST


def _dot(a, b):
    return jnp.dot(a, b, preferred_element_type=F32)


def _dot_nt(a, b):
    return lax.dot_general(a, b, (((1,), (1,)), ((), ())), preferred_element_type=F32)


def _dot_tn(a, b):
    return lax.dot_general(a, b, (((0,), (0,)), ((), ())), preferred_element_type=F32)


def _rows_to_tiles(ref, x, row0=0):
    n, d = x.shape
    per = d // LANES
    for s in range(per):
        ref[pl.ds(row0 * per + s, n, stride=per), :] = x[:, s * LANES:(s + 1) * LANES]


def _tiles_to_rows(ref, n, per):
    return jnp.concatenate([ref[pl.ds(s, n, stride=per), :] for s in range(per)], axis=1)


def _sub_tiles(tm):
    sub = min(SUB_TILE, tm)
    return [slice(r0, r0 + sub) for r0 in range(0, tm, sub)]


def _params(*sem):
    return pltpu.CompilerParams(dimension_semantics=sem, vmem_limit_bytes=VMEM_LIMIT)


def _ff_chunks(d_ff, max_chunk=1024):
    out, c0 = [], 0
    while c0 < d_ff:
        c1 = min(c0 + max_chunk, d_ff)
        out.append((c0, c1))
        c0 = c1
    return out


def _ada_kernel(c_ref, w_ref, b_ref, o_ref):
    s = c_ref[...]
    s = s * jax.nn.sigmoid(s)
    o_ref[0] = jnp.dot(s, w_ref[0], precision=HIGHEST, preferred_element_type=F32) + b_ref[0]


def _ada(c_all, w_ada, b_ada):
    depth, d, d6 = w_ada.shape
    rows = c_all.shape[0]
    tn = d6 // 4
    return pl.pallas_call(
        _ada_kernel,
        grid=(depth, d6 // tn),
        in_specs=[
            pl.BlockSpec((rows, d), lambda l, j: (0, 0)),
            pl.BlockSpec((1, d, tn), lambda l, j: (l, 0, j)),
            pl.BlockSpec((1, 1, tn), lambda l, j: (l, 0, j)),
        ],
        out_specs=pl.BlockSpec((1, rows, tn), lambda l, j: (l, 0, j)),
        out_shape=jax.ShapeDtypeStruct((depth, rows, d6), F32),
        compiler_params=_params("parallel", "parallel"),
        name="adaln_mod",
    )(c_all, w_ada, b_ada.reshape(depth, 1, d6))


def _norm_mod(x, g, shift, scale):
    y = x * lax.rsqrt(jnp.mean(x * x, axis=-1, keepdims=True) + EPS)
    return (y * g) * (1.0 + scale) + shift


def _in_kernel(x_ref, mod_ref, g_ref, wf_ref, wk_ref, bg_ref, *rest, row_w, full, kscale):
    if full:
        wc_ref, cw_ref, q_ref, k_ref, v_ref, gt_ref, yc_ref, og_ref = rest
    else:
        q_ref, k_ref, v_ref, gt_ref = rest
    md = q_ref.shape[1]
    for rows in _sub_tiles(x_ref.shape[1]):
        u = _norm_mod(x_ref[0, rows, :], g_ref[...], mod_ref[0, 0:1, :], mod_ref[0, 1:2, :]).astype(BF16)
        ft = _dot_nt(wf_ref[...], u)
        q_ref[0, :, rows] = ft[:md].astype(BF16)
        k_ref[0, rows, :] = (_dot(u, wk_ref[...]) * kscale).astype(BF16)
        v_ref[0, :, rows] = ft[md:2 * md].astype(BF16)
        gt_ref[0, :, rows] = ft[-bg_ref.shape[0]:] + bg_ref[...]
        if full:
            cd = cw_ref.shape[1]
            c3 = _dot(u, wc_ref[...])
            z = c3[:, cd:2 * cd] * c3[:, 2 * cd:]
            n = z.shape[0]
            t = lax.broadcasted_iota(jnp.int32, (n, 1), 0) % row_w
            zprev = jnp.where(t == 0, 0.0, pltpu.roll(z, 1, axis=0))
            znext = jnp.where(t == row_w - 1, 0.0, pltpu.roll(z, n - 1, axis=0))
            conv = zprev * cw_ref[0:1, :] + z * cw_ref[1:2, :] + znext * cw_ref[2:3, :]
            yc_ref[0, rows, :] = (c3[:, :cd] * conv).astype(BF16)
            og_ref[0, :, rows] = jax.nn.sigmoid(ft[2 * md:3 * md]).astype(BF16)


def _in_proj(h, mod, mod_row, norm_g, wts, row_w, full):
    b, t, d = h.shape
    tm = min(ROW_TILE, t)
    md = wts["wk"].shape[1]
    n_gate = wts["bg"].shape[0]
    head_dim = md // (n_gate // 4)
    wf = wts["wf_full"] if full else wts["wf_state"]
    const = lambda shape: pl.BlockSpec(shape, lambda i, j: (0,) * len(shape))
    tile = lambda w: pl.BlockSpec((1, tm, w), lambda i, j: (i, j, 0))
    tile_t = pl.BlockSpec((1, md, tm), lambda i, j: (i, 0, j))
    feat_major = jax.ShapeDtypeStruct((b, md, t), BF16)
    in_specs = [
        tile(d),
        pl.BlockSpec((1, 6, d), (lambda i, j: (i, 0, 0)) if mod_row is None
                     else (lambda i, j: (mod_row, 0, 0))),
        const((1, d)), const(wf.shape), const(wts["wk"].shape), const(wts["bg"].shape),
    ]
    args = [h, mod, norm_g, wf, wts["wk"], wts["bg"]]
    out_specs = [tile_t, tile(md), tile_t, pl.BlockSpec((1, n_gate, tm), lambda i, j: (i, 0, j))]
    out_shape = [feat_major, jax.ShapeDtypeStruct((b, t, md), BF16), feat_major,
                 jax.ShapeDtypeStruct((b, n_gate, t), F32)]
    if full:
        cd = wts["cw"].shape[1]
        in_specs += [const(wts["wc"].shape), const(wts["cw"].shape)]
        args += [wts["wc"], wts["cw"]]
        out_specs += [tile(cd), tile_t]
        out_shape += [jax.ShapeDtypeStruct((b, t, cd), BF16), feat_major]
    return pl.pallas_call(
        functools.partial(_in_kernel, row_w=row_w, full=full, kscale=head_dim ** -0.5),
        grid=(b, t // tm),
        in_specs=in_specs,
        out_specs=out_specs,
        out_shape=out_shape,
        compiler_params=_params("parallel", "parallel"),
        name="in_proj",
    )(*args)


def _lane_scan(x, op, ident, reverse):
    n = x.shape[1]
    lane = lax.broadcasted_iota(jnp.int32, x.shape, 1)
    s = 1
    while s < n:
        if reverse:
            shifted = jnp.where(lane < n - s, pltpu.roll(x, n - s, axis=1), ident)
        else:
            shifted = jnp.where(lane >= s, pltpu.roll(x, s, axis=1), ident)
        x = op(x, shifted)
        s *= 2
    return x


def _mlstm_kernel(gr_ref, kc_ref, qc_ref, vc_ref, kl_ref, ql_ref, vl_ref, ogl_ref, gain_ref,
                  *rest, need_ctx):
    if need_ctx:
        ogc_ref, yc_ref, yl_ref, rt_ref, ra_ref, rp_ref, rw_ref, s_ref, upd_ref, st_ref = rest
    else:
        yl_ref, rt_ref, ra_ref, rp_ref, rw_ref, s_ref, upd_ref, st_ref = rest
        ogc_ref = yc_ref = None
    L = CHUNK
    hd = kl_ref.shape[2]
    ncc = kc_ref.shape[1] // L
    ncl = kl_ref.shape[1] // L
    nct = ncc + ncl

    head = pl.program_id(1)
    n_heads = gr_ref.shape[1]

    @pl.when(head == 0)
    def _():
        for d in range(2):
            rev = d == 1
            last = 0 if rev else L - 1
            for hh in range(n_heads):
                ig = gr_ref[0, hh, 2 * d]
                lf = jax.nn.log_sigmoid(gr_ref[0, hh, 2 * d + 1])
                a = _lane_scan(lf, jnp.add, 0.0, rev)
                r = ig - a
                p = _lane_scan(r, jnp.maximum, -jnp.inf, rev)
                ra_ref[d, hh, 0:nct, :] = a
                rp_ref[d, hh, 0:nct, :] = p
                rw_ref[d, hh, 0:nct, :] = jnp.exp(r - p[:, last:last + 1])
                r_pad = jnp.concatenate([r, jnp.zeros((L - nct, L), F32)], axis=0)
                rt_ref[d, hh] = r_pad.T

    spos = lax.broadcasted_iota(jnp.int32, (L, L), 0)
    tpos = lax.broadcasted_iota(jnp.int32, (L, L), 1)
    ones_row = jnp.where(lax.broadcasted_iota(jnp.int32, (AUG_ROWS, L), 0) == 0, 1.0, 0.0).astype(BF16)
    gain = gain_ref[...]

    ctx_seq = (kc_ref, qc_ref, vc_ref, ogc_ref, yc_ref, ncc, 0, need_ctx)
    lat_seq = (kl_ref, ql_ref, vl_ref, ogl_ref, yl_ref, ncl, ncc, True)

    for k_ref, q_ref, v_ref, _, _, n, g0, want in (ctx_seq, lat_seq):
        for j in range(n):
            g = g0 + j
            k = k_ref[0, pl.ds(j * L, L), :]
            if want:
                s_ref[g] = _dot(k, q_ref[0, :, pl.ds(j * L, L)])
            vaug = jnp.concatenate([v_ref[0, :, pl.ds(j * L, L)], ones_row], axis=0).astype(F32)
            for d in range(2):
                vw = (vaug * rw_ref[d, head, g:g + 1, :]).astype(BF16)
                upd_ref[d, g] = _dot(vw, k)

    m_in = {}
    for d in range(2):
        last = 0 if d == 1 else L - 1
        caug, m = jnp.zeros((hd + AUG_ROWS, hd), F32), jnp.zeros((1, 1), F32)
        for _, _, _, _, _, n, g0, _ in (ctx_seq, lat_seq):
            for i in range(n):
                g = g0 + (n - 1 - i if d == 1 else i)
                st_ref[d, g] = caug.astype(BF16)
                m_in[d, g] = m
                b_tot = ra_ref[d, head, g:g + 1, last:last + 1]
                p_last = rp_ref[d, head, g:g + 1, last:last + 1]
                m_last = jnp.maximum(p_last, m)
                caug = jnp.exp(m - m_last) * caug + jnp.exp(p_last - m_last) * upd_ref[d, g]
                m = b_tot + m_last

    for k_ref, q_ref, v_ref, og_ref, y_ref, n, g0, want in (ctx_seq, lat_seq):
        for j in range(n if want else 0):
            g = g0 + j
            lanes = pl.ds(j * L, L)
            q = q_ref[0, :, lanes]
            vaug = jnp.concatenate([v_ref[0, :, lanes], ones_row], axis=0)
            rhs, scale = [], []
            for d in range(2):
                a_row = ra_ref[d, head, g:g + 1, :]
                p_row = rp_ref[d, head, g:g + 1, :]
                m = m_in[d, g]
                mask = (spos >= tpos) if d == 1 else (spos <= tpos)
                w = jnp.where(mask, jnp.exp(rt_ref[d, head, :, g:g + 1] - p_row), 0.0)
                m_row = jnp.maximum(p_row, m)
                sw = (s_ref[g] * w * jnp.exp(p_row - m_row)).astype(BF16)
                qs = (q.astype(F32) * jnp.exp(m - m_row)).astype(BF16)
                rhs.append((sw, qs))
                scale.append(jnp.exp(-(a_row + m_row)))
            zero = jnp.zeros((hd, L), BF16)
            nd = _dot(jnp.concatenate([vaug, st_ref[0, g], st_ref[1, g]], axis=1),
                      jnp.concatenate([jnp.concatenate([rhs[0][0], rhs[1][0]], axis=1),
                                       jnp.concatenate([rhs[0][1], zero], axis=1),
                                       jnp.concatenate([zero, rhs[1][1]], axis=1)], axis=0))
            hh = None
            for d in range(2):
                part = nd[:, d * L:(d + 1) * L]
                den = jnp.maximum(jnp.abs(part[hd:hd + 1, :]), scale[d])
                hout = part[:hd, :] * (1.0 / den)
                hh = hout if hh is None else hh + hout
            hn = hh * lax.rsqrt(jnp.mean(hh * hh, axis=0, keepdims=True) + EPS)
            y_ref[0, :, lanes] = (hn * gain * og_ref[0, :, lanes].astype(F32)).astype(BF16)


def _mlstm(gates_c, gates_l, kqv_c, kqv_l, og_c, og_l, gain, need_ctx):
    b, tc, md = kqv_c[0].shape
    tl = kqv_l[0].shape[1]
    n_gate = gates_c.shape[1]
    heads = n_gate // 4
    hd = md // heads
    L = CHUNK
    nct = (tc + tl) // L
    assert nct <= L
    nrow = -(-nct // SUBLANES) * SUBLANES
    g = jnp.concatenate([gates_c, gates_l], axis=2).reshape(b, 4, heads, nct, L)
    g = g.transpose(0, 2, 1, 3, 4)
    tok = lambda t: pl.BlockSpec((1, t, hd), lambda i, h: (i, 0, h))
    feat = lambda t: pl.BlockSpec((1, hd, t), lambda i, h: (i, h, 0))
    in_specs = [pl.BlockSpec((1, heads, 4, nct, L), lambda i, h: (i, 0, 0, 0, 0)),
                tok(tc), feat(tc), feat(tc), tok(tl), feat(tl), feat(tl), feat(tl),
                pl.BlockSpec((hd, 1), lambda i, h: (h, 0))]
    args = [g, *kqv_c, *kqv_l, og_l, gain.reshape(md, 1)]
    out_specs = [feat(tl)]
    out_shape = [jax.ShapeDtypeStruct((b, md, tl), BF16)]
    if need_ctx:
        in_specs.append(feat(tc))
        args.append(og_c)
        out_specs = [feat(tc)] + out_specs
        out_shape = [jax.ShapeDtypeStruct((b, md, tc), BF16)] + out_shape
    res = pl.pallas_call(
        functools.partial(_mlstm_kernel, need_ctx=need_ctx),
        grid=(b, heads),
        in_specs=in_specs,
        out_specs=out_specs,
        out_shape=out_shape,
        scratch_shapes=[
            pltpu.VMEM((2, heads, L, L), F32),
            pltpu.VMEM((2, heads, nrow, L), F32),
            pltpu.VMEM((2, heads, nrow, L), F32),
            pltpu.VMEM((2, heads, nrow, L), F32),
            pltpu.VMEM((nct, L, L), F32),
            pltpu.VMEM((2, nct, hd + AUG_ROWS, hd), F32),
            pltpu.VMEM((2, nct, hd + AUG_ROWS, hd), BF16),
        ],
        compiler_params=_params("parallel", "arbitrary"),
        name="mlstm_scan",
    )(*args)
    return (res[0], res[1]) if need_ctx else (None, res[0])


def _mix_residual(h_ref, yc_ref, ym_ref, mod_ref, woc_ref, wom_ref, rows):
    mix = _dot(yc_ref[0, rows, :], woc_ref[...]) + _dot_tn(ym_ref[0, :, rows], wom_ref[...])
    return h_ref[0, rows, :] + mod_ref[0, 2:3, :] * mix


def _out_dense_kernel(h_ref, yc_ref, ym_ref, mod_ref, g2_ref, woc_ref, wom_ref, w1_ref, w3_ref, w2_ref,
                      *rest, chunks, final):
    if final:
        gf_ref, o_ref = rest
    else:
        (o_ref,) = rest
    h = _mix_residual(h_ref, yc_ref, ym_ref, mod_ref, woc_ref, wom_ref, slice(None))
    u = _norm_mod(h, g2_ref[...], mod_ref[0, 3:4, :], mod_ref[0, 4:5, :]).astype(BF16)
    acc = None
    for c0, c1 in chunks:
        a = _dot(u, w1_ref[:, c0:c1])
        hid = (a * jax.nn.sigmoid(a) * _dot(u, w3_ref[:, c0:c1])).astype(BF16)
        part = _dot(hid, w2_ref[c0:c1, :])
        acc = part if acc is None else acc + part
    h = h + mod_ref[0, 5:6, :] * acc
    if final:
        h = h * lax.rsqrt(jnp.mean(h * h, axis=-1, keepdims=True) + EPS) * gf_ref[...]
    o_ref[0] = h


def _out_moe_kernel(h_ref, yc_ref, ym_ref, mod_ref, g2_ref, woc_ref, wom_ref, router_ref,
                    hm_ref, u_ref, ti_ref, tg_ref, *, n_exp):
    for rows in _sub_tiles(h_ref.shape[1]):
        _out_moe_rows(h_ref, yc_ref, ym_ref, mod_ref, g2_ref, woc_ref, wom_ref, router_ref,
                      hm_ref, u_ref, ti_ref, tg_ref, rows, n_exp)


def _out_moe_rows(h_ref, yc_ref, ym_ref, mod_ref, g2_ref, woc_ref, wom_ref, router_ref,
                  hm_ref, u_ref, ti_ref, tg_ref, rows, n_exp):
    h = _mix_residual(h_ref, yc_ref, ym_ref, mod_ref, woc_ref, wom_ref, rows)
    hm_ref[0, rows, :] = h
    u = _norm_mod(h, g2_ref[...], mod_ref[0, 3:4, :], mod_ref[0, 4:5, :])
    _rows_to_tiles(u_ref, u, rows.start)
    u_hi = u.astype(BF16)
    u_lo = (u - u_hi.astype(F32)).astype(BF16)
    parts = _dot(u_hi, router_ref[...]) + _dot(u_lo, router_ref[...])
    logits = parts[:, :LANES] + parts[:, LANES:]
    n_pad = -(-n_exp // SUBLANES) * SUBLANES
    lt = logits.T[:n_pad, :]
    row = lax.broadcasted_iota(jnp.int32, lt.shape, 0)
    lt = jnp.where(row < n_exp, lt, -jnp.inf)
    v1 = jnp.max(lt, axis=0, keepdims=True)
    i1 = jnp.min(jnp.where(lt == v1, row, n_pad), axis=0, keepdims=True)
    rest = jnp.where(row == i1, -jnp.inf, lt)
    v2 = jnp.max(rest, axis=0, keepdims=True)
    i2 = jnp.min(jnp.where(rest == v2, row, n_pad), axis=0, keepdims=True)
    e2 = jnp.exp(v2 - v1)
    inv = 1.0 / (1.0 + e2)
    ti_ref[0, :, rows] = jnp.concatenate([i1, i2], axis=0)
    tg_ref[0, :, rows] = jnp.concatenate([inv, e2 * inv], axis=0)


def _out_common(h, yc, ym, mod, mod_row, norm_g, wts):
    b, t, d = h.shape
    tm = min(ROW_TILE, t)
    const = lambda shape: pl.BlockSpec(shape, lambda i, j: (0,) * len(shape))
    tile = lambda w: pl.BlockSpec((1, tm, w), lambda i, j: (i, j, 0))
    in_specs = [
        tile(d), tile(yc.shape[2]), pl.BlockSpec((1, ym.shape[1], tm), lambda i, j: (i, 0, j)),
        pl.BlockSpec((1, 6, d), (lambda i, j: (i, 0, 0)) if mod_row is None
                     else (lambda i, j: (mod_row, 0, 0))),
        const((1, d)), const(wts["woc"].shape), const(wts["wom"].shape),
    ]
    args = [h, yc, ym, mod, norm_g, wts["woc"], wts["wom"]]
    return b, t, d, tm, const, tile, in_specs, args


def _out_dense(h, yc, ym, mod, mod_row, norm_g, wts, ffn, final_g):
    b, t, d, tm, const, tile, in_specs, args = _out_common(h, yc, ym, mod, mod_row, norm_g, wts)
    w1, w3, w2 = ffn
    in_specs += [const(w1.shape), const(w3.shape), const(w2.shape)]
    args += [w1, w3, w2]
    if final_g is not None:
        in_specs.append(const((1, d)))
        args.append(final_g)
    return pl.pallas_call(
        functools.partial(_out_dense_kernel, chunks=_ff_chunks(w1.shape[1]), final=final_g is not None),
        grid=(b, t // tm),
        in_specs=in_specs,
        out_specs=tile(d),
        out_shape=jax.ShapeDtypeStruct((b, t, d), F32),
        compiler_params=_params("parallel", "parallel"),
        name="out_dense_ffn",
    )(*args)


def _out_moe(h, yc, ym, mod, mod_row, norm_g, wts, router_pad, n_exp):
    b, t, d, tm, const, tile, in_specs, args = _out_common(h, yc, ym, mod, mod_row, norm_g, wts)
    in_specs.append(const(router_pad.shape))
    args.append(router_pad)
    per = d // LANES
    top2 = pl.BlockSpec((1, TOP_K, tm), lambda i, j: (i, 0, j))
    return pl.pallas_call(
        functools.partial(_out_moe_kernel, n_exp=n_exp),
        grid=(b, t // tm),
        in_specs=in_specs,
        out_specs=[tile(d), pl.BlockSpec((tm * per, LANES), lambda i, j: (i * (t // tm) + j, 0)),
                   top2, top2],
        out_shape=[jax.ShapeDtypeStruct((b, t, d), F32), jax.ShapeDtypeStruct((b * t * per, LANES), F32),
                   jax.ShapeDtypeStruct((b, TOP_K, t), jnp.int32), jax.ShapeDtypeStruct((b, TOP_K, t), F32)],
        compiler_params=_params("parallel", "parallel"),
        name="out_router",
    )(*args)


ISSUE_UNROLL = 8


def _expert_kernel(be_ref, nu_ref, tok_ref, toknext_ref, u_hbm, w1_ref, w3_ref, w2_ref, y_ref, xbuf, sem,
                   *, chunks, per):
    i = pl.program_id(0)
    n_used = nu_ref[0]
    blk = xbuf.shape[1] // per
    slot = i % 2

    def row_copy(idx_ref, r, to_slot):
        src = pl.multiple_of(idx_ref[0, 0, r], per)
        return pltpu.make_async_copy(u_hbm.at[pl.ds(src, per), :],
                                     xbuf.at[to_slot, pl.ds(r * per, per), :], sem.at[to_slot])

    @pl.when(i == 0)
    def _():
        def body(r, c):
            row_copy(tok_ref, r, 0).start()
            return c
        lax.fori_loop(0, blk, body, 0, unroll=ISSUE_UNROLL)

    @pl.when(i <= n_used)
    def _():
        pltpu.make_async_copy(u_hbm.at[pl.ds(0, blk * per), :], xbuf.at[slot], sem.at[slot]).wait()

    @pl.when(i < n_used)
    def _():
        for r in range(blk):
            row_copy(toknext_ref, r, 1 - slot).start()
        x = _tiles_to_rows(xbuf.at[slot], blk, per).astype(BF16)
        acc = None
        for c0, c1 in chunks:
            a = _dot(x, w1_ref[0, :, c0:c1])
            hid = (a * jax.nn.sigmoid(a) * _dot(x, w3_ref[0, :, c0:c1])).astype(BF16)
            part = _dot(hid, w2_ref[0, c0:c1, :])
            acc = part if acc is None else acc + part
        _rows_to_tiles(y_ref, acc)

    @pl.when(i >= n_used)
    def _():
        y_ref[...] = jnp.zeros(y_ref.shape, F32)


def _experts(u_tiles, tok_buf, blk_e, n_used, w1, w3, w2):
    d, f = w1.shape[1], w1.shape[2]
    per = d // LANES
    nblk = tok_buf.shape[0] // MOE_BLOCK
    tok3 = (tok_buf * per).reshape(nblk, 1, MOE_BLOCK)
    smem = lambda fn: pl.BlockSpec((1, 1, MOE_BLOCK), fn, memory_space=pltpu.SMEM)
    grid_spec = pltpu.PrefetchScalarGridSpec(
        num_scalar_prefetch=2,
        grid=(nblk,),
        in_specs=[
            smem(lambda i, be, nu: (i, 0, 0)),
            smem(lambda i, be, nu: (jnp.minimum(i + 1, nblk - 1), 0, 0)),
            pl.BlockSpec(memory_space=pl.ANY),
            pl.BlockSpec((1, d, f), lambda i, be, nu: (be[i], 0, 0)),
            pl.BlockSpec((1, d, f), lambda i, be, nu: (be[i], 0, 0)),
            pl.BlockSpec((1, f, d), lambda i, be, nu: (be[i], 0, 0)),
        ],
        out_specs=pl.BlockSpec((MOE_BLOCK * per, LANES), lambda i, be, nu: (i, 0)),
        scratch_shapes=[pltpu.VMEM((2, MOE_BLOCK * per, LANES), F32), pltpu.SemaphoreType.DMA((2,))],
    )
    return pl.pallas_call(
        functools.partial(_expert_kernel, chunks=_ff_chunks(f), per=per),
        grid_spec=grid_spec,
        out_shape=jax.ShapeDtypeStruct((nblk * MOE_BLOCK * per, LANES), F32),
        compiler_params=_params("arbitrary"),
        name="moe_experts",
    )(blk_e, n_used, tok3, tok3, u_tiles, w1, w3, w2)


def _combine_kernel(dst_ref, dnext_ref, h_ref, gate_ref, mod_ref, *rest, final):
    if final:
        gf_ref, y_hbm, o_ref, ybuf, sem = rest
    else:
        y_hbm, o_ref, ybuf, sem = rest
    i = pl.program_id(0)
    n = pl.num_programs(0)
    tc, d = h_ref.shape
    per = d // LANES

    def issue(idx_ref, slot):
        def body(r, c):
            for k in range(TOP_K):
                src = pl.multiple_of(idx_ref[0, 0, TOP_K * r + k], per)
                pltpu.make_async_copy(y_hbm.at[pl.ds(src, per), :],
                                      ybuf.at[slot, k, pl.ds(pl.multiple_of(r * per, per), per), :],
                                      sem.at[slot]).start()
            return c
        lax.fori_loop(0, tc, body, 0, unroll=ISSUE_UNROLL)

    @pl.when(i == 0)
    def _():
        issue(dst_ref, 0)

    slot = i % 2

    @pl.when(i + 1 < n)
    def _():
        issue(dnext_ref, 1 - slot)

    for k in range(TOP_K):
        pltpu.make_async_copy(y_hbm.at[pl.ds(0, tc * per), :], ybuf.at[slot, k], sem.at[slot]).wait()

    gate = gate_ref[...]
    y = (gate[:, 0:1] * _tiles_to_rows(ybuf.at[slot, 0], tc, per)
         + gate[:, 1:2] * _tiles_to_rows(ybuf.at[slot, 1], tc, per))
    h = h_ref[...] + mod_ref[0, 5:6, :] * y
    if final:
        h = h * lax.rsqrt(jnp.mean(h * h, axis=-1, keepdims=True) + EPS) * gf_ref[...]
    o_ref[...] = h


def _combine(h_flat, y, dest, gates, mod, mod_row, tokens_per_batch, final_g):
    n_tok, d = h_flat.shape
    tc = COMBINE_TILE
    steps = n_tok // tc
    per_b = tokens_per_batch // tc
    per = d // LANES
    dst2 = (dest * per).reshape(steps, 1, TOP_K * tc)
    smem = lambda fn: pl.BlockSpec((1, 1, TOP_K * tc), fn, memory_space=pltpu.SMEM)
    in_specs = [
        smem(lambda i: (i, 0, 0)),
        smem(lambda i: (jnp.minimum(i + 1, steps - 1), 0, 0)),
        pl.BlockSpec((tc, d), lambda i: (i, 0)),
        pl.BlockSpec((tc, TOP_K), lambda i: (i, 0)),
        pl.BlockSpec((1, 6, d), (lambda i: (i // per_b, 0, 0)) if mod_row is None
                     else (lambda i: (mod_row, 0, 0))),
    ]
    args = [dst2, dst2, h_flat, gates, mod]
    if final_g is not None:
        in_specs.append(pl.BlockSpec((1, d), lambda i: (0, 0)))
        args.append(final_g)
    in_specs.append(pl.BlockSpec(memory_space=pl.ANY))
    args.append(y)
    return pl.pallas_call(
        functools.partial(_combine_kernel, final=final_g is not None),
        grid=(steps,),
        in_specs=in_specs,
        out_specs=pl.BlockSpec((tc, d), lambda i: (i, 0)),
        out_shape=jax.ShapeDtypeStruct((n_tok, d), F32),
        scratch_shapes=[pltpu.VMEM((2, TOP_K, tc * per, LANES), F32), pltpu.SemaphoreType.DMA((2,))],
        compiler_params=_params("arbitrary"),
        name="moe_combine",
    )(*args)


def _routing(top_i, n_exp):
    e = top_i.reshape(-1)
    a = e.shape[0]
    experts = jnp.arange(n_exp, dtype=jnp.int32)
    onehot = (e[:, None] == experts[None, :]).astype(jnp.int32)
    csum = jnp.cumsum(onehot, axis=0)
    rank = jnp.sum(onehot * csum, axis=1) - 1
    counts = csum[-1]
    padded = (counts + MOE_BLOCK - 1) // MOE_BLOCK * MOE_BLOCK
    pend = jnp.cumsum(padded)
    pstart = pend - padded
    start = jnp.cumsum(counts) - counts
    dest = pstart[e] + rank
    cap = -(-a // MOE_BLOCK) * MOE_BLOCK + n_exp * MOE_BLOCK
    nblk = cap // MOE_BLOCK
    blk_first = jnp.arange(nblk, dtype=jnp.int32) * MOE_BLOCK
    blk_e = jnp.minimum(jnp.sum((pend[None, :] <= blk_first[:, None]).astype(jnp.int32), axis=1), n_exp - 1)
    n_used = (pend[-1] // MOE_BLOCK).astype(jnp.int32).reshape(1)
    order = jnp.argsort(e).astype(jnp.int32)
    row = jnp.arange(cap, dtype=jnp.int32)
    row_e = jnp.repeat(blk_e, MOE_BLOCK)
    within = row - pstart[row_e]
    valid = (within < counts[row_e]) & (row < pend[-1])
    tok_buf = jnp.where(valid, order[jnp.clip(start[row_e] + within, 0, a - 1)] // TOP_K, 0)
    return dest.astype(jnp.int32), tok_buf.astype(jnp.int32), blk_e.astype(jnp.int32), n_used


def _moe_mixer(h_mid, u, top_i, top_g, mod, mod_row, moe_w, n_exp, final_g):
    b, t, d = h_mid.shape
    n_tok = b * t
    dest, tok_buf, blk_e, n_used = _routing(top_i, n_exp)
    y = _experts(u, tok_buf, blk_e, n_used, *moe_w)
    out = _combine(h_mid.reshape(n_tok, d), y, dest, top_g.reshape(n_tok, TOP_K), mod, mod_row, t, final_g)
    return out.reshape(b, t, d)


def _layer_weights(w_in_l, conv_w_l, b_gates_l, w_out_l):
    cd = conv_w_l.shape[1]
    n_gate = b_gates_l.size
    md = (w_in_l.shape[1] - 3 * cd - n_gate) // 4
    q0 = 3 * cd
    g0 = q0 + 3 * md
    wq, wv = w_in_l[:, q0:q0 + md], w_in_l[:, q0 + 2 * md:g0]
    wg, wo = w_in_l[:, g0:g0 + n_gate], w_in_l[:, g0 + n_gate:]
    return {
        "wc": w_in_l[:, :q0].astype(BF16),
        "wf_full": jnp.concatenate([wq, wv, wo, wg], axis=1).T.astype(BF16),
        "wf_state": jnp.concatenate([wq, wv, wg], axis=1).T.astype(BF16),
        "wk": w_in_l[:, q0 + md:q0 + 2 * md].astype(BF16),
        "bg": b_gates_l.reshape(n_gate, 1),
        "cw": conv_w_l,
        "woc": w_out_l[:cd].astype(BF16),
        "wom": w_out_l[cd:].astype(BF16),
    }


def kernel(x, c, ctx, c_ctx, norm1_g, norm2_g, w_ada, b_ada, w_in, conv_w, b_gates, mlstm_norm_g, w_out,
           ffn_w1, ffn_w3, ffn_w2, moe_router, moe_w1, moe_w3, moe_w2, final_norm_g):
    b, t, d = x.shape
    tc = ctx.shape[1]
    depth = w_in.shape[0]
    n_exp = moe_router.shape[-1]
    assert t % GRID_W == 0 and t % CHUNK == 0 and tc % CHUNK == 0 and d % (SUBLANES * LANES) == 0

    ctx_row = b
    n_rows = -(-(b + 1) // SUBLANES) * SUBLANES
    c_all = jnp.concatenate([c, c_ctx[None, :], jnp.zeros((n_rows - b - 1, d), F32)], axis=0)
    mod_all = _ada(c_all, w_ada, b_ada).reshape(depth, n_rows, 6, d)
    final_g = final_norm_g.reshape(1, d)

    h, hc = x, ctx
    for layer in range(depth):
        last = layer == depth - 1
        j = layer // 2
        mod = mod_all[layer]
        wts = _layer_weights(w_in[layer], conv_w[layer], b_gates[layer], w_out[layer])
        n1 = norm1_g[layer].reshape(1, d)
        n2 = norm2_g[layer].reshape(1, d)

        ql, kl, vl, gl, ycl, ogl = _in_proj(h, mod, None, n1, wts, GRID_W, True)
        res_c = _in_proj(hc, mod, ctx_row, n1, wts, tc, not last)
        qc, kc, vc, gc = res_c[:4]
        ogc = None if last else res_c[5]
        ymc, yml = _mlstm(gc, gl, (kc, qc, vc), (kl, ql, vl), ogc, ogl,
                          mlstm_norm_g[layer].reshape(1, -1), not last)

        def mixer(hh, yc_, ym_, mod_row, fin):
            if layer % 2 == 0:
                ffn = (ffn_w1[j].astype(BF16), ffn_w3[j].astype(BF16), ffn_w2[j].astype(BF16))
                return _out_dense(hh, yc_, ym_, mod, mod_row, n2, wts, ffn, fin)
            r_full = jnp.pad(moe_router[j], ((0, 0), (0, LANES - n_exp)))
            r_hi = r_full.astype(BF16)
            r_lo = (r_full - r_hi.astype(F32)).astype(BF16)
            router_pad = jnp.concatenate([r_hi, r_lo], axis=1)
            h_mid, u2, top_i, top_g = _out_moe(hh, yc_, ym_, mod, mod_row, n2, wts, router_pad, n_exp)
            top_i, top_g = top_i.transpose(0, 2, 1), top_g.transpose(0, 2, 1)
            moe_w = (moe_w1[j].astype(BF16), moe_w3[j].astype(BF16), moe_w2[j].astype(BF16))
            return _moe_mixer(h_mid, u2, top_i, top_g, mod, mod_row, moe_w, n_exp, fin)

        h = mixer(h, ycl, yml, None, final_g if last else None)
        if not last:
            hc = mixer(hc, res_c[4], ymc, ctx_row, None)
    return h
```

```python
import functools

import jax
import jax.numpy as jnp
from jax import lax
from jax.experimental import pallas as pl
from jax.experimental.pallas import tpu as pltpu

GRID_W = 64
CHUNK = 128
N_GATES = 4
EPS = 1e-6
TOP_K = 2
ROW_TILE = 1024
FFN_ROW_TILE = 512
MOE_BLOCK = 512
COMBINE_TILE = 512
SUB_TILE = 256
FF_CHUNK = 1024
ADA_COL_TILES = 4
ISSUE_UNROLL = 8
LANES = 128
SUBLANES = 8
AUG_ROWS = 32
VMEM_LIMIT = 56 * 1024 * 1024

F32 = jnp.float32
BF16 = jnp.bfloat16
HIGHEST = lax.Precision.HIGHEST


def _dot(a, b):
    return jnp.dot(a, b, preferred_element_type=F32)


def _dot_nt(a, b):
    return lax.dot_general(a, b, (((1,), (1,)), ((), ())), preferred_element_type=F32)


def _dot_tn(a, b):
    return lax.dot_general(a, b, (((0,), (0,)), ((), ())), preferred_element_type=F32)


def _rows_to_tiles(ref, x, row0=0):
    n, d = x.shape
    per = d // LANES
    for s in range(per):
        ref[pl.ds(row0 * per + s, n, stride=per), :] = x[:, s * LANES:(s + 1) * LANES]


def _tiles_to_rows(ref, n, per):
    return jnp.concatenate([ref[pl.ds(s, n, stride=per), :] for s in range(per)], axis=1)


def _sub_tiles(tm):
    sub = min(SUB_TILE, tm)
    return [slice(r0, r0 + sub) for r0 in range(0, tm, sub)]


def _params(*sem):
    return pltpu.CompilerParams(dimension_semantics=sem, vmem_limit_bytes=VMEM_LIMIT)


def _ff_chunks(d_ff):
    out, c0 = [], 0
    while c0 < d_ff:
        c1 = min(c0 + FF_CHUNK, d_ff)
        out.append((c0, c1))
        c0 = c1
    return out


def _ada_kernel(c_ref, w_ref, b_ref, o_ref):
    s = c_ref[...]
    s = s * jax.nn.sigmoid(s)
    o_ref[0] = jnp.dot(s, w_ref[0], precision=HIGHEST, preferred_element_type=F32) + b_ref[0]


def _ada(c_all, w_ada, b_ada):
    depth, d, d6 = w_ada.shape
    rows = c_all.shape[0]
    tn = d6 // ADA_COL_TILES
    return pl.pallas_call(
        _ada_kernel,
        grid=(depth, d6 // tn),
        in_specs=[
            pl.BlockSpec((rows, d), lambda l, j: (0, 0)),
            pl.BlockSpec((1, d, tn), lambda l, j: (l, 0, j)),
            pl.BlockSpec((1, 1, tn), lambda l, j: (l, 0, j)),
        ],
        out_specs=pl.BlockSpec((1, rows, tn), lambda l, j: (l, 0, j)),
        out_shape=jax.ShapeDtypeStruct((depth, rows, d6), F32),
        compiler_params=_params("parallel", "parallel"),
        name="adaln_mod",
    )(c_all, w_ada, b_ada.reshape(depth, 1, d6))


def _norm_mod(x, g, shift, scale):
    y = x * lax.rsqrt(jnp.mean(x * x, axis=-1, keepdims=True) + EPS)
    return (y * g) * (1.0 + scale) + shift


def _in_kernel(x_ref, mod_ref, g_ref, wf_ref, wk_ref, bg_ref, *rest, row_w, full, kscale):
    if full:
        wc_ref, cw_ref, q_ref, k_ref, v_ref, gt_ref, yc_ref, og_ref = rest
    else:
        q_ref, k_ref, v_ref, gt_ref = rest
    md = q_ref.shape[1]
    for rows in _sub_tiles(x_ref.shape[1]):
        u = _norm_mod(x_ref[0, rows, :], g_ref[...], mod_ref[0, 0:1, :], mod_ref[0, 1:2, :]).astype(BF16)
        ft = _dot_nt(wf_ref[...], u)
        q_ref[0, :, rows] = ft[:md].astype(BF16)
        k_ref[0, rows, :] = (_dot(u, wk_ref[...]) * kscale).astype(BF16)
        v_ref[0, :, rows] = ft[md:2 * md].astype(BF16)
        gt_ref[0, :, rows] = ft[-bg_ref.shape[0]:] + bg_ref[...]
        if full:
            cd = cw_ref.shape[1]
            c3 = _dot(u, wc_ref[...])
            z = c3[:, cd:2 * cd] * c3[:, 2 * cd:]
            n = z.shape[0]
            t = lax.broadcasted_iota(jnp.int32, (n, 1), 0) % row_w
            zprev = jnp.where(t == 0, 0.0, pltpu.roll(z, 1, axis=0))
            znext = jnp.where(t == row_w - 1, 0.0, pltpu.roll(z, n - 1, axis=0))
            conv = zprev * cw_ref[0:1, :] + z * cw_ref[1:2, :] + znext * cw_ref[2:3, :]
            yc_ref[0, rows, :] = (c3[:, :cd] * conv).astype(BF16)
            og_ref[0, :, rows] = jax.nn.sigmoid(ft[2 * md:3 * md]).astype(BF16)


def _in_proj(h, mod, mod_row, norm_g, wts, row_w, full):
    b, t, d = h.shape
    tm = min(ROW_TILE, t)
    md = wts["wk"].shape[1]
    n_gate = wts["bg"].shape[0]
    head_dim = md // (n_gate // N_GATES)
    wf = wts["wf_full"] if full else wts["wf_state"]
    const = lambda shape: pl.BlockSpec(shape, lambda i, j: (0,) * len(shape))
    tile = lambda w: pl.BlockSpec((1, tm, w), lambda i, j: (i, j, 0))
    tile_t = pl.BlockSpec((1, md, tm), lambda i, j: (i, 0, j))
    feat_major = jax.ShapeDtypeStruct((b, md, t), BF16)
    in_specs = [
        tile(d),
        pl.BlockSpec((1, 6, d), (lambda i, j: (i, 0, 0)) if mod_row is None
                     else (lambda i, j: (mod_row, 0, 0))),
        const((1, d)), const(wf.shape), const(wts["wk"].shape), const(wts["bg"].shape),
    ]
    args = [h, mod, norm_g, wf, wts["wk"], wts["bg"]]
    out_specs = [tile_t, tile(md), tile_t, pl.BlockSpec((1, n_gate, tm), lambda i, j: (i, 0, j))]
    out_shape = [feat_major, jax.ShapeDtypeStruct((b, t, md), BF16), feat_major,
                 jax.ShapeDtypeStruct((b, n_gate, t), F32)]
    if full:
        cd = wts["cw"].shape[1]
        in_specs += [const(wts["wc"].shape), const(wts["cw"].shape)]
        args += [wts["wc"], wts["cw"]]
        out_specs += [tile(cd), tile_t]
        out_shape += [jax.ShapeDtypeStruct((b, t, cd), BF16), feat_major]
    return pl.pallas_call(
        functools.partial(_in_kernel, row_w=row_w, full=full, kscale=head_dim ** -0.5),
        grid=(b, t // tm),
        in_specs=in_specs,
        out_specs=out_specs,
        out_shape=out_shape,
        compiler_params=_params("parallel", "parallel"),
        name="in_proj",
    )(*args)


def _lane_scan(x, op, ident, reverse):
    n = x.shape[1]
    lane = lax.broadcasted_iota(jnp.int32, x.shape, 1)
    s = 1
    while s < n:
        if reverse:
            shifted = jnp.where(lane < n - s, pltpu.roll(x, n - s, axis=1), ident)
        else:
            shifted = jnp.where(lane >= s, pltpu.roll(x, s, axis=1), ident)
        x = op(x, shifted)
        s *= 2
    return x


def _mlstm_kernel(gr_ref, kc_ref, qc_ref, vc_ref, kl_ref, ql_ref, vl_ref, ogl_ref, gain_ref,
                  *rest, need_ctx):
    if need_ctx:
        ogc_ref, yc_ref, yl_ref, rt_ref, ra_ref, rp_ref, rw_ref, s_ref, upd_ref, st_ref = rest
    else:
        yl_ref, rt_ref, ra_ref, rp_ref, rw_ref, s_ref, upd_ref, st_ref = rest
        ogc_ref = yc_ref = None
    L = CHUNK
    hd = kl_ref.shape[2]
    ncc = kc_ref.shape[1] // L
    ncl = kl_ref.shape[1] // L
    nct = ncc + ncl

    head = pl.program_id(1)
    n_heads = gr_ref.shape[1]

    @pl.when(head == 0)
    def _():
        for d in range(2):
            rev = d == 1
            last = 0 if rev else L - 1
            for hh in range(n_heads):
                ig = gr_ref[0, hh, 2 * d]
                lf = jax.nn.log_sigmoid(gr_ref[0, hh, 2 * d + 1])
                a = _lane_scan(lf, jnp.add, 0.0, rev)
                r = ig - a
                p = _lane_scan(r, jnp.maximum, -jnp.inf, rev)
                ra_ref[d, hh, 0:nct, :] = a
                rp_ref[d, hh, 0:nct, :] = p
                rw_ref[d, hh, 0:nct, :] = jnp.exp(r - p[:, last:last + 1])
                r_pad = jnp.concatenate([r, jnp.zeros((L - nct, L), F32)], axis=0)
                rt_ref[d, hh] = r_pad.T

    spos = lax.broadcasted_iota(jnp.int32, (L, L), 0)
    tpos = lax.broadcasted_iota(jnp.int32, (L, L), 1)
    ones_row = jnp.where(lax.broadcasted_iota(jnp.int32, (AUG_ROWS, L), 0) == 0, 1.0, 0.0).astype(BF16)
    gain = gain_ref[...]

    ctx_seq = (kc_ref, qc_ref, vc_ref, ogc_ref, yc_ref, ncc, 0, need_ctx)
    lat_seq = (kl_ref, ql_ref, vl_ref, ogl_ref, yl_ref, ncl, ncc, True)

    for k_ref, q_ref, v_ref, _, _, n, g0, want in (ctx_seq, lat_seq):
        for j in range(n):
            g = g0 + j
            k = k_ref[0, pl.ds(j * L, L), :]
            if want:
                s_ref[g] = _dot(k, q_ref[0, :, pl.ds(j * L, L)])
            vaug = jnp.concatenate([v_ref[0, :, pl.ds(j * L, L)], ones_row], axis=0).astype(F32)
            for d in range(2):
                vw = (vaug * rw_ref[d, head, g:g + 1, :]).astype(BF16)
                upd_ref[d, g] = _dot(vw, k)

    m_in = {}
    for d in range(2):
        last = 0 if d == 1 else L - 1
        caug, m = jnp.zeros((hd + AUG_ROWS, hd), F32), jnp.zeros((1, 1), F32)
        for _, _, _, _, _, n, g0, _ in (ctx_seq, lat_seq):
            for i in range(n):
                g = g0 + (n - 1 - i if d == 1 else i)
                st_ref[d, g] = caug.astype(BF16)
                m_in[d, g] = m
                b_tot = ra_ref[d, head, g:g + 1, last:last + 1]
                p_last = rp_ref[d, head, g:g + 1, last:last + 1]
                m_last = jnp.maximum(p_last, m)
                caug = jnp.exp(m - m_last) * caug + jnp.exp(p_last - m_last) * upd_ref[d, g]
                m = b_tot + m_last

    for k_ref, q_ref, v_ref, og_ref, y_ref, n, g0, want in (ctx_seq, lat_seq):
        for j in range(n if want else 0):
            g = g0 + j
            lanes = pl.ds(j * L, L)
            q = q_ref[0, :, lanes]
            vaug = jnp.concatenate([v_ref[0, :, lanes], ones_row], axis=0)
            rhs, scale = [], []
            for d in range(2):
                a_row = ra_ref[d, head, g:g + 1, :]
                p_row = rp_ref[d, head, g:g + 1, :]
                m = m_in[d, g]
                mask = (spos >= tpos) if d == 1 else (spos <= tpos)
                w = jnp.where(mask, jnp.exp(rt_ref[d, head, :, g:g + 1] - p_row), 0.0)
                m_row = jnp.maximum(p_row, m)
                sw = (s_ref[g] * w * jnp.exp(p_row - m_row)).astype(BF16)
                qs = (q.astype(F32) * jnp.exp(m - m_row)).astype(BF16)
                rhs.append((sw, qs))
                scale.append(jnp.exp(-(a_row + m_row)))
            zero = jnp.zeros((hd, L), BF16)
            nd = _dot(jnp.concatenate([vaug, st_ref[0, g], st_ref[1, g]], axis=1),
                      jnp.concatenate([jnp.concatenate([rhs[0][0], rhs[1][0]], axis=1),
                                       jnp.concatenate([rhs[0][1], zero], axis=1),
                                       jnp.concatenate([zero, rhs[1][1]], axis=1)], axis=0))
            hh = None
            for d in range(2):
                part = nd[:, d * L:(d + 1) * L]
                den = jnp.maximum(jnp.abs(part[hd:hd + 1, :]), scale[d])
                hout = part[:hd, :] * (1.0 / den)
                hh = hout if hh is None else hh + hout
            hn = hh * lax.rsqrt(jnp.mean(hh * hh, axis=0, keepdims=True) + EPS)
            y_ref[0, :, lanes] = (hn * gain * og_ref[0, :, lanes].astype(F32)).astype(BF16)


def _mlstm(gates_c, gates_l, kqv_c, kqv_l, og_c, og_l, gain, need_ctx):
    b, tc, md = kqv_c[0].shape
    tl = kqv_l[0].shape[1]
    n_gate = gates_c.shape[1]
    heads = n_gate // N_GATES
    hd = md // heads
    L = CHUNK
    nct = (tc + tl) // L
    assert nct <= L
    nrow = -(-nct // SUBLANES) * SUBLANES
    g = jnp.concatenate([gates_c, gates_l], axis=2).reshape(b, N_GATES, heads, nct, L)
    g = g.transpose(0, 2, 1, 3, 4)
    tok = lambda t: pl.BlockSpec((1, t, hd), lambda i, h: (i, 0, h))
    feat = lambda t: pl.BlockSpec((1, hd, t), lambda i, h: (i, h, 0))
    in_specs = [pl.BlockSpec((1, heads, N_GATES, nct, L), lambda i, h: (i, 0, 0, 0, 0)),
                tok(tc), feat(tc), feat(tc), tok(tl), feat(tl), feat(tl), feat(tl),
                pl.BlockSpec((hd, 1), lambda i, h: (h, 0))]
    args = [g, *kqv_c, *kqv_l, og_l, gain.reshape(md, 1)]
    out_specs = [feat(tl)]
    out_shape = [jax.ShapeDtypeStruct((b, md, tl), BF16)]
    if need_ctx:
        in_specs.append(feat(tc))
        args.append(og_c)
        out_specs = [feat(tc)] + out_specs
        out_shape = [jax.ShapeDtypeStruct((b, md, tc), BF16)] + out_shape
    res = pl.pallas_call(
        functools.partial(_mlstm_kernel, need_ctx=need_ctx),
        grid=(b, heads),
        in_specs=in_specs,
        out_specs=out_specs,
        out_shape=out_shape,
        scratch_shapes=[
            pltpu.VMEM((2, heads, L, L), F32),
            pltpu.VMEM((2, heads, nrow, L), F32),
            pltpu.VMEM((2, heads, nrow, L), F32),
            pltpu.VMEM((2, heads, nrow, L), F32),
            pltpu.VMEM((nct, L, L), F32),
            pltpu.VMEM((2, nct, hd + AUG_ROWS, hd), F32),
            pltpu.VMEM((2, nct, hd + AUG_ROWS, hd), BF16),
        ],
        compiler_params=_params("parallel", "arbitrary"),
        name="mlstm_scan",
    )(*args)
    return (res[0], res[1]) if need_ctx else (None, res[0])


def _mix_residual(h_ref, yc_ref, ym_ref, mod_ref, woc_ref, wom_ref, rows):
    mix = _dot(yc_ref[0, rows, :], woc_ref[...]) + _dot_tn(ym_ref[0, :, rows], wom_ref[...])
    return h_ref[0, rows, :] + mod_ref[0, 2:3, :] * mix


def _out_dense_kernel(h_ref, yc_ref, ym_ref, mod_ref, g2_ref, woc_ref, wom_ref, w1_ref, w3_ref, w2_ref,
                      *rest, chunks, final):
    if final:
        gf_ref, o_ref = rest
    else:
        (o_ref,) = rest
    h = _mix_residual(h_ref, yc_ref, ym_ref, mod_ref, woc_ref, wom_ref, slice(None))
    u = _norm_mod(h, g2_ref[...], mod_ref[0, 3:4, :], mod_ref[0, 4:5, :]).astype(BF16)
    acc = None
    for c0, c1 in chunks:
        a = _dot(u, w1_ref[:, c0:c1])
        hid = (a * jax.nn.sigmoid(a) * _dot(u, w3_ref[:, c0:c1])).astype(BF16)
        part = _dot(hid, w2_ref[c0:c1, :])
        acc = part if acc is None else acc + part
    h = h + mod_ref[0, 5:6, :] * acc
    if final:
        h = h * lax.rsqrt(jnp.mean(h * h, axis=-1, keepdims=True) + EPS) * gf_ref[...]
    o_ref[0] = h


def _out_moe_kernel(h_ref, yc_ref, ym_ref, mod_ref, g2_ref, woc_ref, wom_ref, router_ref,
                    hm_ref, u_ref, ti_ref, tg_ref, *, n_exp):
    for rows in _sub_tiles(h_ref.shape[1]):
        _out_moe_rows(h_ref, yc_ref, ym_ref, mod_ref, g2_ref, woc_ref, wom_ref, router_ref,
                      hm_ref, u_ref, ti_ref, tg_ref, rows, n_exp)


def _out_moe_rows(h_ref, yc_ref, ym_ref, mod_ref, g2_ref, woc_ref, wom_ref, router_ref,
                  hm_ref, u_ref, ti_ref, tg_ref, rows, n_exp):
    h = _mix_residual(h_ref, yc_ref, ym_ref, mod_ref, woc_ref, wom_ref, rows)
    hm_ref[0, rows, :] = h
    u = _norm_mod(h, g2_ref[...], mod_ref[0, 3:4, :], mod_ref[0, 4:5, :])
    _rows_to_tiles(u_ref, u, rows.start)
    u_hi = u.astype(BF16)
    u_lo = (u - u_hi.astype(F32)).astype(BF16)
    parts = _dot(u_hi, router_ref[...]) + _dot(u_lo, router_ref[...])
    logits = parts[:, :LANES] + parts[:, LANES:]
    n_pad = -(-n_exp // SUBLANES) * SUBLANES
    lt = logits.T[:n_pad, :]
    row = lax.broadcasted_iota(jnp.int32, lt.shape, 0)
    lt = jnp.where(row < n_exp, lt, -jnp.inf)
    v1 = jnp.max(lt, axis=0, keepdims=True)
    i1 = jnp.min(jnp.where(lt == v1, row, n_pad), axis=0, keepdims=True)
    rest = jnp.where(row == i1, -jnp.inf, lt)
    v2 = jnp.max(rest, axis=0, keepdims=True)
    i2 = jnp.min(jnp.where(rest == v2, row, n_pad), axis=0, keepdims=True)
    e2 = jnp.exp(v2 - v1)
    inv = 1.0 / (1.0 + e2)
    ti_ref[0, :, rows] = jnp.concatenate([i1, i2], axis=0)
    tg_ref[0, :, rows] = jnp.concatenate([inv, e2 * inv], axis=0)


def _out_common(h, yc, ym, mod, mod_row, norm_g, wts, row_tile):
    b, t, d = h.shape
    tm = min(row_tile, t)
    const = lambda shape: pl.BlockSpec(shape, lambda i, j: (0,) * len(shape))
    tile = lambda w: pl.BlockSpec((1, tm, w), lambda i, j: (i, j, 0))
    in_specs = [
        tile(d), tile(yc.shape[2]), pl.BlockSpec((1, ym.shape[1], tm), lambda i, j: (i, 0, j)),
        pl.BlockSpec((1, 6, d), (lambda i, j: (i, 0, 0)) if mod_row is None
                     else (lambda i, j: (mod_row, 0, 0))),
        const((1, d)), const(wts["woc"].shape), const(wts["wom"].shape),
    ]
    args = [h, yc, ym, mod, norm_g, wts["woc"], wts["wom"]]
    return b, t, d, tm, const, tile, in_specs, args


def _out_dense(h, yc, ym, mod, mod_row, norm_g, wts, ffn, final_g):
    b, t, d, tm, const, tile, in_specs, args = _out_common(h, yc, ym, mod, mod_row, norm_g, wts, FFN_ROW_TILE)
    w1, w3, w2 = ffn
    in_specs += [const(w1.shape), const(w3.shape), const(w2.shape)]
    args += [w1, w3, w2]
    if final_g is not None:
        in_specs.append(const((1, d)))
        args.append(final_g)
    return pl.pallas_call(
        functools.partial(_out_dense_kernel, chunks=_ff_chunks(w1.shape[1]), final=final_g is not None),
        grid=(b, t // tm),
        in_specs=in_specs,
        out_specs=tile(d),
        out_shape=jax.ShapeDtypeStruct((b, t, d), F32),
        compiler_params=_params("parallel", "parallel"),
        name="out_dense_ffn",
    )(*args)


def _out_moe(h, yc, ym, mod, mod_row, norm_g, wts, router_pad, n_exp):
    b, t, d, tm, const, tile, in_specs, args = _out_common(h, yc, ym, mod, mod_row, norm_g, wts, ROW_TILE)
    in_specs.append(const(router_pad.shape))
    args.append(router_pad)
    per = d // LANES
    top2 = pl.BlockSpec((1, TOP_K, tm), lambda i, j: (i, 0, j))
    return pl.pallas_call(
        functools.partial(_out_moe_kernel, n_exp=n_exp),
        grid=(b, t // tm),
        in_specs=in_specs,
        out_specs=[tile(d), pl.BlockSpec((tm * per, LANES), lambda i, j: (i * (t // tm) + j, 0)),
                   top2, top2],
        out_shape=[jax.ShapeDtypeStruct((b, t, d), F32), jax.ShapeDtypeStruct((b * t * per, LANES), F32),
                   jax.ShapeDtypeStruct((b, TOP_K, t), jnp.int32), jax.ShapeDtypeStruct((b, TOP_K, t), F32)],
        compiler_params=_params("parallel", "parallel"),
        name="out_router",
    )(*args)


def _expert_kernel(be_ref, nu_ref, tok_ref, toknext_ref, u_hbm, w1_ref, w3_ref, w2_ref, y_ref, xbuf, sem,
                   *, chunks, per):
    i = pl.program_id(0)
    n_used = nu_ref[0]
    blk = xbuf.shape[1] // per
    slot = i % 2

    def row_copy(idx_ref, r, to_slot):
        src = pl.multiple_of(idx_ref[0, 0, r], per)
        return pltpu.make_async_copy(u_hbm.at[pl.ds(src, per), :],
                                     xbuf.at[to_slot, pl.ds(r * per, per), :], sem.at[to_slot])

    @pl.when(i == 0)
    def _():
        def body(r, c):
            row_copy(tok_ref, r, 0).start()
            return c
        lax.fori_loop(0, blk, body, 0, unroll=ISSUE_UNROLL)

    @pl.when(i <= n_used)
    def _():
        pltpu.make_async_copy(u_hbm.at[pl.ds(0, blk * per), :], xbuf.at[slot], sem.at[slot]).wait()

    @pl.when(i < n_used)
    def _():
        for r in range(blk):
            row_copy(toknext_ref, r, 1 - slot).start()
        x = _tiles_to_rows(xbuf.at[slot], blk, per).astype(BF16)
        acc = None
        for c0, c1 in chunks:
            a = _dot(x, w1_ref[0, :, c0:c1])
            hid = (a * jax.nn.sigmoid(a) * _dot(x, w3_ref[0, :, c0:c1])).astype(BF16)
            part = _dot(hid, w2_ref[0, c0:c1, :])
            acc = part if acc is None else acc + part
        _rows_to_tiles(y_ref, acc)

    @pl.when(i >= n_used)
    def _():
        y_ref[...] = jnp.zeros(y_ref.shape, F32)


def _experts(u_tiles, tok_buf, blk_e, n_used, w1, w3, w2):
    d, f = w1.shape[1], w1.shape[2]
    per = d // LANES
    nblk = tok_buf.shape[0] // MOE_BLOCK
    tok3 = (tok_buf * per).reshape(nblk, 1, MOE_BLOCK)
    smem = lambda fn: pl.BlockSpec((1, 1, MOE_BLOCK), fn, memory_space=pltpu.SMEM)
    grid_spec = pltpu.PrefetchScalarGridSpec(
        num_scalar_prefetch=2,
        grid=(nblk,),
        in_specs=[
            smem(lambda i, be, nu: (i, 0, 0)),
            smem(lambda i, be, nu: (jnp.minimum(i + 1, nblk - 1), 0, 0)),
            pl.BlockSpec(memory_space=pl.ANY),
            pl.BlockSpec((1, d, f), lambda i, be, nu: (be[i], 0, 0)),
            pl.BlockSpec((1, d, f), lambda i, be, nu: (be[i], 0, 0)),
            pl.BlockSpec((1, f, d), lambda i, be, nu: (be[i], 0, 0)),
        ],
        out_specs=pl.BlockSpec((MOE_BLOCK * per, LANES), lambda i, be, nu: (i, 0)),
        scratch_shapes=[pltpu.VMEM((2, MOE_BLOCK * per, LANES), F32), pltpu.SemaphoreType.DMA((2,))],
    )
    return pl.pallas_call(
        functools.partial(_expert_kernel, chunks=_ff_chunks(f), per=per),
        grid_spec=grid_spec,
        out_shape=jax.ShapeDtypeStruct((nblk * MOE_BLOCK * per, LANES), F32),
        compiler_params=_params("arbitrary"),
        name="moe_experts",
    )(blk_e, n_used, tok3, tok3, u_tiles, w1, w3, w2)


def _combine_kernel(dst_ref, dnext_ref, h_ref, gate_ref, mod_ref, *rest, final):
    if final:
        gf_ref, y_hbm, o_ref, ybuf, sem = rest
    else:
        y_hbm, o_ref, ybuf, sem = rest
    i = pl.program_id(0)
    n = pl.num_programs(0)
    tc, d = h_ref.shape
    per = d // LANES

    def issue(idx_ref, slot):
        def body(r, c):
            for k in range(TOP_K):
                src = pl.multiple_of(idx_ref[0, 0, TOP_K * r + k], per)
                pltpu.make_async_copy(y_hbm.at[pl.ds(src, per), :],
                                      ybuf.at[slot, k, pl.ds(pl.multiple_of(r * per, per), per), :],
                                      sem.at[slot]).start()
            return c
        lax.fori_loop(0, tc, body, 0, unroll=ISSUE_UNROLL)

    @pl.when(i == 0)
    def _():
        issue(dst_ref, 0)

    slot = i % 2

    @pl.when(i + 1 < n)
    def _():
        issue(dnext_ref, 1 - slot)

    for k in range(TOP_K):
        pltpu.make_async_copy(y_hbm.at[pl.ds(0, tc * per), :], ybuf.at[slot, k], sem.at[slot]).wait()

    gate = gate_ref[...]
    y = (gate[:, 0:1] * _tiles_to_rows(ybuf.at[slot, 0], tc, per)
         + gate[:, 1:2] * _tiles_to_rows(ybuf.at[slot, 1], tc, per))
    h = h_ref[...] + mod_ref[0, 5:6, :] * y
    if final:
        h = h * lax.rsqrt(jnp.mean(h * h, axis=-1, keepdims=True) + EPS) * gf_ref[...]
    o_ref[...] = h


def _combine(h_flat, y, dest, gates, mod, mod_row, tokens_per_batch, final_g):
    n_tok, d = h_flat.shape
    tc = COMBINE_TILE
    steps = n_tok // tc
    per_b = tokens_per_batch // tc
    per = d // LANES
    dst2 = (dest * per).reshape(steps, 1, TOP_K * tc)
    smem = lambda fn: pl.BlockSpec((1, 1, TOP_K * tc), fn, memory_space=pltpu.SMEM)
    in_specs = [
        smem(lambda i: (i, 0, 0)),
        smem(lambda i: (jnp.minimum(i + 1, steps - 1), 0, 0)),
        pl.BlockSpec((tc, d), lambda i: (i, 0)),
        pl.BlockSpec((tc, TOP_K), lambda i: (i, 0)),
        pl.BlockSpec((1, 6, d), (lambda i: (i // per_b, 0, 0)) if mod_row is None
                     else (lambda i: (mod_row, 0, 0))),
    ]
    args = [dst2, dst2, h_flat, gates, mod]
    if final_g is not None:
        in_specs.append(pl.BlockSpec((1, d), lambda i: (0, 0)))
        args.append(final_g)
    in_specs.append(pl.BlockSpec(memory_space=pl.ANY))
    args.append(y)
    return pl.pallas_call(
        functools.partial(_combine_kernel, final=final_g is not None),
        grid=(steps,),
        in_specs=in_specs,
        out_specs=pl.BlockSpec((tc, d), lambda i: (i, 0)),
        out_shape=jax.ShapeDtypeStruct((n_tok, d), F32),
        scratch_shapes=[pltpu.VMEM((2, TOP_K, tc * per, LANES), F32), pltpu.SemaphoreType.DMA((2,))],
        compiler_params=_params("arbitrary"),
        name="moe_combine",
    )(*args)


def _routing(top_i, n_exp):
    e = top_i.reshape(-1)
    a = e.shape[0]
    experts = jnp.arange(n_exp, dtype=jnp.int32)
    onehot = (e[:, None] == experts[None, :]).astype(jnp.int32)
    csum = jnp.cumsum(onehot, axis=0)
    rank = jnp.sum(onehot * csum, axis=1) - 1
    counts = csum[-1]
    padded = (counts + MOE_BLOCK - 1) // MOE_BLOCK * MOE_BLOCK
    pend = jnp.cumsum(padded)
    pstart = pend - padded
    start = jnp.cumsum(counts) - counts
    dest = pstart[e] + rank
    cap = -(-a // MOE_BLOCK) * MOE_BLOCK + n_exp * MOE_BLOCK
    nblk = cap // MOE_BLOCK
    blk_first = jnp.arange(nblk, dtype=jnp.int32) * MOE_BLOCK
    blk_e = jnp.minimum(jnp.sum((pend[None, :] <= blk_first[:, None]).astype(jnp.int32), axis=1), n_exp - 1)
    n_used = (pend[-1] // MOE_BLOCK).astype(jnp.int32).reshape(1)
    order = jnp.argsort(e).astype(jnp.int32)
    row = jnp.arange(cap, dtype=jnp.int32)
    row_e = jnp.repeat(blk_e, MOE_BLOCK)
    within = row - pstart[row_e]
    valid = (within < counts[row_e]) & (row < pend[-1])
    tok_buf = jnp.where(valid, order[jnp.clip(start[row_e] + within, 0, a - 1)] // TOP_K, 0)
    return dest.astype(jnp.int32), tok_buf.astype(jnp.int32), blk_e.astype(jnp.int32), n_used


def _moe_mixer(h_mid, u, top_i, top_g, mod, mod_row, moe_w, n_exp, final_g):
    b, t, d = h_mid.shape
    n_tok = b * t
    dest, tok_buf, blk_e, n_used = _routing(top_i, n_exp)
    y = _experts(u, tok_buf, blk_e, n_used, *moe_w)
    out = _combine(h_mid.reshape(n_tok, d), y, dest, top_g.reshape(n_tok, TOP_K), mod, mod_row, t, final_g)
    return out.reshape(b, t, d)


def _layer_weights(w_in_l, conv_w_l, b_gates_l, w_out_l):
    cd = conv_w_l.shape[1]
    n_gate = b_gates_l.size
    md = (w_in_l.shape[1] - 3 * cd - n_gate) // 4
    q0 = 3 * cd
    g0 = q0 + 3 * md
    wq, wv = w_in_l[:, q0:q0 + md], w_in_l[:, q0 + 2 * md:g0]
    wg, wo = w_in_l[:, g0:g0 + n_gate], w_in_l[:, g0 + n_gate:]
    return {
        "wc": w_in_l[:, :q0].astype(BF16),
        "wf_full": jnp.concatenate([wq, wv, wo, wg], axis=1).T.astype(BF16),
        "wf_state": jnp.concatenate([wq, wv, wg], axis=1).T.astype(BF16),
        "wk": w_in_l[:, q0 + md:q0 + 2 * md].astype(BF16),
        "bg": b_gates_l.reshape(n_gate, 1),
        "cw": conv_w_l,
        "woc": w_out_l[:cd].astype(BF16),
        "wom": w_out_l[cd:].astype(BF16),
    }


def kernel(x, c, ctx, c_ctx, norm1_g, norm2_g, w_ada, b_ada, w_in, conv_w, b_gates, mlstm_norm_g, w_out,
           ffn_w1, ffn_w3, ffn_w2, moe_router, moe_w1, moe_w3, moe_w2, final_norm_g):
    b, t, d = x.shape
    tc = ctx.shape[1]
    depth = w_in.shape[0]
    n_exp = moe_router.shape[-1]
    assert t % GRID_W == 0 and t % CHUNK == 0 and tc % CHUNK == 0 and d % (SUBLANES * LANES) == 0

    ctx_row = b
    n_rows = -(-(b + 1) // SUBLANES) * SUBLANES
    c_all = jnp.concatenate([c, c_ctx[None, :], jnp.zeros((n_rows - b - 1, d), F32)], axis=0)
    mod_all = _ada(c_all, w_ada, b_ada).reshape(depth, n_rows, 6, d)
    final_g = final_norm_g.reshape(1, d)

    h, hc = x, ctx
    for layer in range(depth):
        last = layer == depth - 1
        j = layer // 2
        mod = mod_all[layer]
        wts = _layer_weights(w_in[layer], conv_w[layer], b_gates[layer], w_out[layer])
        n1 = norm1_g[layer].reshape(1, d)
        n2 = norm2_g[layer].reshape(1, d)

        ql, kl, vl, gl, ycl, ogl = _in_proj(h, mod, None, n1, wts, GRID_W, True)
        res_c = _in_proj(hc, mod, ctx_row, n1, wts, tc, not last)
        qc, kc, vc, gc = res_c[:4]
        ogc = None if last else res_c[5]
        ymc, yml = _mlstm(gc, gl, (kc, qc, vc), (kl, ql, vl), ogc, ogl,
                          mlstm_norm_g[layer].reshape(1, -1), not last)

        def mixer(hh, yc_, ym_, mod_row, fin):
            if layer % 2 == 0:
                ffn = (ffn_w1[j].astype(BF16), ffn_w3[j].astype(BF16), ffn_w2[j].astype(BF16))
                return _out_dense(hh, yc_, ym_, mod, mod_row, n2, wts, ffn, fin)
            r_full = jnp.pad(moe_router[j], ((0, 0), (0, LANES - n_exp)))
            r_hi = r_full.astype(BF16)
            r_lo = (r_full - r_hi.astype(F32)).astype(BF16)
            router_pad = jnp.concatenate([r_hi, r_lo], axis=1)
            h_mid, u2, top_i, top_g = _out_moe(hh, yc_, ym_, mod, mod_row, n2, wts, router_pad, n_exp)
            top_i, top_g = top_i.transpose(0, 2, 1), top_g.transpose(0, 2, 1)
            moe_w = (moe_w1[j].astype(BF16), moe_w3[j].astype(BF16), moe_w2[j].astype(BF16))
            return _moe_mixer(h_mid, u2, top_i, top_g, mod, mod_row, moe_w, n_exp, fin)

        h = mixer(h, ycl, yml, None, final_g if last else None)
        if not last:
            hc = mixer(hc, res_c[4], ymc, ctx_row, None)
    return h
```

```python
import functools

import jax
import jax.numpy as jnp
from jax import lax
from jax.experimental import pallas as pl
from jax.experimental.pallas import tpu as pltpu

GRID_W = 64
CHUNK = 128
N_GATES = 4
EPS = 1e-6
TOP_K = 2
ROW_TILE = 1024
FFN_ROW_TILE = 512
MOE_BLOCK = 512
COMBINE_TILE = 512
SUB_TILE = 256
FF_CHUNK = 1024
ADA_COL_TILES = 4
ISSUE_UNROLL = 8
LANES = 128
SUBLANES = 8
AUG_ROWS = 32
VMEM_LIMIT = 56 * 1024 * 1024

F32 = jnp.float32
BF16 = jnp.bfloat16
HIGHEST = lax.Precision.HIGHEST


def _dot(a, b):
    return jnp.dot(a, b, preferred_element_type=F32)


def _dot_nt(a, b):
    return lax.dot_general(a, b, (((1,), (1,)), ((), ())), preferred_element_type=F32)


def _dot_tn(a, b):
    return lax.dot_general(a, b, (((0,), (0,)), ((), ())), preferred_element_type=F32)


def _rows_to_tiles(ref, x, row0=0):
    n, d = x.shape
    per = d // LANES
    for s in range(per):
        ref[pl.ds(row0 * per + s, n, stride=per), :] = x[:, s * LANES:(s + 1) * LANES]


def _tiles_to_rows(ref, n, per):
    return jnp.concatenate([ref[pl.ds(s, n, stride=per), :] for s in range(per)], axis=1)


def _sub_tiles(tm):
    sub = min(SUB_TILE, tm)
    return [slice(r0, r0 + sub) for r0 in range(0, tm, sub)]


def _params(*sem):
    return pltpu.CompilerParams(dimension_semantics=sem, vmem_limit_bytes=VMEM_LIMIT)


def _ff_chunks(d_ff):
    out, c0 = [], 0
    while c0 < d_ff:
        c1 = min(c0 + FF_CHUNK, d_ff)
        out.append((c0, c1))
        c0 = c1
    return out


def _ada_kernel(c_ref, w_ref, b_ref, o_ref):
    s = c_ref[...]
    s = s * jax.nn.sigmoid(s)
    o_ref[0] = jnp.dot(s, w_ref[0], precision=HIGHEST, preferred_element_type=F32) + b_ref[0]


def _ada(c_all, w_ada, b_ada):
    depth, d, d6 = w_ada.shape
    rows = c_all.shape[0]
    tn = d6 // ADA_COL_TILES
    return pl.pallas_call(
        _ada_kernel,
        grid=(depth, d6 // tn),
        in_specs=[
            pl.BlockSpec((rows, d), lambda l, j: (0, 0)),
            pl.BlockSpec((1, d, tn), lambda l, j: (l, 0, j)),
            pl.BlockSpec((1, 1, tn), lambda l, j: (l, 0, j)),
        ],
        out_specs=pl.BlockSpec((1, rows, tn), lambda l, j: (l, 0, j)),
        out_shape=jax.ShapeDtypeStruct((depth, rows, d6), F32),
        compiler_params=_params("parallel", "parallel"),
        name="adaln_mod",
    )(c_all, w_ada, b_ada.reshape(depth, 1, d6))


def _norm_mod(x, g, shift, scale):
    y = x * lax.rsqrt(jnp.mean(x * x, axis=-1, keepdims=True) + EPS)
    return (y * g) * (1.0 + scale) + shift


def _in_kernel(x_ref, mod_ref, g_ref, wf_ref, wk_ref, bg_ref, *rest, row_w, full, kscale):
    if full:
        wc_ref, cw_ref, q_ref, k_ref, v_ref, gt_ref, yc_ref, og_ref = rest
    else:
        q_ref, k_ref, v_ref, gt_ref = rest
    md = q_ref.shape[1]
    for rows in _sub_tiles(x_ref.shape[1]):
        u = _norm_mod(x_ref[0, rows, :], g_ref[...], mod_ref[0, 0:1, :], mod_ref[0, 1:2, :]).astype(BF16)
        ft = _dot_nt(wf_ref[...], u)
        q_ref[0, :, rows] = ft[:md].astype(BF16)
        k_ref[0, rows, :] = (_dot(u, wk_ref[...]) * kscale).astype(BF16)
        v_ref[0, :, rows] = ft[md:2 * md].astype(BF16)
        gt_ref[0, :, rows] = ft[-bg_ref.shape[0]:] + bg_ref[...]
        if full:
            cd = cw_ref.shape[1]
            c3 = _dot(u, wc_ref[...])
            z = c3[:, cd:2 * cd] * c3[:, 2 * cd:]
            n = z.shape[0]
            t = lax.broadcasted_iota(jnp.int32, (n, 1), 0) % row_w
            zprev = jnp.where(t == 0, 0.0, pltpu.roll(z, 1, axis=0))
            znext = jnp.where(t == row_w - 1, 0.0, pltpu.roll(z, n - 1, axis=0))
            conv = zprev * cw_ref[0:1, :] + z * cw_ref[1:2, :] + znext * cw_ref[2:3, :]
            yc_ref[0, rows, :] = (c3[:, :cd] * conv).astype(BF16)
            og_ref[0, :, rows] = jax.nn.sigmoid(ft[2 * md:3 * md]).astype(BF16)


def _in_proj(h, mod, mod_row, norm_g, wts, row_w, full):
    b, t, d = h.shape
    tm = min(ROW_TILE, t)
    md = wts["wk"].shape[1]
    n_gate = wts["bg"].shape[0]
    head_dim = md // (n_gate // N_GATES)
    wf = wts["wf_full"] if full else wts["wf_state"]
    const = lambda shape: pl.BlockSpec(shape, lambda i, j: (0,) * len(shape))
    tile = lambda w: pl.BlockSpec((1, tm, w), lambda i, j: (i, j, 0))
    tile_t = pl.BlockSpec((1, md, tm), lambda i, j: (i, 0, j))
    feat_major = jax.ShapeDtypeStruct((b, md, t), BF16)
    in_specs = [
        tile(d),
        pl.BlockSpec((1, 6, d), (lambda i, j: (i, 0, 0)) if mod_row is None
                     else (lambda i, j: (mod_row, 0, 0))),
        const((1, d)), const(wf.shape), const(wts["wk"].shape), const(wts["bg"].shape),
    ]
    args = [h, mod, norm_g, wf, wts["wk"], wts["bg"]]
    out_specs = [tile_t, tile(md), tile_t, pl.BlockSpec((1, n_gate, tm), lambda i, j: (i, 0, j))]
    out_shape = [feat_major, jax.ShapeDtypeStruct((b, t, md), BF16), feat_major,
                 jax.ShapeDtypeStruct((b, n_gate, t), F32)]
    if full:
        cd = wts["cw"].shape[1]
        in_specs += [const(wts["wc"].shape), const(wts["cw"].shape)]
        args += [wts["wc"], wts["cw"]]
        out_specs += [tile(cd), tile_t]
        out_shape += [jax.ShapeDtypeStruct((b, t, cd), BF16), feat_major]
    return pl.pallas_call(
        functools.partial(_in_kernel, row_w=row_w, full=full, kscale=head_dim ** -0.5),
        grid=(b, t // tm),
        in_specs=in_specs,
        out_specs=out_specs,
        out_shape=out_shape,
        compiler_params=_params("parallel", "parallel"),
        name="in_proj",
    )(*args)


def _lane_scan(x, op, ident, reverse):
    n = x.shape[1]
    lane = lax.broadcasted_iota(jnp.int32, x.shape, 1)
    s = 1
    while s < n:
        if reverse:
            shifted = jnp.where(lane < n - s, pltpu.roll(x, n - s, axis=1), ident)
        else:
            shifted = jnp.where(lane >= s, pltpu.roll(x, s, axis=1), ident)
        x = op(x, shifted)
        s *= 2
    return x


def _mlstm_kernel(gc_ref, gl_ref, kc_ref, qc_ref, vc_ref, kl_ref, ql_ref, vl_ref, ogl_ref, gain_ref,
                  *rest, need_ctx):
    if need_ctx:
        ogc_ref, yc_ref, yl_ref, rt_ref, ra_ref, rp_ref, rw_ref, s_ref, upd_ref, st_ref = rest
    else:
        yl_ref, rt_ref, ra_ref, rp_ref, rw_ref, s_ref, upd_ref, st_ref = rest
        ogc_ref = yc_ref = None
    L = CHUNK
    hd = kl_ref.shape[2]
    ncc = kc_ref.shape[1] // L
    ncl = kl_ref.shape[1] // L
    nct = ncc + ncl

    head = pl.program_id(1)
    n_heads = gl_ref.shape[1] // N_GATES

    def gate_rows(gate, hh):
        r = gate * n_heads + hh
        return jnp.concatenate([ref[0, r:r + 1, c * L:(c + 1) * L]
                                for ref, n in ((gc_ref, ncc), (gl_ref, ncl)) for c in range(n)], axis=0)

    @pl.when(head == 0)
    def _():
        for d in range(2):
            rev = d == 1
            last = 0 if rev else L - 1
            for hh in range(n_heads):
                ig = gate_rows(2 * d, hh)
                lf = jax.nn.log_sigmoid(gate_rows(2 * d + 1, hh))
                a = _lane_scan(lf, jnp.add, 0.0, rev)
                r = ig - a
                p = _lane_scan(r, jnp.maximum, -jnp.inf, rev)
                ra_ref[d, hh, 0:nct, :] = a
                rp_ref[d, hh, 0:nct, :] = p
                rw_ref[d, hh, 0:nct, :] = jnp.exp(r - p[:, last:last + 1])
                r_pad = jnp.concatenate([r, jnp.zeros((L - nct, L), F32)], axis=0)
                rt_ref[d, hh] = r_pad.T

    spos = lax.broadcasted_iota(jnp.int32, (L, L), 0)
    tpos = lax.broadcasted_iota(jnp.int32, (L, L), 1)
    ones_row = jnp.where(lax.broadcasted_iota(jnp.int32, (AUG_ROWS, L), 0) == 0, 1.0, 0.0).astype(BF16)
    gain = gain_ref[...]

    ctx_seq = (kc_ref, qc_ref, vc_ref, ogc_ref, yc_ref, ncc, 0, need_ctx)
    lat_seq = (kl_ref, ql_ref, vl_ref, ogl_ref, yl_ref, ncl, ncc, True)

    for k_ref, q_ref, v_ref, _, _, n, g0, want in (ctx_seq, lat_seq):
        for j in range(n):
            g = g0 + j
            k = k_ref[0, pl.ds(j * L, L), :]
            if want:
                s_ref[g] = _dot(k, q_ref[0, :, pl.ds(j * L, L)])
            vaug = jnp.concatenate([v_ref[0, :, pl.ds(j * L, L)], ones_row], axis=0).astype(F32)
            for d in range(2):
                vw = (vaug * rw_ref[d, head, g:g + 1, :]).astype(BF16)
                upd_ref[d, g] = _dot(vw, k)

    m_in = {}
    for d in range(2):
        last = 0 if d == 1 else L - 1
        caug, m = jnp.zeros((hd + AUG_ROWS, hd), F32), jnp.zeros((1, 1), F32)
        for _, _, _, _, _, n, g0, _ in (ctx_seq, lat_seq):
            for i in range(n):
                g = g0 + (n - 1 - i if d == 1 else i)
                st_ref[d, g] = caug.astype(BF16)
                m_in[d, g] = m
                b_tot = ra_ref[d, head, g:g + 1, last:last + 1]
                p_last = rp_ref[d, head, g:g + 1, last:last + 1]
                m_last = jnp.maximum(p_last, m)
                caug = jnp.exp(m - m_last) * caug + jnp.exp(p_last - m_last) * upd_ref[d, g]
                m = b_tot + m_last

    for k_ref, q_ref, v_ref, og_ref, y_ref, n, g0, want in (ctx_seq, lat_seq):
        for j in range(n if want else 0):
            g = g0 + j
            lanes = pl.ds(j * L, L)
            q = q_ref[0, :, lanes]
            vaug = jnp.concatenate([v_ref[0, :, lanes], ones_row], axis=0)
            rhs, scale = [], []
            for d in range(2):
                a_row = ra_ref[d, head, g:g + 1, :]
                p_row = rp_ref[d, head, g:g + 1, :]
                m = m_in[d, g]
                mask = (spos >= tpos) if d == 1 else (spos <= tpos)
                w = jnp.where(mask, jnp.exp(rt_ref[d, head, :, g:g + 1] - p_row), 0.0)
                m_row = jnp.maximum(p_row, m)
                sw = (s_ref[g] * w * jnp.exp(p_row - m_row)).astype(BF16)
                qs = (q.astype(F32) * jnp.exp(m - m_row)).astype(BF16)
                rhs.append((sw, qs))
                scale.append(jnp.exp(-(a_row + m_row)))
            zero = jnp.zeros((hd, L), BF16)
            nd = _dot(jnp.concatenate([vaug, st_ref[0, g], st_ref[1, g]], axis=1),
                      jnp.concatenate([jnp.concatenate([rhs[0][0], rhs[1][0]], axis=1),
                                       jnp.concatenate([rhs[0][1], zero], axis=1),
                                       jnp.concatenate([zero, rhs[1][1]], axis=1)], axis=0))
            hh = None
            for d in range(2):
                part = nd[:, d * L:(d + 1) * L]
                den = jnp.maximum(jnp.abs(part[hd:hd + 1, :]), scale[d])
                hout = part[:hd, :] * (1.0 / den)
                hh = hout if hh is None else hh + hout
            hn = hh * lax.rsqrt(jnp.mean(hh * hh, axis=0, keepdims=True) + EPS)
            y_ref[0, :, lanes] = (hn * gain * og_ref[0, :, lanes].astype(F32)).astype(BF16)


def _mlstm(gates_c, gates_l, kqv_c, kqv_l, og_c, og_l, gain, need_ctx):
    b, tc, md = kqv_c[0].shape
    tl = kqv_l[0].shape[1]
    n_gate = gates_c.shape[1]
    heads = n_gate // N_GATES
    hd = md // heads
    L = CHUNK
    nct = (tc + tl) // L
    assert nct <= L
    nrow = -(-nct // SUBLANES) * SUBLANES
    tok = lambda t: pl.BlockSpec((1, t, hd), lambda i, h: (i, 0, h))
    feat = lambda t: pl.BlockSpec((1, hd, t), lambda i, h: (i, h, 0))
    gate_spec = lambda t: pl.BlockSpec((1, n_gate, t), lambda i, h: (i, 0, 0))
    in_specs = [gate_spec(tc), gate_spec(tl),
                tok(tc), feat(tc), feat(tc), tok(tl), feat(tl), feat(tl), feat(tl),
                pl.BlockSpec((hd, 1), lambda i, h: (h, 0))]
    args = [gates_c, gates_l, *kqv_c, *kqv_l, og_l, gain.reshape(md, 1)]
    out_specs = [feat(tl)]
    out_shape = [jax.ShapeDtypeStruct((b, md, tl), BF16)]
    if need_ctx:
        in_specs.append(feat(tc))
        args.append(og_c)
        out_specs = [feat(tc)] + out_specs
        out_shape = [jax.ShapeDtypeStruct((b, md, tc), BF16)] + out_shape
    res = pl.pallas_call(
        functools.partial(_mlstm_kernel, need_ctx=need_ctx),
        grid=(b, heads),
        in_specs=in_specs,
        out_specs=out_specs,
        out_shape=out_shape,
        scratch_shapes=[
            pltpu.VMEM((2, heads, L, L), F32),
            pltpu.VMEM((2, heads, nrow, L), F32),
            pltpu.VMEM((2, heads, nrow, L), F32),
            pltpu.VMEM((2, heads, nrow, L), F32),
            pltpu.VMEM((nct, L, L), F32),
            pltpu.VMEM((2, nct, hd + AUG_ROWS, hd), F32),
            pltpu.VMEM((2, nct, hd + AUG_ROWS, hd), BF16),
        ],
        compiler_params=_params("parallel", "arbitrary"),
        name="mlstm_scan",
    )(*args)
    return (res[0], res[1]) if need_ctx else (None, res[0])


def _mix_residual(h_ref, yc_ref, ym_ref, mod_ref, woc_ref, wom_ref, rows):
    mix = _dot(yc_ref[0, rows, :], woc_ref[...]) + _dot_tn(ym_ref[0, :, rows], wom_ref[...])
    return h_ref[0, rows, :] + mod_ref[0, 2:3, :] * mix


def _out_dense_kernel(h_ref, yc_ref, ym_ref, mod_ref, g2_ref, woc_ref, wom_ref, w1_ref, w3_ref, w2_ref,
                      *rest, chunks, final):
    if final:
        gf_ref, o_ref = rest
    else:
        (o_ref,) = rest
    h = _mix_residual(h_ref, yc_ref, ym_ref, mod_ref, woc_ref, wom_ref, slice(None))
    u = _norm_mod(h, g2_ref[...], mod_ref[0, 3:4, :], mod_ref[0, 4:5, :]).astype(BF16)
    acc = None
    for c0, c1 in chunks:
        a = _dot(u, w1_ref[:, c0:c1])
        hid = (a * jax.nn.sigmoid(a) * _dot(u, w3_ref[:, c0:c1])).astype(BF16)
        part = _dot(hid, w2_ref[c0:c1, :])
        acc = part if acc is None else acc + part
    h = h + mod_ref[0, 5:6, :] * acc
    if final:
        h = h * lax.rsqrt(jnp.mean(h * h, axis=-1, keepdims=True) + EPS) * gf_ref[...]
    o_ref[0] = h


def _out_moe_kernel(h_ref, yc_ref, ym_ref, mod_ref, g2_ref, woc_ref, wom_ref, router_ref,
                    hm_ref, u_ref, ti_ref, tg_ref, *, n_exp):
    for rows in _sub_tiles(h_ref.shape[1]):
        _out_moe_rows(h_ref, yc_ref, ym_ref, mod_ref, g2_ref, woc_ref, wom_ref, router_ref,
                      hm_ref, u_ref, ti_ref, tg_ref, rows, n_exp)


def _out_moe_rows(h_ref, yc_ref, ym_ref, mod_ref, g2_ref, woc_ref, wom_ref, router_ref,
                  hm_ref, u_ref, ti_ref, tg_ref, rows, n_exp):
    h = _mix_residual(h_ref, yc_ref, ym_ref, mod_ref, woc_ref, wom_ref, rows)
    hm_ref[0, rows, :] = h
    u = _norm_mod(h, g2_ref[...], mod_ref[0, 3:4, :], mod_ref[0, 4:5, :])
    _rows_to_tiles(u_ref, u, rows.start)
    u_hi = u.astype(BF16)
    u_lo = (u - u_hi.astype(F32)).astype(BF16)
    parts = _dot(u_hi, router_ref[...]) + _dot(u_lo, router_ref[...])
    logits = parts[:, :LANES] + parts[:, LANES:]
    n_pad = -(-n_exp // SUBLANES) * SUBLANES
    lt = logits.T[:n_pad, :]
    row = lax.broadcasted_iota(jnp.int32, lt.shape, 0)
    lt = jnp.where(row < n_exp, lt, -jnp.inf)
    v1 = jnp.max(lt, axis=0, keepdims=True)
    i1 = jnp.min(jnp.where(lt == v1, row, n_pad), axis=0, keepdims=True)
    rest = jnp.where(row == i1, -jnp.inf, lt)
    v2 = jnp.max(rest, axis=0, keepdims=True)
    i2 = jnp.min(jnp.where(rest == v2, row, n_pad), axis=0, keepdims=True)
    e2 = jnp.exp(v2 - v1)
    inv = 1.0 / (1.0 + e2)
    ti_ref[:, rows] = jnp.concatenate([i1, i2], axis=0)
    tg_ref[:, rows] = jnp.concatenate([inv, e2 * inv], axis=0)


def _out_common(h, yc, ym, mod, mod_row, norm_g, wts, row_tile):
    b, t, d = h.shape
    tm = min(row_tile, t)
    const = lambda shape: pl.BlockSpec(shape, lambda i, j: (0,) * len(shape))
    tile = lambda w: pl.BlockSpec((1, tm, w), lambda i, j: (i, j, 0))
    in_specs = [
        tile(d), tile(yc.shape[2]), pl.BlockSpec((1, ym.shape[1], tm), lambda i, j: (i, 0, j)),
        pl.BlockSpec((1, 6, d), (lambda i, j: (i, 0, 0)) if mod_row is None
                     else (lambda i, j: (mod_row, 0, 0))),
        const((1, d)), const(wts["woc"].shape), const(wts["wom"].shape),
    ]
    args = [h, yc, ym, mod, norm_g, wts["woc"], wts["wom"]]
    return b, t, d, tm, const, tile, in_specs, args


def _out_dense(h, yc, ym, mod, mod_row, norm_g, wts, ffn, final_g):
    b, t, d, tm, const, tile, in_specs, args = _out_common(h, yc, ym, mod, mod_row, norm_g, wts, FFN_ROW_TILE)
    w1, w3, w2 = ffn
    in_specs += [const(w1.shape), const(w3.shape), const(w2.shape)]
    args += [w1, w3, w2]
    if final_g is not None:
        in_specs.append(const((1, d)))
        args.append(final_g)
    return pl.pallas_call(
        functools.partial(_out_dense_kernel, chunks=_ff_chunks(w1.shape[1]), final=final_g is not None),
        grid=(b, t // tm),
        in_specs=in_specs,
        out_specs=tile(d),
        out_shape=jax.ShapeDtypeStruct((b, t, d), F32),
        compiler_params=_params("parallel", "parallel"),
        name="out_dense_ffn",
    )(*args)


def _out_moe(h, yc, ym, mod, mod_row, norm_g, wts, router_pad, n_exp):
    b, t, d, tm, const, tile, in_specs, args = _out_common(h, yc, ym, mod, mod_row, norm_g, wts, ROW_TILE)
    in_specs.append(const(router_pad.shape))
    args.append(router_pad)
    per = d // LANES
    top2 = pl.BlockSpec((TOP_K, tm), lambda i, j: (0, i * (t // tm) + j))
    return pl.pallas_call(
        functools.partial(_out_moe_kernel, n_exp=n_exp),
        grid=(b, t // tm),
        in_specs=in_specs,
        out_specs=[tile(d), pl.BlockSpec((tm * per, LANES), lambda i, j: (i * (t // tm) + j, 0)),
                   top2, top2],
        out_shape=[jax.ShapeDtypeStruct((b, t, d), F32), jax.ShapeDtypeStruct((b * t * per, LANES), F32),
                   jax.ShapeDtypeStruct((TOP_K, b * t), jnp.int32), jax.ShapeDtypeStruct((TOP_K, b * t), F32)],
        compiler_params=_params("parallel", "parallel"),
        name="out_router",
    )(*args)


def _expert_kernel(be_ref, nu_ref, tok_ref, toknext_ref, u_hbm, w1_ref, w3_ref, w2_ref, y_ref, xbuf, sem,
                   *, chunks, per):
    i = pl.program_id(0)
    n_used = nu_ref[0]
    blk = xbuf.shape[1] // per
    slot = i % 2

    def row_copy(idx_ref, r, to_slot):
        src = pl.multiple_of(idx_ref[0, 0, r], per)
        return pltpu.make_async_copy(u_hbm.at[pl.ds(src, per), :],
                                     xbuf.at[to_slot, pl.ds(r * per, per), :], sem.at[to_slot])

    @pl.when(i == 0)
    def _():
        def body(r, c):
            row_copy(tok_ref, r, 0).start()
            return c
        lax.fori_loop(0, blk, body, 0, unroll=ISSUE_UNROLL)

    @pl.when(i <= n_used)
    def _():
        pltpu.make_async_copy(u_hbm.at[pl.ds(0, blk * per), :], xbuf.at[slot], sem.at[slot]).wait()

    @pl.when(i < n_used)
    def _():
        for r in range(blk):
            row_copy(toknext_ref, r, 1 - slot).start()
        x = _tiles_to_rows(xbuf.at[slot], blk, per).astype(BF16)
        acc = None
        for c0, c1 in chunks:
            a = _dot(x, w1_ref[0, :, c0:c1])
            hid = (a * jax.nn.sigmoid(a) * _dot(x, w3_ref[0, :, c0:c1])).astype(BF16)
            part = _dot(hid, w2_ref[0, c0:c1, :])
            acc = part if acc is None else acc + part
        _rows_to_tiles(y_ref, acc)

    @pl.when(i >= n_used)
    def _():
        y_ref[...] = jnp.zeros(y_ref.shape, F32)


def _experts(u_tiles, tok_buf, blk_e, n_used, w1, w3, w2):
    d, f = w1.shape[1], w1.shape[2]
    per = d // LANES
    nblk = tok_buf.shape[0] // MOE_BLOCK
    tok3 = (tok_buf * per).reshape(nblk, 1, MOE_BLOCK)
    smem = lambda fn: pl.BlockSpec((1, 1, MOE_BLOCK), fn, memory_space=pltpu.SMEM)
    grid_spec = pltpu.PrefetchScalarGridSpec(
        num_scalar_prefetch=2,
        grid=(nblk,),
        in_specs=[
            smem(lambda i, be, nu: (i, 0, 0)),
            smem(lambda i, be, nu: (jnp.minimum(i + 1, nblk - 1), 0, 0)),
            pl.BlockSpec(memory_space=pl.ANY),
            pl.BlockSpec((1, d, f), lambda i, be, nu: (be[i], 0, 0)),
            pl.BlockSpec((1, d, f), lambda i, be, nu: (be[i], 0, 0)),
            pl.BlockSpec((1, f, d), lambda i, be, nu: (be[i], 0, 0)),
        ],
        out_specs=pl.BlockSpec((MOE_BLOCK * per, LANES), lambda i, be, nu: (i, 0)),
        scratch_shapes=[pltpu.VMEM((2, MOE_BLOCK * per, LANES), F32), pltpu.SemaphoreType.DMA((2,))],
    )
    return pl.pallas_call(
        functools.partial(_expert_kernel, chunks=_ff_chunks(f), per=per),
        grid_spec=grid_spec,
        out_shape=jax.ShapeDtypeStruct((nblk * MOE_BLOCK * per, LANES), F32),
        compiler_params=_params("arbitrary"),
        name="moe_experts",
    )(blk_e, n_used, tok3, tok3, u_tiles, w1, w3, w2)


def _combine_kernel(dst_ref, dnext_ref, h_ref, gate_ref, mod_ref, *rest, final):
    if final:
        gf_ref, y_hbm, o_ref, ybuf, sem = rest
    else:
        y_hbm, o_ref, ybuf, sem = rest
    i = pl.program_id(0)
    n = pl.num_programs(0)
    tc, d = h_ref.shape
    per = d // LANES

    def issue(idx_ref, slot):
        def body(r, c):
            for k in range(TOP_K):
                src = pl.multiple_of(idx_ref[0, 0, k * tc + r], per)
                pltpu.make_async_copy(y_hbm.at[pl.ds(src, per), :],
                                      ybuf.at[slot, k, pl.ds(pl.multiple_of(r * per, per), per), :],
                                      sem.at[slot]).start()
            return c
        lax.fori_loop(0, tc, body, 0, unroll=ISSUE_UNROLL)

    @pl.when(i == 0)
    def _():
        issue(dst_ref, 0)

    slot = i % 2

    @pl.when(i + 1 < n)
    def _():
        issue(dnext_ref, 1 - slot)

    for k in range(TOP_K):
        pltpu.make_async_copy(y_hbm.at[pl.ds(0, tc * per), :], ybuf.at[slot, k], sem.at[slot]).wait()

    gate = jnp.concatenate([gate_ref[...], jnp.zeros((SUBLANES - TOP_K, tc), F32)], axis=0).T
    y = (gate[:, 0:1] * _tiles_to_rows(ybuf.at[slot, 0], tc, per)
         + gate[:, 1:2] * _tiles_to_rows(ybuf.at[slot, 1], tc, per))
    h = h_ref[...] + mod_ref[0, 5:6, :] * y
    if final:
        h = h * lax.rsqrt(jnp.mean(h * h, axis=-1, keepdims=True) + EPS) * gf_ref[...]
    o_ref[...] = h


def _combine(h_flat, y, dest, gates, mod, mod_row, tokens_per_batch, final_g):
    n_tok, d = h_flat.shape
    tc = COMBINE_TILE
    steps = n_tok // tc
    per_b = tokens_per_batch // tc
    per = d // LANES
    dst2 = (dest * per).reshape(TOP_K, steps, tc).transpose(1, 0, 2).reshape(steps, 1, TOP_K * tc)
    smem = lambda fn: pl.BlockSpec((1, 1, TOP_K * tc), fn, memory_space=pltpu.SMEM)
    in_specs = [
        smem(lambda i: (i, 0, 0)),
        smem(lambda i: (jnp.minimum(i + 1, steps - 1), 0, 0)),
        pl.BlockSpec((tc, d), lambda i: (i, 0)),
        pl.BlockSpec((TOP_K, tc), lambda i: (0, i)),
        pl.BlockSpec((1, 6, d), (lambda i: (i // per_b, 0, 0)) if mod_row is None
                     else (lambda i: (mod_row, 0, 0))),
    ]
    args = [dst2, dst2, h_flat, gates, mod]
    if final_g is not None:
        in_specs.append(pl.BlockSpec((1, d), lambda i: (0, 0)))
        args.append(final_g)
    in_specs.append(pl.BlockSpec(memory_space=pl.ANY))
    args.append(y)
    return pl.pallas_call(
        functools.partial(_combine_kernel, final=final_g is not None),
        grid=(steps,),
        in_specs=in_specs,
        out_specs=pl.BlockSpec((tc, d), lambda i: (i, 0)),
        out_shape=jax.ShapeDtypeStruct((n_tok, d), F32),
        scratch_shapes=[pltpu.VMEM((2, TOP_K, tc * per, LANES), F32), pltpu.SemaphoreType.DMA((2,))],
        compiler_params=_params("arbitrary"),
        name="moe_combine",
    )(*args)


def _routing(top_i, n_exp):
    n_tok = top_i.shape[1]
    e = top_i.reshape(-1)
    a = e.shape[0]
    experts = jnp.arange(n_exp, dtype=jnp.int32)
    onehot = (e[:, None] == experts[None, :]).astype(jnp.int32)
    csum = jnp.cumsum(onehot, axis=0)
    rank = jnp.sum(onehot * csum, axis=1) - 1
    counts = csum[-1]
    padded = (counts + MOE_BLOCK - 1) // MOE_BLOCK * MOE_BLOCK
    pend = jnp.cumsum(padded)
    pstart = pend - padded
    start = jnp.cumsum(counts) - counts
    dest = pstart[e] + rank
    cap = -(-a // MOE_BLOCK) * MOE_BLOCK + n_exp * MOE_BLOCK
    nblk = cap // MOE_BLOCK
    blk_first = jnp.arange(nblk, dtype=jnp.int32) * MOE_BLOCK
    blk_e = jnp.minimum(jnp.sum((pend[None, :] <= blk_first[:, None]).astype(jnp.int32), axis=1), n_exp - 1)
    n_used = (pend[-1] // MOE_BLOCK).astype(jnp.int32).reshape(1)
    order = jnp.argsort(e).astype(jnp.int32)
    row = jnp.arange(cap, dtype=jnp.int32)
    row_e = jnp.repeat(blk_e, MOE_BLOCK)
    within = row - pstart[row_e]
    valid = (within < counts[row_e]) & (row < pend[-1])
    tok_buf = jnp.where(valid, order[jnp.clip(start[row_e] + within, 0, a - 1)] % n_tok, 0)
    return dest.astype(jnp.int32).reshape(TOP_K, n_tok), tok_buf.astype(jnp.int32), blk_e.astype(jnp.int32), n_used


def _moe_mixer(h_mid, u, top_i, top_g, mod, mod_row, moe_w, n_exp, final_g):
    b, t, d = h_mid.shape
    n_tok = b * t
    dest, tok_buf, blk_e, n_used = _routing(top_i, n_exp)
    y = _experts(u, tok_buf, blk_e, n_used, *moe_w)
    out = _combine(h_mid.reshape(n_tok, d), y, dest, top_g, mod, mod_row, t, final_g)
    return out.reshape(b, t, d)


def _layer_weights(w_in_l, conv_w_l, b_gates_l, w_out_l):
    cd = conv_w_l.shape[1]
    n_gate = b_gates_l.size
    md = (w_in_l.shape[1] - 3 * cd - n_gate) // 4
    q0 = 3 * cd
    g0 = q0 + 3 * md
    wq, wv = w_in_l[:, q0:q0 + md], w_in_l[:, q0 + 2 * md:g0]
    wg, wo = w_in_l[:, g0:g0 + n_gate], w_in_l[:, g0 + n_gate:]
    return {
        "wc": w_in_l[:, :q0].astype(BF16),
        "wf_full": jnp.concatenate([wq, wv, wo, wg], axis=1).T.astype(BF16),
        "wf_state": jnp.concatenate([wq, wv, wg], axis=1).T.astype(BF16),
        "wk": w_in_l[:, q0 + md:q0 + 2 * md].astype(BF16),
        "bg": b_gates_l.reshape(n_gate, 1),
        "cw": conv_w_l,
        "woc": w_out_l[:cd].astype(BF16),
        "wom": w_out_l[cd:].astype(BF16),
    }


def kernel(x, c, ctx, c_ctx, norm1_g, norm2_g, w_ada, b_ada, w_in, conv_w, b_gates, mlstm_norm_g, w_out,
           ffn_w1, ffn_w3, ffn_w2, moe_router, moe_w1, moe_w3, moe_w2, final_norm_g):
    b, t, d = x.shape
    tc = ctx.shape[1]
    depth = w_in.shape[0]
    n_exp = moe_router.shape[-1]
    assert t % GRID_W == 0 and t % CHUNK == 0 and tc % CHUNK == 0 and d % (SUBLANES * LANES) == 0

    ctx_row = b
    n_rows = -(-(b + 1) // SUBLANES) * SUBLANES
    c_all = jnp.concatenate([c, c_ctx[None, :], jnp.zeros((n_rows - b - 1, d), F32)], axis=0)
    mod_all = _ada(c_all, w_ada, b_ada).reshape(depth, n_rows, 6, d)
    final_g = final_norm_g.reshape(1, d)

    h, hc = x, ctx
    for layer in range(depth):
        last = layer == depth - 1
        j = layer // 2
        mod = mod_all[layer]
        wts = _layer_weights(w_in[layer], conv_w[layer], b_gates[layer], w_out[layer])
        n1 = norm1_g[layer].reshape(1, d)
        n2 = norm2_g[layer].reshape(1, d)

        ql, kl, vl, gl, ycl, ogl = _in_proj(h, mod, None, n1, wts, GRID_W, True)
        res_c = _in_proj(hc, mod, ctx_row, n1, wts, tc, not last)
        qc, kc, vc, gc = res_c[:4]
        ogc = None if last else res_c[5]
        ymc, yml = _mlstm(gc, gl, (kc, qc, vc), (kl, ql, vl), ogc, ogl,
                          mlstm_norm_g[layer].reshape(1, -1), not last)

        def mixer(hh, yc_, ym_, mod_row, fin):
            if layer % 2 == 0:
                ffn = (ffn_w1[j].astype(BF16), ffn_w3[j].astype(BF16), ffn_w2[j].astype(BF16))
                return _out_dense(hh, yc_, ym_, mod, mod_row, n2, wts, ffn, fin)
            r_full = jnp.pad(moe_router[j], ((0, 0), (0, LANES - n_exp)))
            r_hi = r_full.astype(BF16)
            r_lo = (r_full - r_hi.astype(F32)).astype(BF16)
            router_pad = jnp.concatenate([r_hi, r_lo], axis=1)
            h_mid, u2, top_i, top_g = _out_moe(hh, yc_, ym_, mod, mod_row, n2, wts, router_pad, n_exp)
            moe_w = (moe_w1[j].astype(BF16), moe_w3[j].astype(BF16), moe_w2[j].astype(BF16))
            return _moe_mixer(h_mid, u2, top_i, top_g, mod, mod_row, moe_w, n_exp, fin)

        h = mixer(h, ycl, yml, None, final_g if last else None)
        if not last:
            hc = mixer(hc, res_c[4], ymc, ctx_row, None)
    return h
```

```python
import functools

import jax
import jax.numpy as jnp
from jax import lax
from jax.experimental import pallas as pl
from jax.experimental.pallas import tpu as pltpu

GRID_W = 64
CHUNK = 128
N_GATES = 4
EPS = 1e-6
TOP_K = 2
ROW_TILE = 1024
FFN_ROW_TILE = 512
MOE_BLOCK = 512
COMBINE_TILE = 512
SUB_TILE = 256
FF_CHUNK = 1024
ADA_COL_TILES = 4
ISSUE_UNROLL = 8
LANES = 128
SUBLANES = 8
AUG_ROWS = 32
VMEM_LIMIT = 56 * 1024 * 1024

F32 = jnp.float32
BF16 = jnp.bfloat16
HIGHEST = lax.Precision.HIGHEST


def _dot(a, b):
    return jnp.dot(a, b, preferred_element_type=F32)


def _dot_nt(a, b):
    return lax.dot_general(a, b, (((1,), (1,)), ((), ())), preferred_element_type=F32)


def _dot_tn(a, b):
    return lax.dot_general(a, b, (((0,), (0,)), ((), ())), preferred_element_type=F32)


def _rows_to_tiles(ref, x, row0=0):
    n, d = x.shape
    per = d // LANES
    for s in range(per):
        ref[pl.ds(row0 * per + s, n, stride=per), :] = x[:, s * LANES:(s + 1) * LANES]


def _tiles_to_rows(ref, n, per):
    return jnp.concatenate([ref[pl.ds(s, n, stride=per), :] for s in range(per)], axis=1)


def _sub_tiles(tm):
    sub = min(SUB_TILE, tm)
    return [slice(r0, r0 + sub) for r0 in range(0, tm, sub)]


def _params(*sem):
    return pltpu.CompilerParams(dimension_semantics=sem, vmem_limit_bytes=VMEM_LIMIT)


def _ff_chunks(d_ff):
    out, c0 = [], 0
    while c0 < d_ff:
        c1 = min(c0 + FF_CHUNK, d_ff)
        out.append((c0, c1))
        c0 = c1
    return out


def _ada_kernel(c_ref, w_ref, b_ref, o_ref):
    s = c_ref[...]
    s = s * jax.nn.sigmoid(s)
    o_ref[0] = jnp.dot(s, w_ref[0], precision=HIGHEST, preferred_element_type=F32) + b_ref[0]


def _ada(c_all, w_ada, b_ada):
    depth, d, d6 = w_ada.shape
    rows = c_all.shape[0]
    tn = d6 // ADA_COL_TILES
    return pl.pallas_call(
        _ada_kernel,
        grid=(depth, d6 // tn),
        in_specs=[
            pl.BlockSpec((rows, d), lambda l, j: (0, 0)),
            pl.BlockSpec((1, d, tn), lambda l, j: (l, 0, j)),
            pl.BlockSpec((1, 1, tn), lambda l, j: (l, 0, j)),
        ],
        out_specs=pl.BlockSpec((1, rows, tn), lambda l, j: (l, 0, j)),
        out_shape=jax.ShapeDtypeStruct((depth, rows, d6), F32),
        compiler_params=_params("parallel", "parallel"),
        name="adaln_mod",
    )(c_all, w_ada, b_ada.reshape(depth, 1, d6))


def _norm_mod(x, g, shift, scale):
    y = x * lax.rsqrt(jnp.mean(x * x, axis=-1, keepdims=True) + EPS)
    return (y * g) * (1.0 + scale) + shift


def _in_kernel(x_ref, mod_ref, g_ref, wf_ref, wk_ref, bg_ref, *rest, row_w, full, kscale):
    if full:
        wc_ref, cw_ref, q_ref, k_ref, v_ref, gt_ref, yc_ref, og_ref = rest
    else:
        q_ref, k_ref, v_ref, gt_ref = rest
    md = q_ref.shape[1]
    for rows in _sub_tiles(x_ref.shape[1]):
        u = _norm_mod(x_ref[0, rows, :], g_ref[...], mod_ref[0, 0:1, :], mod_ref[0, 1:2, :]).astype(BF16)
        ft = _dot_nt(wf_ref[...], u)
        q_ref[0, :, rows] = ft[:md].astype(BF16)
        k_ref[0, rows, :] = (_dot(u, wk_ref[...]) * kscale).astype(BF16)
        v_ref[0, :, rows] = ft[md:2 * md].astype(BF16)
        gt_ref[0, :, rows] = ft[-bg_ref.shape[0]:] + bg_ref[...]
        if full:
            cd = cw_ref.shape[1]
            c3 = _dot(u, wc_ref[...])
            z = c3[:, cd:2 * cd] * c3[:, 2 * cd:]
            n = z.shape[0]
            t = lax.broadcasted_iota(jnp.int32, (n, 1), 0) % row_w
            zprev = jnp.where(t == 0, 0.0, pltpu.roll(z, 1, axis=0))
            znext = jnp.where(t == row_w - 1, 0.0, pltpu.roll(z, n - 1, axis=0))
            conv = zprev * cw_ref[0:1, :] + z * cw_ref[1:2, :] + znext * cw_ref[2:3, :]
            yc_ref[0, rows, :] = (c3[:, :cd] * conv).astype(BF16)
            og_ref[0, :, rows] = jax.nn.sigmoid(ft[2 * md:3 * md]).astype(BF16)


def _in_proj(h, mod, mod_row, norm_g, wts, row_w, full):
    b, t, d = h.shape
    tm = min(ROW_TILE, t)
    md = wts["wk"].shape[1]
    n_gate = wts["bg"].shape[0]
    head_dim = md // (n_gate // N_GATES)
    wf = wts["wf_full"] if full else wts["wf_state"]
    const = lambda shape: pl.BlockSpec(shape, lambda i, j: (0,) * len(shape))
    tile = lambda w: pl.BlockSpec((1, tm, w), lambda i, j: (i, j, 0))
    tile_t = pl.BlockSpec((1, md, tm), lambda i, j: (i, 0, j))
    feat_major = jax.ShapeDtypeStruct((b, md, t), BF16)
    in_specs = [
        tile(d),
        pl.BlockSpec((1, 6, d), (lambda i, j: (i, 0, 0)) if mod_row is None
                     else (lambda i, j: (mod_row, 0, 0))),
        const((1, d)), const(wf.shape), const(wts["wk"].shape), const(wts["bg"].shape),
    ]
    args = [h, mod, norm_g, wf, wts["wk"], wts["bg"]]
    out_specs = [tile_t, tile(md), tile_t, pl.BlockSpec((1, n_gate, tm), lambda i, j: (i, 0, j))]
    out_shape = [feat_major, jax.ShapeDtypeStruct((b, t, md), BF16), feat_major,
                 jax.ShapeDtypeStruct((b, n_gate, t), F32)]
    if full:
        cd = wts["cw"].shape[1]
        in_specs += [const(wts["wc"].shape), const(wts["cw"].shape)]
        args += [wts["wc"], wts["cw"]]
        out_specs += [tile(cd), tile_t]
        out_shape += [jax.ShapeDtypeStruct((b, t, cd), BF16), feat_major]
    return pl.pallas_call(
        functools.partial(_in_kernel, row_w=row_w, full=full, kscale=head_dim ** -0.5),
        grid=(b, t // tm),
        in_specs=in_specs,
        out_specs=out_specs,
        out_shape=out_shape,
        compiler_params=_params("parallel", "parallel"),
        name="in_proj",
    )(*args)


def _lane_scan(x, op, ident, reverse):
    n = x.shape[1]
    lane = lax.broadcasted_iota(jnp.int32, x.shape, 1)
    s = 1
    while s < n:
        if reverse:
            shifted = jnp.where(lane < n - s, pltpu.roll(x, n - s, axis=1), ident)
        else:
            shifted = jnp.where(lane >= s, pltpu.roll(x, s, axis=1), ident)
        x = op(x, shifted)
        s *= 2
    return x


def _mlstm_kernel(gc_ref, gl_ref, kc_ref, qc_ref, vc_ref, kl_ref, ql_ref, vl_ref, ogl_ref, gain_ref,
                  *rest, need_ctx):
    if need_ctx:
        ogc_ref, yc_ref, yl_ref, rt_ref, ra_ref, rp_ref, rw_ref, s_ref, upd_ref, st_ref = rest
    else:
        yl_ref, rt_ref, ra_ref, rp_ref, rw_ref, s_ref, upd_ref, st_ref = rest
        ogc_ref = yc_ref = None
    L = CHUNK
    hd = kl_ref.shape[2]
    ncc = kc_ref.shape[1] // L
    ncl = kl_ref.shape[1] // L
    nct = ncc + ncl

    head = pl.program_id(1)
    n_heads = gl_ref.shape[1] // N_GATES

    def gate_rows(gate, hh):
        r = gate * n_heads + hh
        return jnp.concatenate([ref[0, r:r + 1, c * L:(c + 1) * L]
                                for ref, n in ((gc_ref, ncc), (gl_ref, ncl)) for c in range(n)], axis=0)

    @pl.when(head == 0)
    def _():
        for d in range(2):
            rev = d == 1
            last = 0 if rev else L - 1
            for hh in range(n_heads):
                ig = gate_rows(2 * d, hh)
                lf = jax.nn.log_sigmoid(gate_rows(2 * d + 1, hh))
                a = _lane_scan(lf, jnp.add, 0.0, rev)
                r = ig - a
                p = _lane_scan(r, jnp.maximum, -jnp.inf, rev)
                ra_ref[d, hh, 0:nct, :] = a
                rp_ref[d, hh, 0:nct, :] = p
                rw_ref[d, hh, 0:nct, :] = jnp.exp(r - p[:, last:last + 1])
                r_pad = jnp.concatenate([r, jnp.zeros((L - nct, L), F32)], axis=0)
                rt_ref[d, hh] = r_pad.T

    spos = lax.broadcasted_iota(jnp.int32, (L, L), 0)
    tpos = lax.broadcasted_iota(jnp.int32, (L, L), 1)
    ones_row = jnp.where(lax.broadcasted_iota(jnp.int32, (AUG_ROWS, L), 0) == 0, 1.0, 0.0).astype(BF16)
    gain = gain_ref[...]

    ctx_seq = (kc_ref, qc_ref, vc_ref, ogc_ref, yc_ref, ncc, 0, need_ctx)
    lat_seq = (kl_ref, ql_ref, vl_ref, ogl_ref, yl_ref, ncl, ncc, True)

    for k_ref, q_ref, v_ref, _, _, n, g0, want in (ctx_seq, lat_seq):
        for j in range(n):
            g = g0 + j
            k = k_ref[0, pl.ds(j * L, L), :]
            if want:
                s_ref[g] = _dot(k, q_ref[0, :, pl.ds(j * L, L)])
            vaug = jnp.concatenate([v_ref[0, :, pl.ds(j * L, L)], ones_row], axis=0).astype(F32)
            for d in range(2):
                vw = (vaug * rw_ref[d, head, g:g + 1, :]).astype(BF16)
                upd_ref[d, g] = _dot(vw, k)

    m_in = {}
    for d in range(2):
        last = 0 if d == 1 else L - 1
        caug, m = jnp.zeros((hd + AUG_ROWS, hd), F32), jnp.zeros((1, 1), F32)
        for _, _, _, _, _, n, g0, _ in (ctx_seq, lat_seq):
            for i in range(n):
                g = g0 + (n - 1 - i if d == 1 else i)
                st_ref[d, g] = caug.astype(BF16)
                m_in[d, g] = m
                b_tot = ra_ref[d, head, g:g + 1, last:last + 1]
                p_last = rp_ref[d, head, g:g + 1, last:last + 1]
                m_last = jnp.maximum(p_last, m)
                caug = jnp.exp(m - m_last) * caug + jnp.exp(p_last - m_last) * upd_ref[d, g]
                m = b_tot + m_last

    for k_ref, q_ref, v_ref, og_ref, y_ref, n, g0, want in (ctx_seq, lat_seq):
        for j in range(n if want else 0):
            g = g0 + j
            lanes = pl.ds(j * L, L)
            q = q_ref[0, :, lanes]
            vaug = jnp.concatenate([v_ref[0, :, lanes], ones_row], axis=0)
            rhs, scale = [], []
            for d in range(2):
                a_row = ra_ref[d, head, g:g + 1, :]
                p_row = rp_ref[d, head, g:g + 1, :]
                m = m_in[d, g]
                mask = (spos >= tpos) if d == 1 else (spos <= tpos)
                w = jnp.where(mask, jnp.exp(rt_ref[d, head, :, g:g + 1] - p_row), 0.0)
                m_row = jnp.maximum(p_row, m)
                sw = (s_ref[g] * w * jnp.exp(p_row - m_row)).astype(BF16)
                qs = (q.astype(F32) * jnp.exp(m - m_row)).astype(BF16)
                rhs.append((sw, qs))
                scale.append(jnp.exp(-(a_row + m_row)))
            zero = jnp.zeros((hd, L), BF16)
            nd = _dot(jnp.concatenate([vaug, st_ref[0, g], st_ref[1, g]], axis=1),
                      jnp.concatenate([jnp.concatenate([rhs[0][0], rhs[1][0]], axis=1),
                                       jnp.concatenate([rhs[0][1], zero], axis=1),
                                       jnp.concatenate([zero, rhs[1][1]], axis=1)], axis=0))
            hh = None
            for d in range(2):
                part = nd[:, d * L:(d + 1) * L]
                den = jnp.maximum(jnp.abs(part[hd:hd + 1, :]), scale[d])
                hout = part[:hd, :] * (1.0 / den)
                hh = hout if hh is None else hh + hout
            hn = hh * lax.rsqrt(jnp.mean(hh * hh, axis=0, keepdims=True) + EPS)
            y_ref[0, :, lanes] = (hn * gain * og_ref[0, :, lanes].astype(F32)).astype(BF16)


def _mlstm(gates_c, gates_l, kqv_c, kqv_l, og_c, og_l, gain, need_ctx):
    b, tc, md = kqv_c[0].shape
    tl = kqv_l[0].shape[1]
    n_gate = gates_c.shape[1]
    heads = n_gate // N_GATES
    hd = md // heads
    L = CHUNK
    nct = (tc + tl) // L
    assert nct <= L
    nrow = -(-nct // SUBLANES) * SUBLANES
    tok = lambda t: pl.BlockSpec((1, t, hd), lambda i, h: (i, 0, h))
    feat = lambda t: pl.BlockSpec((1, hd, t), lambda i, h: (i, h, 0))
    gate_spec = lambda t: pl.BlockSpec((1, n_gate, t), lambda i, h: (i, 0, 0))
    in_specs = [gate_spec(tc), gate_spec(tl),
                tok(tc), feat(tc), feat(tc), tok(tl), feat(tl), feat(tl), feat(tl),
                pl.BlockSpec((hd, 1), lambda i, h: (h, 0))]
    args = [gates_c, gates_l, *kqv_c, *kqv_l, og_l, gain.reshape(md, 1)]
    out_specs = [feat(tl)]
    out_shape = [jax.ShapeDtypeStruct((b, md, tl), BF16)]
    if need_ctx:
        in_specs.append(feat(tc))
        args.append(og_c)
        out_specs = [feat(tc)] + out_specs
        out_shape = [jax.ShapeDtypeStruct((b, md, tc), BF16)] + out_shape
    res = pl.pallas_call(
        functools.partial(_mlstm_kernel, need_ctx=need_ctx),
        grid=(b, heads),
        in_specs=in_specs,
        out_specs=out_specs,
        out_shape=out_shape,
        scratch_shapes=[
            pltpu.VMEM((2, heads, L, L), F32),
            pltpu.VMEM((2, heads, nrow, L), F32),
            pltpu.VMEM((2, heads, nrow, L), F32),
            pltpu.VMEM((2, heads, nrow, L), F32),
            pltpu.VMEM((nct, L, L), F32),
            pltpu.VMEM((2, nct, hd + AUG_ROWS, hd), F32),
            pltpu.VMEM((2, nct, hd + AUG_ROWS, hd), BF16),
        ],
        compiler_params=_params("parallel", "arbitrary"),
        name="mlstm_scan",
    )(*args)
    return (res[0], res[1]) if need_ctx else (None, res[0])


def _mix_residual(h_ref, yc_ref, ym_ref, mod_ref, woc_ref, wom_ref, rows):
    mix = _dot(yc_ref[0, rows, :], woc_ref[...]) + _dot_tn(ym_ref[0, :, rows], wom_ref[...])
    return h_ref[0, rows, :] + mod_ref[0, 2:3, :] * mix


def _out_dense_kernel(h_ref, yc_ref, ym_ref, mod_ref, g2_ref, woc_ref, wom_ref, w1_ref, w3_ref, w2_ref,
                      *rest, chunks, final):
    if final:
        gf_ref, o_ref = rest
    else:
        (o_ref,) = rest
    h = _mix_residual(h_ref, yc_ref, ym_ref, mod_ref, woc_ref, wom_ref, slice(None))
    u = _norm_mod(h, g2_ref[...], mod_ref[0, 3:4, :], mod_ref[0, 4:5, :]).astype(BF16)
    acc = None
    for c0, c1 in chunks:
        a = _dot(u, w1_ref[:, c0:c1])
        hid = (a * jax.nn.sigmoid(a) * _dot(u, w3_ref[:, c0:c1])).astype(BF16)
        part = _dot(hid, w2_ref[c0:c1, :])
        acc = part if acc is None else acc + part
    h = h + mod_ref[0, 5:6, :] * acc
    if final:
        h = h * lax.rsqrt(jnp.mean(h * h, axis=-1, keepdims=True) + EPS) * gf_ref[...]
    o_ref[0] = h


def _out_moe_kernel(h_ref, yc_ref, ym_ref, mod_ref, g2_ref, woc_ref, wom_ref, router_ref,
                    hm_ref, u_ref, ti_ref, tg_ref, *, n_exp):
    for rows in _sub_tiles(h_ref.shape[1]):
        _out_moe_rows(h_ref, yc_ref, ym_ref, mod_ref, g2_ref, woc_ref, wom_ref, router_ref,
                      hm_ref, u_ref, ti_ref, tg_ref, rows, n_exp)


def _out_moe_rows(h_ref, yc_ref, ym_ref, mod_ref, g2_ref, woc_ref, wom_ref, router_ref,
                  hm_ref, u_ref, ti_ref, tg_ref, rows, n_exp):
    h = _mix_residual(h_ref, yc_ref, ym_ref, mod_ref, woc_ref, wom_ref, rows)
    hm_ref[0, rows, :] = h
    u = _norm_mod(h, g2_ref[...], mod_ref[0, 3:4, :], mod_ref[0, 4:5, :])
    _rows_to_tiles(u_ref, u, rows.start)
    u_hi = u.astype(BF16)
    u_lo = (u - u_hi.astype(F32)).astype(BF16)
    parts = _dot(u_hi, router_ref[...]) + _dot(u_lo, router_ref[...])
    logits = parts[:, :LANES] + parts[:, LANES:]
    n_pad = -(-n_exp // SUBLANES) * SUBLANES
    lt = logits.T[:n_pad, :]
    row = lax.broadcasted_iota(jnp.int32, lt.shape, 0)
    lt = jnp.where(row < n_exp, lt, -jnp.inf)
    v1 = jnp.max(lt, axis=0, keepdims=True)
    i1 = jnp.min(jnp.where(lt == v1, row, n_pad), axis=0, keepdims=True)
    rest = jnp.where(row == i1, -jnp.inf, lt)
    v2 = jnp.max(rest, axis=0, keepdims=True)
    i2 = jnp.min(jnp.where(rest == v2, row, n_pad), axis=0, keepdims=True)
    e2 = jnp.exp(v2 - v1)
    inv = 1.0 / (1.0 + e2)
    ti_ref[:, rows] = jnp.concatenate([i1, i2], axis=0)
    tg_ref[:, rows] = jnp.concatenate([inv, e2 * inv], axis=0)


def _out_common(h, yc, ym, mod, mod_row, norm_g, wts, row_tile):
    b, t, d = h.shape
    tm = min(row_tile, t)
    const = lambda shape: pl.BlockSpec(shape, lambda i, j: (0,) * len(shape))
    tile = lambda w: pl.BlockSpec((1, tm, w), lambda i, j: (i, j, 0))
    in_specs = [
        tile(d), tile(yc.shape[2]), pl.BlockSpec((1, ym.shape[1], tm), lambda i, j: (i, 0, j)),
        pl.BlockSpec((1, 6, d), (lambda i, j: (i, 0, 0)) if mod_row is None
                     else (lambda i, j: (mod_row, 0, 0))),
        const((1, d)), const(wts["woc"].shape), const(wts["wom"].shape),
    ]
    args = [h, yc, ym, mod, norm_g, wts["woc"], wts["wom"]]
    return b, t, d, tm, const, tile, in_specs, args


def _out_dense(h, yc, ym, mod, mod_row, norm_g, wts, ffn, final_g):
    b, t, d, tm, const, tile, in_specs, args = _out_common(h, yc, ym, mod, mod_row, norm_g, wts, FFN_ROW_TILE)
    w1, w3, w2 = ffn
    in_specs += [const(w1.shape), const(w3.shape), const(w2.shape)]
    args += [w1, w3, w2]
    if final_g is not None:
        in_specs.append(const((1, d)))
        args.append(final_g)
    return pl.pallas_call(
        functools.partial(_out_dense_kernel, chunks=_ff_chunks(w1.shape[1]), final=final_g is not None),
        grid=(b, t // tm),
        in_specs=in_specs,
        out_specs=tile(d),
        out_shape=jax.ShapeDtypeStruct((b, t, d), F32),
        compiler_params=_params("parallel", "parallel"),
        name="out_dense_ffn",
    )(*args)


def _out_moe(h, yc, ym, mod, mod_row, norm_g, wts, router_pad, n_exp):
    b, t, d, tm, const, tile, in_specs, args = _out_common(h, yc, ym, mod, mod_row, norm_g, wts, ROW_TILE)
    in_specs.append(const(router_pad.shape))
    args.append(router_pad)
    per = d // LANES
    top2 = pl.BlockSpec((TOP_K, tm), lambda i, j: (0, i * (t // tm) + j))
    return pl.pallas_call(
        functools.partial(_out_moe_kernel, n_exp=n_exp),
        grid=(b, t // tm),
        in_specs=in_specs,
        out_specs=[tile(d), pl.BlockSpec((tm * per, LANES), lambda i, j: (i * (t // tm) + j, 0)),
                   top2, top2],
        out_shape=[jax.ShapeDtypeStruct((b, t, d), F32), jax.ShapeDtypeStruct((b * t * per, LANES), F32),
                   jax.ShapeDtypeStruct((TOP_K, b * t), jnp.int32), jax.ShapeDtypeStruct((TOP_K, b * t), F32)],
        compiler_params=_params("parallel", "parallel"),
        name="out_router",
    )(*args)


def _expert_kernel(be_ref, nu_ref, tok_ref, toknext_ref, u_hbm, w1_ref, w3_ref, w2_ref, y_ref, xbuf, sem,
                   *, chunks, per):
    i = pl.program_id(0)
    n_used = nu_ref[0]
    blk = xbuf.shape[1] // per
    slot = i % 2

    def row_copy(idx_ref, r, to_slot):
        src = pl.multiple_of(idx_ref[0, 0, r], per)
        return pltpu.make_async_copy(u_hbm.at[pl.ds(src, per), :],
                                     xbuf.at[to_slot, pl.ds(r * per, per), :], sem.at[to_slot])

    @pl.when(i == 0)
    def _():
        def body(r, c):
            row_copy(tok_ref, r, 0).start()
            return c
        lax.fori_loop(0, blk, body, 0, unroll=ISSUE_UNROLL)

    @pl.when(i <= n_used)
    def _():
        pltpu.make_async_copy(u_hbm.at[pl.ds(0, blk * per), :], xbuf.at[slot], sem.at[slot]).wait()

    @pl.when(i < n_used)
    def _():
        for r in range(blk):
            row_copy(toknext_ref, r, 1 - slot).start()
        x = _tiles_to_rows(xbuf.at[slot], blk, per).astype(BF16)
        acc = None
        for c0, c1 in chunks:
            a = _dot(x, w1_ref[0, :, c0:c1])
            hid = (a * jax.nn.sigmoid(a) * _dot(x, w3_ref[0, :, c0:c1])).astype(BF16)
            part = _dot(hid, w2_ref[0, c0:c1, :])
            acc = part if acc is None else acc + part
        _rows_to_tiles(y_ref, acc)

    @pl.when(i >= n_used)
    def _():
        y_ref[...] = jnp.zeros(y_ref.shape, F32)


def _experts(u_tiles, tok_buf, blk_e, n_used, w1, w3, w2):
    d, f = w1.shape[1], w1.shape[2]
    per = d // LANES
    nblk = tok_buf.shape[0] // MOE_BLOCK
    tok3 = (tok_buf * per).reshape(nblk, 1, MOE_BLOCK)
    smem = lambda fn: pl.BlockSpec((1, 1, MOE_BLOCK), fn, memory_space=pltpu.SMEM)
    grid_spec = pltpu.PrefetchScalarGridSpec(
        num_scalar_prefetch=2,
        grid=(nblk,),
        in_specs=[
            smem(lambda i, be, nu: (i, 0, 0)),
            smem(lambda i, be, nu: (jnp.minimum(i + 1, nblk - 1), 0, 0)),
            pl.BlockSpec(memory_space=pl.ANY),
            pl.BlockSpec((1, d, f), lambda i, be, nu: (be[i], 0, 0)),
            pl.BlockSpec((1, d, f), lambda i, be, nu: (be[i], 0, 0)),
            pl.BlockSpec((1, f, d), lambda i, be, nu: (be[i], 0, 0)),
        ],
        out_specs=pl.BlockSpec((MOE_BLOCK * per, LANES), lambda i, be, nu: (i, 0)),
        scratch_shapes=[pltpu.VMEM((2, MOE_BLOCK * per, LANES), F32), pltpu.SemaphoreType.DMA((2,))],
    )
    return pl.pallas_call(
        functools.partial(_expert_kernel, chunks=_ff_chunks(f), per=per),
        grid_spec=grid_spec,
        out_shape=jax.ShapeDtypeStruct((nblk * MOE_BLOCK * per, LANES), F32),
        compiler_params=_params("arbitrary"),
        name="moe_experts",
    )(blk_e, n_used, tok3, tok3, u_tiles, w1, w3, w2)


def _combine_kernel(dst_ref, dnext_ref, h_ref, gate_ref, mod_ref, *rest, final):
    if final:
        gf_ref, y_hbm, o_ref, ybuf, sem = rest
    else:
        y_hbm, o_ref, ybuf, sem = rest
    i = pl.program_id(0)
    n = pl.num_programs(0)
    tc, d = h_ref.shape
    per = d // LANES

    def issue(idx_ref, slot):
        def body(r, c):
            for k in range(TOP_K):
                src = pl.multiple_of(idx_ref[0, 0, k * tc + r], per)
                pltpu.make_async_copy(y_hbm.at[pl.ds(src, per), :],
                                      ybuf.at[slot, k, pl.ds(pl.multiple_of(r * per, per), per), :],
                                      sem.at[slot]).start(priority=k)
            return c
        lax.fori_loop(0, tc, body, 0, unroll=ISSUE_UNROLL)

    @pl.when(i == 0)
    def _():
        issue(dst_ref, 0)

    slot = i % 2

    @pl.when(i + 1 < n)
    def _():
        issue(dnext_ref, 1 - slot)

    for k in range(TOP_K):
        pltpu.make_async_copy(y_hbm.at[pl.ds(0, tc * per), :], ybuf.at[slot, k], sem.at[slot]).wait()

    gate = jnp.concatenate([gate_ref[...], jnp.zeros((SUBLANES - TOP_K, tc), F32)], axis=0).T
    y = (gate[:, 0:1] * _tiles_to_rows(ybuf.at[slot, 0], tc, per)
         + gate[:, 1:2] * _tiles_to_rows(ybuf.at[slot, 1], tc, per))
    h = h_ref[...] + mod_ref[0, 5:6, :] * y
    if final:
        h = h * lax.rsqrt(jnp.mean(h * h, axis=-1, keepdims=True) + EPS) * gf_ref[...]
    o_ref[...] = h


def _combine(h_flat, y, dest, gates, mod, mod_row, tokens_per_batch, final_g):
    n_tok, d = h_flat.shape
    tc = COMBINE_TILE
    steps = n_tok // tc
    per_b = tokens_per_batch // tc
    per = d // LANES
    dst2 = (dest * per).reshape(TOP_K, steps, tc).transpose(1, 0, 2).reshape(steps, 1, TOP_K * tc)
    smem = lambda fn: pl.BlockSpec((1, 1, TOP_K * tc), fn, memory_space=pltpu.SMEM)
    in_specs = [
        smem(lambda i: (i, 0, 0)),
        smem(lambda i: (jnp.minimum(i + 1, steps - 1), 0, 0)),
        pl.BlockSpec((tc, d), lambda i: (i, 0)),
        pl.BlockSpec((TOP_K, tc), lambda i: (0, i)),
        pl.BlockSpec((1, 6, d), (lambda i: (i // per_b, 0, 0)) if mod_row is None
                     else (lambda i: (mod_row, 0, 0))),
    ]
    args = [dst2, dst2, h_flat, gates, mod]
    if final_g is not None:
        in_specs.append(pl.BlockSpec((1, d), lambda i: (0, 0)))
        args.append(final_g)
    in_specs.append(pl.BlockSpec(memory_space=pl.ANY))
    args.append(y)
    return pl.pallas_call(
        functools.partial(_combine_kernel, final=final_g is not None),
        grid=(steps,),
        in_specs=in_specs,
        out_specs=pl.BlockSpec((tc, d), lambda i: (i, 0)),
        out_shape=jax.ShapeDtypeStruct((n_tok, d), F32),
        scratch_shapes=[pltpu.VMEM((2, TOP_K, tc * per, LANES), F32), pltpu.SemaphoreType.DMA((2,))],
        compiler_params=_params("arbitrary"),
        name="moe_combine",
    )(*args)


def _routing(top_i, n_exp):
    n_tok = top_i.shape[1]
    e = top_i.reshape(-1)
    a = e.shape[0]
    experts = jnp.arange(n_exp, dtype=jnp.int32)
    onehot = (e[:, None] == experts[None, :]).astype(jnp.int32)
    csum = jnp.cumsum(onehot, axis=0)
    rank = jnp.sum(onehot * csum, axis=1) - 1
    counts = csum[-1]
    padded = (counts + MOE_BLOCK - 1) // MOE_BLOCK * MOE_BLOCK
    pend = jnp.cumsum(padded)
    pstart = pend - padded
    start = jnp.cumsum(counts) - counts
    dest = pstart[e] + rank
    cap = -(-a // MOE_BLOCK) * MOE_BLOCK + n_exp * MOE_BLOCK
    nblk = cap // MOE_BLOCK
    blk_first = jnp.arange(nblk, dtype=jnp.int32) * MOE_BLOCK
    blk_e = jnp.minimum(jnp.sum((pend[None, :] <= blk_first[:, None]).astype(jnp.int32), axis=1), n_exp - 1)
    n_used = (pend[-1] // MOE_BLOCK).astype(jnp.int32).reshape(1)
    order = jnp.argsort(e).astype(jnp.int32)
    row = jnp.arange(cap, dtype=jnp.int32)
    row_e = jnp.repeat(blk_e, MOE_BLOCK)
    within = row - pstart[row_e]
    valid = (within < counts[row_e]) & (row < pend[-1])
    tok_buf = jnp.where(valid, order[jnp.clip(start[row_e] + within, 0, a - 1)] % n_tok, 0)
    return dest.astype(jnp.int32).reshape(TOP_K, n_tok), tok_buf.astype(jnp.int32), blk_e.astype(jnp.int32), n_used


def _moe_mixer(h_mid, u, top_i, top_g, mod, mod_row, moe_w, n_exp, final_g):
    b, t, d = h_mid.shape
    n_tok = b * t
    dest, tok_buf, blk_e, n_used = _routing(top_i, n_exp)
    y = _experts(u, tok_buf, blk_e, n_used, *moe_w)
    out = _combine(h_mid.reshape(n_tok, d), y, dest, top_g, mod, mod_row, t, final_g)
    return out.reshape(b, t, d)


def _layer_weights(w_in_l, conv_w_l, b_gates_l, w_out_l):
    cd = conv_w_l.shape[1]
    n_gate = b_gates_l.size
    md = (w_in_l.shape[1] - 3 * cd - n_gate) // 4
    q0 = 3 * cd
    g0 = q0 + 3 * md
    wq, wv = w_in_l[:, q0:q0 + md], w_in_l[:, q0 + 2 * md:g0]
    wg, wo = w_in_l[:, g0:g0 + n_gate], w_in_l[:, g0 + n_gate:]
    return {
        "wc": w_in_l[:, :q0].astype(BF16),
        "wf_full": jnp.concatenate([wq, wv, wo, wg], axis=1).T.astype(BF16),
        "wf_state": jnp.concatenate([wq, wv, wg], axis=1).T.astype(BF16),
        "wk": w_in_l[:, q0 + md:q0 + 2 * md].astype(BF16),
        "bg": b_gates_l.reshape(n_gate, 1),
        "cw": conv_w_l,
        "woc": w_out_l[:cd].astype(BF16),
        "wom": w_out_l[cd:].astype(BF16),
    }


def kernel(x, c, ctx, c_ctx, norm1_g, norm2_g, w_ada, b_ada, w_in, conv_w, b_gates, mlstm_norm_g, w_out,
           ffn_w1, ffn_w3, ffn_w2, moe_router, moe_w1, moe_w3, moe_w2, final_norm_g):
    b, t, d = x.shape
    tc = ctx.shape[1]
    depth = w_in.shape[0]
    n_exp = moe_router.shape[-1]
    assert t % GRID_W == 0 and t % CHUNK == 0 and tc % CHUNK == 0 and d % (SUBLANES * LANES) == 0

    ctx_row = b
    n_rows = -(-(b + 1) // SUBLANES) * SUBLANES
    c_all = jnp.concatenate([c, c_ctx[None, :], jnp.zeros((n_rows - b - 1, d), F32)], axis=0)
    mod_all = _ada(c_all, w_ada, b_ada).reshape(depth, n_rows, 6, d)
    final_g = final_norm_g.reshape(1, d)

    h, hc = x, ctx
    for layer in range(depth):
        last = layer == depth - 1
        j = layer // 2
        mod = mod_all[layer]
        wts = _layer_weights(w_in[layer], conv_w[layer], b_gates[layer], w_out[layer])
        n1 = norm1_g[layer].reshape(1, d)
        n2 = norm2_g[layer].reshape(1, d)

        ql, kl, vl, gl, ycl, ogl = _in_proj(h, mod, None, n1, wts, GRID_W, True)
        res_c = _in_proj(hc, mod, ctx_row, n1, wts, tc, not last)
        qc, kc, vc, gc = res_c[:4]
        ogc = None if last else res_c[5]
        ymc, yml = _mlstm(gc, gl, (kc, qc, vc), (kl, ql, vl), ogc, ogl,
                          mlstm_norm_g[layer].reshape(1, -1), not last)

        def mixer(hh, yc_, ym_, mod_row, fin):
            if layer % 2 == 0:
                ffn = (ffn_w1[j].astype(BF16), ffn_w3[j].astype(BF16), ffn_w2[j].astype(BF16))
                return _out_dense(hh, yc_, ym_, mod, mod_row, n2, wts, ffn, fin)
            r_full = jnp.pad(moe_router[j], ((0, 0), (0, LANES - n_exp)))
            r_hi = r_full.astype(BF16)
            r_lo = (r_full - r_hi.astype(F32)).astype(BF16)
            router_pad = jnp.concatenate([r_hi, r_lo], axis=1)
            h_mid, u2, top_i, top_g = _out_moe(hh, yc_, ym_, mod, mod_row, n2, wts, router_pad, n_exp)
            moe_w = (moe_w1[j].astype(BF16), moe_w3[j].astype(BF16), moe_w2[j].astype(BF16))
            return _moe_mixer(h_mid, u2, top_i, top_g, mod, mod_row, moe_w, n_exp, fin)

        h = mixer(h, ycl, yml, None, final_g if last else None)
        if not last:
            hc = mixer(hc, res_c[4], ymc, ctx_row, None)
    return h
```

```python
import functools

import jax
import jax.numpy as jnp
from jax import lax
from jax.experimental import pallas as pl
from jax.experimental.pallas import tpu as pltpu

GRID_W = 64
CHUNK = 128
N_GATES = 4
EPS = 1e-6
TOP_K = 2
ROW_TILE = 1024
FFN_ROW_TILE = 512
MOE_BLOCK = 512
COMBINE_TILE = 512
SUB_TILE = 256
FF_CHUNK = 1024
ADA_COL_TILES = 4
ISSUE_UNROLL = 8
LANES = 128
SUBLANES = 8
AUG_ROWS = 32
VMEM_LIMIT = 56 * 1024 * 1024

F32 = jnp.float32
BF16 = jnp.bfloat16
HIGHEST = lax.Precision.HIGHEST


def _dot(a, b):
    return jnp.dot(a, b, preferred_element_type=F32)


def _dot_nt(a, b):
    return lax.dot_general(a, b, (((1,), (1,)), ((), ())), preferred_element_type=F32)


def _dot_tn(a, b):
    return lax.dot_general(a, b, (((0,), (0,)), ((), ())), preferred_element_type=F32)


def _rows_to_tiles(ref, x, row0=0):
    n, d = x.shape
    per = d // LANES
    for s in range(per):
        ref[pl.ds(row0 * per + s, n, stride=per), :] = x[:, s * LANES:(s + 1) * LANES]


def _tiles_to_rows(ref, n, per):
    return jnp.concatenate([ref[pl.ds(s, n, stride=per), :] for s in range(per)], axis=1)


def _sub_tiles(tm):
    sub = min(SUB_TILE, tm)
    return [slice(r0, r0 + sub) for r0 in range(0, tm, sub)]


def _params(*sem):
    return pltpu.CompilerParams(dimension_semantics=sem, vmem_limit_bytes=VMEM_LIMIT)


def _ff_chunks(d_ff):
    out, c0 = [], 0
    while c0 < d_ff:
        c1 = min(c0 + FF_CHUNK, d_ff)
        out.append((c0, c1))
        c0 = c1
    return out


def _ada_kernel(c_ref, w_ref, b_ref, o_ref):
    s = c_ref[...]
    s = s * jax.nn.sigmoid(s)
    o_ref[0] = jnp.dot(s, w_ref[0], precision=HIGHEST, preferred_element_type=F32) + b_ref[0]


def _ada(c_all, w_ada, b_ada):
    depth, d, d6 = w_ada.shape
    rows = c_all.shape[0]
    tn = d6 // ADA_COL_TILES
    return pl.pallas_call(
        _ada_kernel,
        grid=(depth, d6 // tn),
        in_specs=[
            pl.BlockSpec((rows, d), lambda l, j: (0, 0)),
            pl.BlockSpec((1, d, tn), lambda l, j: (l, 0, j)),
            pl.BlockSpec((1, 1, tn), lambda l, j: (l, 0, j)),
        ],
        out_specs=pl.BlockSpec((1, rows, tn), lambda l, j: (l, 0, j)),
        out_shape=jax.ShapeDtypeStruct((depth, rows, d6), F32),
        compiler_params=_params("parallel", "parallel"),
        name="adaln_mod",
    )(c_all, w_ada, b_ada.reshape(depth, 1, d6))


def _norm_mod(x, g, shift, scale):
    y = x * lax.rsqrt(jnp.mean(x * x, axis=-1, keepdims=True) + EPS)
    return (y * g) * (1.0 + scale) + shift


def _in_kernel(x_ref, mod_ref, g_ref, wf_ref, wk_ref, bg_ref, *rest, row_w, full, kscale):
    if full:
        wc_ref, cw_ref, q_ref, k_ref, v_ref, gt_ref, yc_ref, og_ref = rest
    else:
        q_ref, k_ref, v_ref, gt_ref = rest
    md = q_ref.shape[1]
    for rows in _sub_tiles(x_ref.shape[1]):
        u = _norm_mod(x_ref[0, rows, :], g_ref[...], mod_ref[0, 0:1, :], mod_ref[0, 1:2, :]).astype(BF16)
        ft = _dot_nt(wf_ref[...], u)
        q_ref[0, :, rows] = ft[:md].astype(BF16)
        k_ref[0, rows, :] = (_dot(u, wk_ref[...]) * kscale).astype(BF16)
        v_ref[0, :, rows] = ft[md:2 * md].astype(BF16)
        gt_ref[0, :, rows] = ft[-bg_ref.shape[0]:] + bg_ref[...]
        if full:
            cd = cw_ref.shape[1]
            c3 = _dot(u, wc_ref[...])
            z = c3[:, cd:2 * cd] * c3[:, 2 * cd:]
            n = z.shape[0]
            t = lax.broadcasted_iota(jnp.int32, (n, 1), 0) % row_w
            zprev = jnp.where(t == 0, 0.0, pltpu.roll(z, 1, axis=0))
            znext = jnp.where(t == row_w - 1, 0.0, pltpu.roll(z, n - 1, axis=0))
            conv = zprev * cw_ref[0:1, :] + z * cw_ref[1:2, :] + znext * cw_ref[2:3, :]
            yc_ref[0, rows, :] = (c3[:, :cd] * conv).astype(BF16)
            og_ref[0, :, rows] = jax.nn.sigmoid(ft[2 * md:3 * md]).astype(BF16)


def _in_proj(h, mod, mod_row, norm_g, wts, row_w, full):
    b, t, d = h.shape
    tm = min(ROW_TILE, t)
    md = wts["wk"].shape[1]
    n_gate = wts["bg"].shape[0]
    head_dim = md // (n_gate // N_GATES)
    wf = wts["wf_full"] if full else wts["wf_state"]
    const = lambda shape: pl.BlockSpec(shape, lambda i, j: (0,) * len(shape))
    tile = lambda w: pl.BlockSpec((1, tm, w), lambda i, j: (i, j, 0))
    tile_t = pl.BlockSpec((1, md, tm), lambda i, j: (i, 0, j))
    feat_major = jax.ShapeDtypeStruct((b, md, t), BF16)
    in_specs = [
        tile(d),
        pl.BlockSpec((1, 6, d), (lambda i, j: (i, 0, 0)) if mod_row is None
                     else (lambda i, j: (mod_row, 0, 0))),
        const((1, d)), const(wf.shape), const(wts["wk"].shape), const(wts["bg"].shape),
    ]
    args = [h, mod, norm_g, wf, wts["wk"], wts["bg"]]
    out_specs = [tile_t, tile(md), tile_t, pl.BlockSpec((1, n_gate, tm), lambda i, j: (i, 0, j))]
    out_shape = [feat_major, jax.ShapeDtypeStruct((b, t, md), BF16), feat_major,
                 jax.ShapeDtypeStruct((b, n_gate, t), F32)]
    if full:
        cd = wts["cw"].shape[1]
        in_specs += [const(wts["wc"].shape), const(wts["cw"].shape)]
        args += [wts["wc"], wts["cw"]]
        out_specs += [tile(cd), tile_t]
        out_shape += [jax.ShapeDtypeStruct((b, t, cd), BF16), feat_major]
    return pl.pallas_call(
        functools.partial(_in_kernel, row_w=row_w, full=full, kscale=head_dim ** -0.5),
        grid=(b, t // tm),
        in_specs=in_specs,
        out_specs=out_specs,
        out_shape=out_shape,
        compiler_params=_params("parallel", "parallel"),
        name="in_proj",
    )(*args)


def _lane_scan(x, op, ident, reverse):
    n = x.shape[1]
    lane = lax.broadcasted_iota(jnp.int32, x.shape, 1)
    s = 1
    while s < n:
        if reverse:
            shifted = jnp.where(lane < n - s, pltpu.roll(x, n - s, axis=1), ident)
        else:
            shifted = jnp.where(lane >= s, pltpu.roll(x, s, axis=1), ident)
        x = op(x, shifted)
        s *= 2
    return x


def _mlstm_kernel(gc_ref, gl_ref, kc_ref, qc_ref, vc_ref, kl_ref, ql_ref, vl_ref, ogl_ref, gain_ref,
                  *rest, need_ctx):
    if need_ctx:
        ogc_ref, yc_ref, yl_ref, rt_ref, ra_ref, rp_ref, rw_ref, s_ref, upd_ref, st_ref = rest
    else:
        yl_ref, rt_ref, ra_ref, rp_ref, rw_ref, s_ref, upd_ref, st_ref = rest
        ogc_ref = yc_ref = None
    L = CHUNK
    hd = kl_ref.shape[2]
    ncc = kc_ref.shape[1] // L
    ncl = kl_ref.shape[1] // L
    nct = ncc + ncl

    head = pl.program_id(1)
    n_heads = gl_ref.shape[1] // N_GATES

    def gate_rows(gate, hh):
        r = gate * n_heads + hh
        return jnp.concatenate([ref[0, r:r + 1, c * L:(c + 1) * L]
                                for ref, n in ((gc_ref, ncc), (gl_ref, ncl)) for c in range(n)], axis=0)

    @pl.when(head == 0)
    def _():
        for d in range(2):
            rev = d == 1
            last = 0 if rev else L - 1
            for hh in range(n_heads):
                ig = gate_rows(2 * d, hh)
                lf = jax.nn.log_sigmoid(gate_rows(2 * d + 1, hh))
                a = _lane_scan(lf, jnp.add, 0.0, rev)
                r = ig - a
                p = _lane_scan(r, jnp.maximum, -jnp.inf, rev)
                ra_ref[d, hh, 0:nct, :] = a
                rp_ref[d, hh, 0:nct, :] = p
                rw_ref[d, hh, 0:nct, :] = jnp.exp(r - p[:, last:last + 1])
                r_pad = jnp.concatenate([r, jnp.zeros((L - nct, L), F32)], axis=0)
                rt_ref[d, hh] = r_pad.T

    spos = lax.broadcasted_iota(jnp.int32, (L, L), 0)
    tpos = lax.broadcasted_iota(jnp.int32, (L, L), 1)
    ones_row = jnp.where(lax.broadcasted_iota(jnp.int32, (AUG_ROWS, L), 0) == 0, 1.0, 0.0).astype(BF16)
    gain = gain_ref[...]

    ctx_seq = (kc_ref, qc_ref, vc_ref, ogc_ref, yc_ref, ncc, 0, need_ctx)
    lat_seq = (kl_ref, ql_ref, vl_ref, ogl_ref, yl_ref, ncl, ncc, True)

    for k_ref, q_ref, v_ref, _, _, n, g0, want in (ctx_seq, lat_seq):
        for j in range(n):
            g = g0 + j
            k = k_ref[0, pl.ds(j * L, L), :]
            if want:
                s_ref[g] = _dot(k, q_ref[0, :, pl.ds(j * L, L)])
            vaug = jnp.concatenate([v_ref[0, :, pl.ds(j * L, L)], ones_row], axis=0).astype(F32)
            for d in range(2):
                vw = (vaug * rw_ref[d, head, g:g + 1, :]).astype(BF16)
                upd_ref[d, g] = _dot(vw, k)

    m_in = {}
    for d in range(2):
        last = 0 if d == 1 else L - 1
        caug, m = jnp.zeros((hd + AUG_ROWS, hd), F32), jnp.zeros((1, 1), F32)
        for _, _, _, _, _, n, g0, _ in (ctx_seq, lat_seq):
            for i in range(n):
                g = g0 + (n - 1 - i if d == 1 else i)
                st_ref[d, g] = caug.astype(BF16)
                m_in[d, g] = m
                b_tot = ra_ref[d, head, g:g + 1, last:last + 1]
                p_last = rp_ref[d, head, g:g + 1, last:last + 1]
                m_last = jnp.maximum(p_last, m)
                caug = jnp.exp(m - m_last) * caug + jnp.exp(p_last - m_last) * upd_ref[d, g]
                m = b_tot + m_last

    for k_ref, q_ref, v_ref, og_ref, y_ref, n, g0, want in (ctx_seq, lat_seq):
        for j in range(n if want else 0):
            g = g0 + j
            lanes = pl.ds(j * L, L)
            q = q_ref[0, :, lanes]
            vaug = jnp.concatenate([v_ref[0, :, lanes], ones_row], axis=0)
            rhs, scale = [], []
            for d in range(2):
                a_row = ra_ref[d, head, g:g + 1, :]
                p_row = rp_ref[d, head, g:g + 1, :]
                m = m_in[d, g]
                mask = (spos >= tpos) if d == 1 else (spos <= tpos)
                w = jnp.where(mask, jnp.exp(rt_ref[d, head, :, g:g + 1] - p_row), 0.0)
                m_row = jnp.maximum(p_row, m)
                sw = (s_ref[g] * w * jnp.exp(p_row - m_row)).astype(BF16)
                qs = (q.astype(F32) * jnp.exp(m - m_row)).astype(BF16)
                rhs.append((sw, qs))
                scale.append(jnp.exp(-(a_row + m_row)))
            zero = jnp.zeros((hd, L), BF16)
            nd = _dot(jnp.concatenate([vaug, st_ref[0, g], st_ref[1, g]], axis=1),
                      jnp.concatenate([jnp.concatenate([rhs[0][0], rhs[1][0]], axis=1),
                                       jnp.concatenate([rhs[0][1], zero], axis=1),
                                       jnp.concatenate([zero, rhs[1][1]], axis=1)], axis=0))
            hh = None
            for d in range(2):
                part = nd[:, d * L:(d + 1) * L]
                den = jnp.maximum(jnp.abs(part[hd:hd + 1, :]), scale[d])
                hout = part[:hd, :] * (1.0 / den)
                hh = hout if hh is None else hh + hout
            hn = hh * lax.rsqrt(jnp.mean(hh * hh, axis=0, keepdims=True) + EPS)
            y_ref[0, :, lanes] = (hn * gain * og_ref[0, :, lanes].astype(F32)).astype(BF16)


def _mlstm(gates_c, gates_l, kqv_c, kqv_l, og_c, og_l, gain, need_ctx):
    b, tc, md = kqv_c[0].shape
    tl = kqv_l[0].shape[1]
    n_gate = gates_c.shape[1]
    heads = n_gate // N_GATES
    hd = md // heads
    L = CHUNK
    nct = (tc + tl) // L
    assert nct <= L
    nrow = -(-nct // SUBLANES) * SUBLANES
    tok = lambda t: pl.BlockSpec((1, t, hd), lambda i, h: (i, 0, h))
    feat = lambda t: pl.BlockSpec((1, hd, t), lambda i, h: (i, h, 0))
    gate_spec = lambda t: pl.BlockSpec((1, n_gate, t), lambda i, h: (i, 0, 0))
    in_specs = [gate_spec(tc), gate_spec(tl),
                tok(tc), feat(tc), feat(tc), tok(tl), feat(tl), feat(tl), feat(tl),
                pl.BlockSpec((hd, 1), lambda i, h: (h, 0))]
    args = [gates_c, gates_l, *kqv_c, *kqv_l, og_l, gain.reshape(md, 1)]
    out_specs = [feat(tl)]
    out_shape = [jax.ShapeDtypeStruct((b, md, tl), BF16)]
    if need_ctx:
        in_specs.append(feat(tc))
        args.append(og_c)
        out_specs = [feat(tc)] + out_specs
        out_shape = [jax.ShapeDtypeStruct((b, md, tc), BF16)] + out_shape
    res = pl.pallas_call(
        functools.partial(_mlstm_kernel, need_ctx=need_ctx),
        grid=(b, heads),
        in_specs=in_specs,
        out_specs=out_specs,
        out_shape=out_shape,
        scratch_shapes=[
            pltpu.VMEM((2, heads, L, L), F32),
            pltpu.VMEM((2, heads, nrow, L), F32),
            pltpu.VMEM((2, heads, nrow, L), F32),
            pltpu.VMEM((2, heads, nrow, L), F32),
            pltpu.VMEM((nct, L, L), F32),
            pltpu.VMEM((2, nct, hd + AUG_ROWS, hd), F32),
            pltpu.VMEM((2, nct, hd + AUG_ROWS, hd), BF16),
        ],
        compiler_params=_params("parallel", "arbitrary"),
        name="mlstm_scan",
    )(*args)
    return (res[0], res[1]) if need_ctx else (None, res[0])


def _mix_residual(h_ref, yc_ref, ym_ref, mod_ref, woc_ref, wom_ref, rows):
    mix = _dot(yc_ref[0, rows, :], woc_ref[...]) + _dot_tn(ym_ref[0, :, rows], wom_ref[...])
    return h_ref[0, rows, :] + mod_ref[0, 2:3, :] * mix


def _out_dense_kernel(h_ref, yc_ref, ym_ref, mod_ref, g2_ref, woc_ref, wom_ref, w1_ref, w3_ref, w2_ref,
                      *rest, chunks, final):
    if final:
        gf_ref, o_ref = rest
    else:
        (o_ref,) = rest
    h = _mix_residual(h_ref, yc_ref, ym_ref, mod_ref, woc_ref, wom_ref, slice(None))
    u = _norm_mod(h, g2_ref[...], mod_ref[0, 3:4, :], mod_ref[0, 4:5, :]).astype(BF16)
    acc = None
    for c0, c1 in chunks:
        a = _dot(u, w1_ref[:, c0:c1])
        hid = (a * jax.nn.sigmoid(a) * _dot(u, w3_ref[:, c0:c1])).astype(BF16)
        part = _dot(hid, w2_ref[c0:c1, :])
        acc = part if acc is None else acc + part
    h = h + mod_ref[0, 5:6, :] * acc
    if final:
        h = h * lax.rsqrt(jnp.mean(h * h, axis=-1, keepdims=True) + EPS) * gf_ref[...]
    o_ref[0] = h


def _out_moe_kernel(h_ref, yc_ref, ym_ref, mod_ref, g2_ref, woc_ref, wom_ref, router_ref,
                    hm_ref, u_ref, ti_ref, tg_ref, *, n_exp):
    for rows in _sub_tiles(h_ref.shape[1]):
        _out_moe_rows(h_ref, yc_ref, ym_ref, mod_ref, g2_ref, woc_ref, wom_ref, router_ref,
                      hm_ref, u_ref, ti_ref, tg_ref, rows, n_exp)


def _out_moe_rows(h_ref, yc_ref, ym_ref, mod_ref, g2_ref, woc_ref, wom_ref, router_ref,
                  hm_ref, u_ref, ti_ref, tg_ref, rows, n_exp):
    h = _mix_residual(h_ref, yc_ref, ym_ref, mod_ref, woc_ref, wom_ref, rows)
    hm_ref[0, rows, :] = h
    u = _norm_mod(h, g2_ref[...], mod_ref[0, 3:4, :], mod_ref[0, 4:5, :])
    _rows_to_tiles(u_ref, u, rows.start)
    u_hi = u.astype(BF16)
    u_lo = (u - u_hi.astype(F32)).astype(BF16)
    parts = _dot(u_hi, router_ref[...]) + _dot(u_lo, router_ref[...])
    logits = parts[:, :LANES] + parts[:, LANES:]
    n_pad = -(-n_exp // SUBLANES) * SUBLANES
    lt = logits.T[:n_pad, :]
    row = lax.broadcasted_iota(jnp.int32, lt.shape, 0)
    lt = jnp.where(row < n_exp, lt, -jnp.inf)
    v1 = jnp.max(lt, axis=0, keepdims=True)
    i1 = jnp.min(jnp.where(lt == v1, row, n_pad), axis=0, keepdims=True)
    rest = jnp.where(row == i1, -jnp.inf, lt)
    v2 = jnp.max(rest, axis=0, keepdims=True)
    i2 = jnp.min(jnp.where(rest == v2, row, n_pad), axis=0, keepdims=True)
    e2 = jnp.exp(v2 - v1)
    inv = 1.0 / (1.0 + e2)
    ti_ref[:, rows] = jnp.concatenate([i1, i2], axis=0)
    tg_ref[:, rows] = jnp.concatenate([inv, e2 * inv], axis=0)


def _out_common(h, yc, ym, mod, mod_row, norm_g, wts, row_tile):
    b, t, d = h.shape
    tm = min(row_tile, t)
    const = lambda shape: pl.BlockSpec(shape, lambda i, j: (0,) * len(shape))
    tile = lambda w: pl.BlockSpec((1, tm, w), lambda i, j: (i, j, 0))
    in_specs = [
        tile(d), tile(yc.shape[2]), pl.BlockSpec((1, ym.shape[1], tm), lambda i, j: (i, 0, j)),
        pl.BlockSpec((1, 6, d), (lambda i, j: (i, 0, 0)) if mod_row is None
                     else (lambda i, j: (mod_row, 0, 0))),
        const((1, d)), const(wts["woc"].shape), const(wts["wom"].shape),
    ]
    args = [h, yc, ym, mod, norm_g, wts["woc"], wts["wom"]]
    return b, t, d, tm, const, tile, in_specs, args


def _out_dense(h, yc, ym, mod, mod_row, norm_g, wts, ffn, final_g):
    b, t, d, tm, const, tile, in_specs, args = _out_common(h, yc, ym, mod, mod_row, norm_g, wts, FFN_ROW_TILE)
    w1, w3, w2 = ffn
    in_specs += [const(w1.shape), const(w3.shape), const(w2.shape)]
    args += [w1, w3, w2]
    if final_g is not None:
        in_specs.append(const((1, d)))
        args.append(final_g)
    return pl.pallas_call(
        functools.partial(_out_dense_kernel, chunks=_ff_chunks(w1.shape[1]), final=final_g is not None),
        grid=(b, t // tm),
        in_specs=in_specs,
        out_specs=tile(d),
        out_shape=jax.ShapeDtypeStruct((b, t, d), F32),
        compiler_params=_params("parallel", "parallel"),
        name="out_dense_ffn",
    )(*args)


def _out_moe(h, yc, ym, mod, mod_row, norm_g, wts, router_pad, n_exp):
    b, t, d, tm, const, tile, in_specs, args = _out_common(h, yc, ym, mod, mod_row, norm_g, wts, ROW_TILE)
    in_specs.append(const(router_pad.shape))
    args.append(router_pad)
    per = d // LANES
    top2 = pl.BlockSpec((TOP_K, tm), lambda i, j: (0, i * (t // tm) + j))
    return pl.pallas_call(
        functools.partial(_out_moe_kernel, n_exp=n_exp),
        grid=(b, t // tm),
        in_specs=in_specs,
        out_specs=[tile(d), pl.BlockSpec((tm * per, LANES), lambda i, j: (i * (t // tm) + j, 0)),
                   top2, top2],
        out_shape=[jax.ShapeDtypeStruct((b, t, d), F32), jax.ShapeDtypeStruct((b * t * per, LANES), F32),
                   jax.ShapeDtypeStruct((TOP_K, b * t), jnp.int32), jax.ShapeDtypeStruct((TOP_K, b * t), F32)],
        compiler_params=_params("parallel", "parallel"),
        name="out_router",
    )(*args)


def _expert_kernel(be_ref, nu_ref, tok_ref, toknext_ref, u_hbm, w1_ref, w3_ref, w2_ref, y_ref, xbuf, sem,
                   *, chunks, per):
    i = pl.program_id(0)
    n_used = nu_ref[0]
    blk = xbuf.shape[1] // per
    slot = i % 2

    def row_copy(idx_ref, r, to_slot):
        src = pl.multiple_of(idx_ref[0, 0, r], per)
        return pltpu.make_async_copy(u_hbm.at[pl.ds(src, per), :],
                                     xbuf.at[to_slot, pl.ds(r * per, per), :], sem.at[to_slot])

    @pl.when(i == 0)
    def _():
        def body(r, c):
            row_copy(tok_ref, r, 0).start()
            return c
        lax.fori_loop(0, blk, body, 0, unroll=ISSUE_UNROLL)

    @pl.when(i <= n_used)
    def _():
        pltpu.make_async_copy(u_hbm.at[pl.ds(0, blk * per), :], xbuf.at[slot], sem.at[slot]).wait()

    @pl.when(i < n_used)
    def _():
        for r in range(blk):
            row_copy(toknext_ref, r, 1 - slot).start()
        x = _tiles_to_rows(xbuf.at[slot], blk, per).astype(BF16)
        acc = None
        for c0, c1 in chunks:
            a = _dot(x, w1_ref[0, :, c0:c1])
            hid = (a * jax.nn.sigmoid(a) * _dot(x, w3_ref[0, :, c0:c1])).astype(BF16)
            part = _dot(hid, w2_ref[0, c0:c1, :])
            acc = part if acc is None else acc + part
        _rows_to_tiles(y_ref, acc)

    @pl.when(i >= n_used)
    def _():
        y_ref[...] = jnp.zeros(y_ref.shape, F32)


def _experts(u_tiles, tok_buf, blk_e, n_used, w1, w3, w2):
    d, f = w1.shape[1], w1.shape[2]
    per = d // LANES
    nblk = tok_buf.shape[0] // MOE_BLOCK
    tok3 = (tok_buf * per).reshape(nblk, 1, MOE_BLOCK)
    smem = lambda fn: pl.BlockSpec((1, 1, MOE_BLOCK), fn, memory_space=pltpu.SMEM)
    grid_spec = pltpu.PrefetchScalarGridSpec(
        num_scalar_prefetch=2,
        grid=(nblk,),
        in_specs=[
            smem(lambda i, be, nu: (i, 0, 0)),
            smem(lambda i, be, nu: (jnp.minimum(i + 1, nblk - 1), 0, 0)),
            pl.BlockSpec(memory_space=pl.ANY),
            pl.BlockSpec((1, d, f), lambda i, be, nu: (be[i], 0, 0)),
            pl.BlockSpec((1, d, f), lambda i, be, nu: (be[i], 0, 0)),
            pl.BlockSpec((1, f, d), lambda i, be, nu: (be[i], 0, 0)),
        ],
        out_specs=pl.BlockSpec((MOE_BLOCK * per, LANES), lambda i, be, nu: (i, 0)),
        scratch_shapes=[pltpu.VMEM((2, MOE_BLOCK * per, LANES), F32), pltpu.SemaphoreType.DMA((2,))],
    )
    return pl.pallas_call(
        functools.partial(_expert_kernel, chunks=_ff_chunks(f), per=per),
        grid_spec=grid_spec,
        out_shape=jax.ShapeDtypeStruct((nblk * MOE_BLOCK * per, LANES), F32),
        compiler_params=_params("arbitrary"),
        name="moe_experts",
    )(blk_e, n_used, tok3, tok3, u_tiles, w1, w3, w2)


def _combine_kernel(dst_ref, dnext_ref, h_ref, gate_ref, mod_ref, *rest, final):
    if final:
        gf_ref, y_hbm, o_ref, ybuf, sem = rest
    else:
        y_hbm, o_ref, ybuf, sem = rest
    i = pl.program_id(0)
    n = pl.num_programs(0)
    tc, d = h_ref.shape
    per = d // LANES

    def issue(idx_ref, slot):
        for r in range(tc):
            for k in range(TOP_K):
                src = pl.multiple_of(idx_ref[0, 0, k * tc + r], per)
                pltpu.make_async_copy(y_hbm.at[pl.ds(src, per), :],
                                      ybuf.at[slot, k, pl.ds(r * per, per), :],
                                      sem.at[slot]).start(priority=k)

    @pl.when(i == 0)
    def _():
        issue(dst_ref, 0)

    slot = i % 2

    @pl.when(i + 1 < n)
    def _():
        issue(dnext_ref, 1 - slot)

    for k in range(TOP_K):
        pltpu.make_async_copy(y_hbm.at[pl.ds(0, tc * per), :], ybuf.at[slot, k], sem.at[slot]).wait()

    gate = jnp.concatenate([gate_ref[...], jnp.zeros((SUBLANES - TOP_K, tc), F32)], axis=0).T
    y = (gate[:, 0:1] * _tiles_to_rows(ybuf.at[slot, 0], tc, per)
         + gate[:, 1:2] * _tiles_to_rows(ybuf.at[slot, 1], tc, per))
    h = h_ref[...] + mod_ref[0, 5:6, :] * y
    if final:
        h = h * lax.rsqrt(jnp.mean(h * h, axis=-1, keepdims=True) + EPS) * gf_ref[...]
    o_ref[...] = h


def _combine(h_flat, y, dest, gates, mod, mod_row, tokens_per_batch, final_g):
    n_tok, d = h_flat.shape
    tc = COMBINE_TILE
    steps = n_tok // tc
    per_b = tokens_per_batch // tc
    per = d // LANES
    dst2 = (dest * per).reshape(TOP_K, steps, tc).transpose(1, 0, 2).reshape(steps, 1, TOP_K * tc)
    smem = lambda fn: pl.BlockSpec((1, 1, TOP_K * tc), fn, memory_space=pltpu.SMEM)
    in_specs = [
        smem(lambda i: (i, 0, 0)),
        smem(lambda i: (jnp.minimum(i + 1, steps - 1), 0, 0)),
        pl.BlockSpec((tc, d), lambda i: (i, 0)),
        pl.BlockSpec((TOP_K, tc), lambda i: (0, i)),
        pl.BlockSpec((1, 6, d), (lambda i: (i // per_b, 0, 0)) if mod_row is None
                     else (lambda i: (mod_row, 0, 0))),
    ]
    args = [dst2, dst2, h_flat, gates, mod]
    if final_g is not None:
        in_specs.append(pl.BlockSpec((1, d), lambda i: (0, 0)))
        args.append(final_g)
    in_specs.append(pl.BlockSpec(memory_space=pl.ANY))
    args.append(y)
    return pl.pallas_call(
        functools.partial(_combine_kernel, final=final_g is not None),
        grid=(steps,),
        in_specs=in_specs,
        out_specs=pl.BlockSpec((tc, d), lambda i: (i, 0)),
        out_shape=jax.ShapeDtypeStruct((n_tok, d), F32),
        scratch_shapes=[pltpu.VMEM((2, TOP_K, tc * per, LANES), F32), pltpu.SemaphoreType.DMA((2,))],
        compiler_params=_params("arbitrary"),
        name="moe_combine",
    )(*args)


def _routing(top_i, n_exp):
    n_tok = top_i.shape[1]
    e = top_i.reshape(-1)
    a = e.shape[0]
    experts = jnp.arange(n_exp, dtype=jnp.int32)
    onehot = (e[:, None] == experts[None, :]).astype(jnp.int32)
    csum = jnp.cumsum(onehot, axis=0)
    rank = jnp.sum(onehot * csum, axis=1) - 1
    counts = csum[-1]
    padded = (counts + MOE_BLOCK - 1) // MOE_BLOCK * MOE_BLOCK
    pend = jnp.cumsum(padded)
    pstart = pend - padded
    start = jnp.cumsum(counts) - counts
    dest = pstart[e] + rank
    cap = -(-a // MOE_BLOCK) * MOE_BLOCK + n_exp * MOE_BLOCK
    nblk = cap // MOE_BLOCK
    blk_first = jnp.arange(nblk, dtype=jnp.int32) * MOE_BLOCK
    blk_e = jnp.minimum(jnp.sum((pend[None, :] <= blk_first[:, None]).astype(jnp.int32), axis=1), n_exp - 1)
    n_used = (pend[-1] // MOE_BLOCK).astype(jnp.int32).reshape(1)
    order = jnp.argsort(e).astype(jnp.int32)
    row = jnp.arange(cap, dtype=jnp.int32)
    row_e = jnp.repeat(blk_e, MOE_BLOCK)
    within = row - pstart[row_e]
    valid = (within < counts[row_e]) & (row < pend[-1])
    tok_buf = jnp.where(valid, order[jnp.clip(start[row_e] + within, 0, a - 1)] % n_tok, 0)
    return dest.astype(jnp.int32).reshape(TOP_K, n_tok), tok_buf.astype(jnp.int32), blk_e.astype(jnp.int32), n_used


def _moe_mixer(h_mid, u, top_i, top_g, mod, mod_row, moe_w, n_exp, final_g):
    b, t, d = h_mid.shape
    n_tok = b * t
    dest, tok_buf, blk_e, n_used = _routing(top_i, n_exp)
    y = _experts(u, tok_buf, blk_e, n_used, *moe_w)
    out = _combine(h_mid.reshape(n_tok, d), y, dest, top_g, mod, mod_row, t, final_g)
    return out.reshape(b, t, d)


def _layer_weights(w_in_l, conv_w_l, b_gates_l, w_out_l):
    cd = conv_w_l.shape[1]
    n_gate = b_gates_l.size
    md = (w_in_l.shape[1] - 3 * cd - n_gate) // 4
    q0 = 3 * cd
    g0 = q0 + 3 * md
    wq, wv = w_in_l[:, q0:q0 + md], w_in_l[:, q0 + 2 * md:g0]
    wg, wo = w_in_l[:, g0:g0 + n_gate], w_in_l[:, g0 + n_gate:]
    return {
        "wc": w_in_l[:, :q0].astype(BF16),
        "wf_full": jnp.concatenate([wq, wv, wo, wg], axis=1).T.astype(BF16),
        "wf_state": jnp.concatenate([wq, wv, wg], axis=1).T.astype(BF16),
        "wk": w_in_l[:, q0 + md:q0 + 2 * md].astype(BF16),
        "bg": b_gates_l.reshape(n_gate, 1),
        "cw": conv_w_l,
        "woc": w_out_l[:cd].astype(BF16),
        "wom": w_out_l[cd:].astype(BF16),
    }


def kernel(x, c, ctx, c_ctx, norm1_g, norm2_g, w_ada, b_ada, w_in, conv_w, b_gates, mlstm_norm_g, w_out,
           ffn_w1, ffn_w3, ffn_w2, moe_router, moe_w1, moe_w3, moe_w2, final_norm_g):
    b, t, d = x.shape
    tc = ctx.shape[1]
    depth = w_in.shape[0]
    n_exp = moe_router.shape[-1]
    assert t % GRID_W == 0 and t % CHUNK == 0 and tc % CHUNK == 0 and d % (SUBLANES * LANES) == 0

    ctx_row = b
    n_rows = -(-(b + 1) // SUBLANES) * SUBLANES
    c_all = jnp.concatenate([c, c_ctx[None, :], jnp.zeros((n_rows - b - 1, d), F32)], axis=0)
    mod_all = _ada(c_all, w_ada, b_ada).reshape(depth, n_rows, 6, d)
    final_g = final_norm_g.reshape(1, d)

    h, hc = x, ctx
    for layer in range(depth):
        last = layer == depth - 1
        j = layer // 2
        mod = mod_all[layer]
        wts = _layer_weights(w_in[layer], conv_w[layer], b_gates[layer], w_out[layer])
        n1 = norm1_g[layer].reshape(1, d)
        n2 = norm2_g[layer].reshape(1, d)

        ql, kl, vl, gl, ycl, ogl = _in_proj(h, mod, None, n1, wts, GRID_W, True)
        res_c = _in_proj(hc, mod, ctx_row, n1, wts, tc, not last)
        qc, kc, vc, gc = res_c[:4]
        ogc = None if last else res_c[5]
        ymc, yml = _mlstm(gc, gl, (kc, qc, vc), (kl, ql, vl), ogc, ogl,
                          mlstm_norm_g[layer].reshape(1, -1), not last)

        def mixer(hh, yc_, ym_, mod_row, fin):
            if layer % 2 == 0:
                ffn = (ffn_w1[j].astype(BF16), ffn_w3[j].astype(BF16), ffn_w2[j].astype(BF16))
                return _out_dense(hh, yc_, ym_, mod, mod_row, n2, wts, ffn, fin)
            r_full = jnp.pad(moe_router[j], ((0, 0), (0, LANES - n_exp)))
            r_hi = r_full.astype(BF16)
            r_lo = (r_full - r_hi.astype(F32)).astype(BF16)
            router_pad = jnp.concatenate([r_hi, r_lo], axis=1)
            h_mid, u2, top_i, top_g = _out_moe(hh, yc_, ym_, mod, mod_row, n2, wts, router_pad, n_exp)
            moe_w = (moe_w1[j].astype(BF16), moe_w3[j].astype(BF16), moe_w2[j].astype(BF16))
            return _moe_mixer(h_mid, u2, top_i, top_g, mod, mod_row, moe_w, n_exp, fin)

        h = mixer(h, ycl, yml, None, final_g if last else None)
        if not last:
            hc = mixer(hc, res_c[4], ymc, ctx_row, None)
    return h
```

```python
import functools

import jax
import jax.numpy as jnp
from jax import lax
from jax.experimental import pallas as pl
from jax.experimental.pallas import tpu as pltpu

GRID_W = 64
CHUNK = 128
N_GATES = 4
EPS = 1e-6
TOP_K = 2
ROW_TILE = 1024
FFN_ROW_TILE = 512
MOE_BLOCK = 512
COMBINE_TILE = 1024
SUB_TILE = 256
FF_CHUNK = 1024
ADA_COL_TILES = 4
ISSUE_UNROLL = 8
LANES = 128
SUBLANES = 8
AUG_ROWS = 32
VMEM_LIMIT = 56 * 1024 * 1024

F32 = jnp.float32
BF16 = jnp.bfloat16
HIGHEST = lax.Precision.HIGHEST


def _dot(a, b):
    return jnp.dot(a, b, preferred_element_type=F32)


def _dot_nt(a, b):
    return lax.dot_general(a, b, (((1,), (1,)), ((), ())), preferred_element_type=F32)


def _dot_tn(a, b):
    return lax.dot_general(a, b, (((0,), (0,)), ((), ())), preferred_element_type=F32)


def _rows_to_tiles(ref, x, row0=0):
    n, d = x.shape
    per = d // LANES
    for s in range(per):
        ref[pl.ds(row0 * per + s, n, stride=per), :] = x[:, s * LANES:(s + 1) * LANES]


def _tiles_to_rows(ref, n, per):
    return jnp.concatenate([ref[pl.ds(s, n, stride=per), :] for s in range(per)], axis=1)


def _sub_tiles(tm):
    sub = min(SUB_TILE, tm)
    return [slice(r0, r0 + sub) for r0 in range(0, tm, sub)]


def _params(*sem):
    return pltpu.CompilerParams(dimension_semantics=sem, vmem_limit_bytes=VMEM_LIMIT)


def _ff_chunks(d_ff):
    out, c0 = [], 0
    while c0 < d_ff:
        c1 = min(c0 + FF_CHUNK, d_ff)
        out.append((c0, c1))
        c0 = c1
    return out


def _ada_kernel(c_ref, w_ref, b_ref, o_ref):
    s = c_ref[...]
    s = s * jax.nn.sigmoid(s)
    o_ref[0] = jnp.dot(s, w_ref[0], precision=HIGHEST, preferred_element_type=F32) + b_ref[0]


def _ada(c_all, w_ada, b_ada):
    depth, d, d6 = w_ada.shape
    rows = c_all.shape[0]
    tn = d6 // ADA_COL_TILES
    return pl.pallas_call(
        _ada_kernel,
        grid=(depth, d6 // tn),
        in_specs=[
            pl.BlockSpec((rows, d), lambda l, j: (0, 0)),
            pl.BlockSpec((1, d, tn), lambda l, j: (l, 0, j)),
            pl.BlockSpec((1, 1, tn), lambda l, j: (l, 0, j)),
        ],
        out_specs=pl.BlockSpec((1, rows, tn), lambda l, j: (l, 0, j)),
        out_shape=jax.ShapeDtypeStruct((depth, rows, d6), F32),
        compiler_params=_params("parallel", "parallel"),
        name="adaln_mod",
    )(c_all, w_ada, b_ada.reshape(depth, 1, d6))


def _norm_mod(x, g, shift, scale):
    y = x * lax.rsqrt(jnp.mean(x * x, axis=-1, keepdims=True) + EPS)
    return (y * g) * (1.0 + scale) + shift


def _in_kernel(x_ref, mod_ref, g_ref, wf_ref, wk_ref, bg_ref, *rest, row_w, full, kscale):
    if full:
        wc_ref, cw_ref, q_ref, k_ref, v_ref, gt_ref, yc_ref, og_ref = rest
    else:
        q_ref, k_ref, v_ref, gt_ref = rest
    md = q_ref.shape[1]
    for rows in _sub_tiles(x_ref.shape[1]):
        u = _norm_mod(x_ref[0, rows, :], g_ref[...], mod_ref[0, 0:1, :], mod_ref[0, 1:2, :]).astype(BF16)
        ft = _dot_nt(wf_ref[...], u)
        q_ref[0, :, rows] = ft[:md].astype(BF16)
        k_ref[0, rows, :] = (_dot(u, wk_ref[...]) * kscale).astype(BF16)
        v_ref[0, :, rows] = ft[md:2 * md].astype(BF16)
        gt_ref[0, :, rows] = ft[-bg_ref.shape[0]:] + bg_ref[...]
        if full:
            cd = cw_ref.shape[1]
            c3 = _dot(u, wc_ref[...])
            z = c3[:, cd:2 * cd] * c3[:, 2 * cd:]
            n = z.shape[0]
            t = lax.broadcasted_iota(jnp.int32, (n, 1), 0) % row_w
            zprev = jnp.where(t == 0, 0.0, pltpu.roll(z, 1, axis=0))
            znext = jnp.where(t == row_w - 1, 0.0, pltpu.roll(z, n - 1, axis=0))
            conv = zprev * cw_ref[0:1, :] + z * cw_ref[1:2, :] + znext * cw_ref[2:3, :]
            yc_ref[0, rows, :] = (c3[:, :cd] * conv).astype(BF16)
            og_ref[0, :, rows] = jax.nn.sigmoid(ft[2 * md:3 * md]).astype(BF16)


def _in_proj(h, mod, mod_row, norm_g, wts, row_w, full):
    b, t, d = h.shape
    tm = min(ROW_TILE, t)
    md = wts["wk"].shape[1]
    n_gate = wts["bg"].shape[0]
    head_dim = md // (n_gate // N_GATES)
    wf = wts["wf_full"] if full else wts["wf_state"]
    const = lambda shape: pl.BlockSpec(shape, lambda i, j: (0,) * len(shape))
    tile = lambda w: pl.BlockSpec((1, tm, w), lambda i, j: (i, j, 0))
    tile_t = pl.BlockSpec((1, md, tm), lambda i, j: (i, 0, j))
    feat_major = jax.ShapeDtypeStruct((b, md, t), BF16)
    in_specs = [
        tile(d),
        pl.BlockSpec((1, 6, d), (lambda i, j: (i, 0, 0)) if mod_row is None
                     else (lambda i, j: (mod_row, 0, 0))),
        const((1, d)), const(wf.shape), const(wts["wk"].shape), const(wts["bg"].shape),
    ]
    args = [h, mod, norm_g, wf, wts["wk"], wts["bg"]]
    out_specs = [tile_t, tile(md), tile_t, pl.BlockSpec((1, n_gate, tm), lambda i, j: (i, 0, j))]
    out_shape = [feat_major, jax.ShapeDtypeStruct((b, t, md), BF16), feat_major,
                 jax.ShapeDtypeStruct((b, n_gate, t), F32)]
    if full:
        cd = wts["cw"].shape[1]
        in_specs += [const(wts["wc"].shape), const(wts["cw"].shape)]
        args += [wts["wc"], wts["cw"]]
        out_specs += [tile(cd), tile_t]
        out_shape += [jax.ShapeDtypeStruct((b, t, cd), BF16), feat_major]
    return pl.pallas_call(
        functools.partial(_in_kernel, row_w=row_w, full=full, kscale=head_dim ** -0.5),
        grid=(b, t // tm),
        in_specs=in_specs,
        out_specs=out_specs,
        out_shape=out_shape,
        compiler_params=_params("parallel", "parallel"),
        name="in_proj",
    )(*args)


def _lane_scan(x, op, ident, reverse):
    n = x.shape[1]
    lane = lax.broadcasted_iota(jnp.int32, x.shape, 1)
    s = 1
    while s < n:
        if reverse:
            shifted = jnp.where(lane < n - s, pltpu.roll(x, n - s, axis=1), ident)
        else:
            shifted = jnp.where(lane >= s, pltpu.roll(x, s, axis=1), ident)
        x = op(x, shifted)
        s *= 2
    return x


def _mlstm_kernel(gc_ref, gl_ref, kc_ref, qc_ref, vc_ref, kl_ref, ql_ref, vl_ref, ogl_ref, gain_ref,
                  *rest, need_ctx):
    if need_ctx:
        ogc_ref, yc_ref, yl_ref, rt_ref, ra_ref, rp_ref, rw_ref, s_ref, upd_ref, st_ref = rest
    else:
        yl_ref, rt_ref, ra_ref, rp_ref, rw_ref, s_ref, upd_ref, st_ref = rest
        ogc_ref = yc_ref = None
    L = CHUNK
    hd = kl_ref.shape[2]
    ncc = kc_ref.shape[1] // L
    ncl = kl_ref.shape[1] // L
    nct = ncc + ncl

    head = pl.program_id(1)
    n_heads = gl_ref.shape[1] // N_GATES

    def gate_rows(gate, hh):
        r = gate * n_heads + hh
        return jnp.concatenate([ref[0, r:r + 1, c * L:(c + 1) * L]
                                for ref, n in ((gc_ref, ncc), (gl_ref, ncl)) for c in range(n)], axis=0)

    @pl.when(head == 0)
    def _():
        for d in range(2):
            rev = d == 1
            last = 0 if rev else L - 1
            for hh in range(n_heads):
                ig = gate_rows(2 * d, hh)
                lf = jax.nn.log_sigmoid(gate_rows(2 * d + 1, hh))
                a = _lane_scan(lf, jnp.add, 0.0, rev)
                r = ig - a
                p = _lane_scan(r, jnp.maximum, -jnp.inf, rev)
                ra_ref[d, hh, 0:nct, :] = a
                rp_ref[d, hh, 0:nct, :] = p
                rw_ref[d, hh, 0:nct, :] = jnp.exp(r - p[:, last:last + 1])
                r_pad = jnp.concatenate([r, jnp.zeros((L - nct, L), F32)], axis=0)
                rt_ref[d, hh] = r_pad.T

    spos = lax.broadcasted_iota(jnp.int32, (L, L), 0)
    tpos = lax.broadcasted_iota(jnp.int32, (L, L), 1)
    ones_row = jnp.where(lax.broadcasted_iota(jnp.int32, (AUG_ROWS, L), 0) == 0, 1.0, 0.0).astype(BF16)
    gain = gain_ref[...]

    ctx_seq = (kc_ref, qc_ref, vc_ref, ogc_ref, yc_ref, ncc, 0, need_ctx)
    lat_seq = (kl_ref, ql_ref, vl_ref, ogl_ref, yl_ref, ncl, ncc, True)

    for k_ref, q_ref, v_ref, _, _, n, g0, want in (ctx_seq, lat_seq):
        for j in range(n):
            g = g0 + j
            k = k_ref[0, pl.ds(j * L, L), :]
            if want:
                s_ref[g] = _dot(k, q_ref[0, :, pl.ds(j * L, L)])
            vaug = jnp.concatenate([v_ref[0, :, pl.ds(j * L, L)], ones_row], axis=0).astype(F32)
            for d in range(2):
                vw = (vaug * rw_ref[d, head, g:g + 1, :]).astype(BF16)
                upd_ref[d, g] = _dot(vw, k)

    m_in = {}
    for d in range(2):
        last = 0 if d == 1 else L - 1
        caug, m = jnp.zeros((hd + AUG_ROWS, hd), F32), jnp.zeros((1, 1), F32)
        for _, _, _, _, _, n, g0, _ in (ctx_seq, lat_seq):
            for i in range(n):
                g = g0 + (n - 1 - i if d == 1 else i)
                st_ref[d, g] = caug.astype(BF16)
                m_in[d, g] = m
                b_tot = ra_ref[d, head, g:g + 1, last:last + 1]
                p_last = rp_ref[d, head, g:g + 1, last:last + 1]
                m_last = jnp.maximum(p_last, m)
                caug = jnp.exp(m - m_last) * caug + jnp.exp(p_last - m_last) * upd_ref[d, g]
                m = b_tot + m_last

    for k_ref, q_ref, v_ref, og_ref, y_ref, n, g0, want in (ctx_seq, lat_seq):
        for j in range(n if want else 0):
            g = g0 + j
            lanes = pl.ds(j * L, L)
            q = q_ref[0, :, lanes]
            vaug = jnp.concatenate([v_ref[0, :, lanes], ones_row], axis=0)
            rhs, scale = [], []
            for d in range(2):
                a_row = ra_ref[d, head, g:g + 1, :]
                p_row = rp_ref[d, head, g:g + 1, :]
                m = m_in[d, g]
                mask = (spos >= tpos) if d == 1 else (spos <= tpos)
                w = jnp.where(mask, jnp.exp(rt_ref[d, head, :, g:g + 1] - p_row), 0.0)
                m_row = jnp.maximum(p_row, m)
                sw = (s_ref[g] * w * jnp.exp(p_row - m_row)).astype(BF16)
                qs = (q.astype(F32) * jnp.exp(m - m_row)).astype(BF16)
                rhs.append((sw, qs))
                scale.append(jnp.exp(-(a_row + m_row)))
            zero = jnp.zeros((hd, L), BF16)
            nd = _dot(jnp.concatenate([vaug, st_ref[0, g], st_ref[1, g]], axis=1),
                      jnp.concatenate([jnp.concatenate([rhs[0][0], rhs[1][0]], axis=1),
                                       jnp.concatenate([rhs[0][1], zero], axis=1),
                                       jnp.concatenate([zero, rhs[1][1]], axis=1)], axis=0))
            hh = None
            for d in range(2):
                part = nd[:, d * L:(d + 1) * L]
                den = jnp.maximum(jnp.abs(part[hd:hd + 1, :]), scale[d])
                hout = part[:hd, :] * (1.0 / den)
                hh = hout if hh is None else hh + hout
            hn = hh * lax.rsqrt(jnp.mean(hh * hh, axis=0, keepdims=True) + EPS)
            y_ref[0, :, lanes] = (hn * gain * og_ref[0, :, lanes].astype(F32)).astype(BF16)


def _mlstm(gates_c, gates_l, kqv_c, kqv_l, og_c, og_l, gain, need_ctx):
    b, tc, md = kqv_c[0].shape
    tl = kqv_l[0].shape[1]
    n_gate = gates_c.shape[1]
    heads = n_gate // N_GATES
    hd = md // heads
    L = CHUNK
    nct = (tc + tl) // L
    assert nct <= L
    nrow = -(-nct // SUBLANES) * SUBLANES
    tok = lambda t: pl.BlockSpec((1, t, hd), lambda i, h: (i, 0, h))
    feat = lambda t: pl.BlockSpec((1, hd, t), lambda i, h: (i, h, 0))
    gate_spec = lambda t: pl.BlockSpec((1, n_gate, t), lambda i, h: (i, 0, 0))
    in_specs = [gate_spec(tc), gate_spec(tl),
                tok(tc), feat(tc), feat(tc), tok(tl), feat(tl), feat(tl), feat(tl),
                pl.BlockSpec((hd, 1), lambda i, h: (h, 0))]
    args = [gates_c, gates_l, *kqv_c, *kqv_l, og_l, gain.reshape(md, 1)]
    out_specs = [feat(tl)]
    out_shape = [jax.ShapeDtypeStruct((b, md, tl), BF16)]
    if need_ctx:
        in_specs.append(feat(tc))
        args.append(og_c)
        out_specs = [feat(tc)] + out_specs
        out_shape = [jax.ShapeDtypeStruct((b, md, tc), BF16)] + out_shape
    res = pl.pallas_call(
        functools.partial(_mlstm_kernel, need_ctx=need_ctx),
        grid=(b, heads),
        in_specs=in_specs,
        out_specs=out_specs,
        out_shape=out_shape,
        scratch_shapes=[
            pltpu.VMEM((2, heads, L, L), F32),
            pltpu.VMEM((2, heads, nrow, L), F32),
            pltpu.VMEM((2, heads, nrow, L), F32),
            pltpu.VMEM((2, heads, nrow, L), F32),
            pltpu.VMEM((nct, L, L), F32),
            pltpu.VMEM((2, nct, hd + AUG_ROWS, hd), F32),
            pltpu.VMEM((2, nct, hd + AUG_ROWS, hd), BF16),
        ],
        compiler_params=_params("parallel", "arbitrary"),
        name="mlstm_scan",
    )(*args)
    return (res[0], res[1]) if need_ctx else (None, res[0])


def _mix_residual(h_ref, yc_ref, ym_ref, mod_ref, woc_ref, wom_ref, rows):
    mix = _dot(yc_ref[0, rows, :], woc_ref[...]) + _dot_tn(ym_ref[0, :, rows], wom_ref[...])
    return h_ref[0, rows, :] + mod_ref[0, 2:3, :] * mix


def _out_dense_kernel(h_ref, yc_ref, ym_ref, mod_ref, g2_ref, woc_ref, wom_ref, w1_ref, w3_ref, w2_ref,
                      *rest, chunks, final):
    if final:
        gf_ref, o_ref = rest
    else:
        (o_ref,) = rest
    h = _mix_residual(h_ref, yc_ref, ym_ref, mod_ref, woc_ref, wom_ref, slice(None))
    u = _norm_mod(h, g2_ref[...], mod_ref[0, 3:4, :], mod_ref[0, 4:5, :]).astype(BF16)
    acc = None
    for c0, c1 in chunks:
        a = _dot(u, w1_ref[:, c0:c1])
        hid = (a * jax.nn.sigmoid(a) * _dot(u, w3_ref[:, c0:c1])).astype(BF16)
        part = _dot(hid, w2_ref[c0:c1, :])
        acc = part if acc is None else acc + part
    h = h + mod_ref[0, 5:6, :] * acc
    if final:
        h = h * lax.rsqrt(jnp.mean(h * h, axis=-1, keepdims=True) + EPS) * gf_ref[...]
    o_ref[0] = h


def _out_moe_kernel(h_ref, yc_ref, ym_ref, mod_ref, g2_ref, woc_ref, wom_ref, router_ref,
                    hm_ref, u_ref, ti_ref, tg_ref, *, n_exp):
    for rows in _sub_tiles(h_ref.shape[1]):
        _out_moe_rows(h_ref, yc_ref, ym_ref, mod_ref, g2_ref, woc_ref, wom_ref, router_ref,
                      hm_ref, u_ref, ti_ref, tg_ref, rows, n_exp)


def _out_moe_rows(h_ref, yc_ref, ym_ref, mod_ref, g2_ref, woc_ref, wom_ref, router_ref,
                  hm_ref, u_ref, ti_ref, tg_ref, rows, n_exp):
    h = _mix_residual(h_ref, yc_ref, ym_ref, mod_ref, woc_ref, wom_ref, rows)
    hm_ref[0, rows, :] = h
    u = _norm_mod(h, g2_ref[...], mod_ref[0, 3:4, :], mod_ref[0, 4:5, :])
    _rows_to_tiles(u_ref, u, rows.start)
    u_hi = u.astype(BF16)
    u_lo = (u - u_hi.astype(F32)).astype(BF16)
    parts = _dot(u_hi, router_ref[...]) + _dot(u_lo, router_ref[...])
    logits = parts[:, :LANES] + parts[:, LANES:]
    n_pad = -(-n_exp // SUBLANES) * SUBLANES
    lt = logits.T[:n_pad, :]
    row = lax.broadcasted_iota(jnp.int32, lt.shape, 0)
    lt = jnp.where(row < n_exp, lt, -jnp.inf)
    v1 = jnp.max(lt, axis=0, keepdims=True)
    i1 = jnp.min(jnp.where(lt == v1, row, n_pad), axis=0, keepdims=True)
    rest = jnp.where(row == i1, -jnp.inf, lt)
    v2 = jnp.max(rest, axis=0, keepdims=True)
    i2 = jnp.min(jnp.where(rest == v2, row, n_pad), axis=0, keepdims=True)
    e2 = jnp.exp(v2 - v1)
    inv = 1.0 / (1.0 + e2)
    ti_ref[:, rows] = jnp.concatenate([i1, i2], axis=0)
    tg_ref[:, rows] = jnp.concatenate([inv, e2 * inv], axis=0)


def _out_common(h, yc, ym, mod, mod_row, norm_g, wts, row_tile):
    b, t, d = h.shape
    tm = min(row_tile, t)
    const = lambda shape: pl.BlockSpec(shape, lambda i, j: (0,) * len(shape))
    tile = lambda w: pl.BlockSpec((1, tm, w), lambda i, j: (i, j, 0))
    in_specs = [
        tile(d), tile(yc.shape[2]), pl.BlockSpec((1, ym.shape[1], tm), lambda i, j: (i, 0, j)),
        pl.BlockSpec((1, 6, d), (lambda i, j: (i, 0, 0)) if mod_row is None
                     else (lambda i, j: (mod_row, 0, 0))),
        const((1, d)), const(wts["woc"].shape), const(wts["wom"].shape),
    ]
    args = [h, yc, ym, mod, norm_g, wts["woc"], wts["wom"]]
    return b, t, d, tm, const, tile, in_specs, args


def _out_dense(h, yc, ym, mod, mod_row, norm_g, wts, ffn, final_g):
    b, t, d, tm, const, tile, in_specs, args = _out_common(h, yc, ym, mod, mod_row, norm_g, wts, FFN_ROW_TILE)
    w1, w3, w2 = ffn
    in_specs += [const(w1.shape), const(w3.shape), const(w2.shape)]
    args += [w1, w3, w2]
    if final_g is not None:
        in_specs.append(const((1, d)))
        args.append(final_g)
    return pl.pallas_call(
        functools.partial(_out_dense_kernel, chunks=_ff_chunks(w1.shape[1]), final=final_g is not None),
        grid=(b, t // tm),
        in_specs=in_specs,
        out_specs=tile(d),
        out_shape=jax.ShapeDtypeStruct((b, t, d), F32),
        compiler_params=_params("parallel", "parallel"),
        name="out_dense_ffn",
    )(*args)


def _out_moe(h, yc, ym, mod, mod_row, norm_g, wts, router_pad, n_exp):
    b, t, d, tm, const, tile, in_specs, args = _out_common(h, yc, ym, mod, mod_row, norm_g, wts, ROW_TILE)
    in_specs.append(const(router_pad.shape))
    args.append(router_pad)
    per = d // LANES
    top2 = pl.BlockSpec((TOP_K, tm), lambda i, j: (0, i * (t // tm) + j))
    return pl.pallas_call(
        functools.partial(_out_moe_kernel, n_exp=n_exp),
        grid=(b, t // tm),
        in_specs=in_specs,
        out_specs=[tile(d), pl.BlockSpec((tm * per, LANES), lambda i, j: (i * (t // tm) + j, 0)),
                   top2, top2],
        out_shape=[jax.ShapeDtypeStruct((b, t, d), F32), jax.ShapeDtypeStruct((b * t * per, LANES), F32),
                   jax.ShapeDtypeStruct((TOP_K, b * t), jnp.int32), jax.ShapeDtypeStruct((TOP_K, b * t), F32)],
        compiler_params=_params("parallel", "parallel"),
        name="out_router",
    )(*args)


def _expert_kernel(be_ref, nu_ref, tok_ref, toknext_ref, u_hbm, w1_ref, w3_ref, w2_ref, y_ref, xbuf, sem,
                   *, chunks, per):
    i = pl.program_id(0)
    n_used = nu_ref[0]
    blk = xbuf.shape[1] // per
    slot = i % 2

    def row_copy(idx_ref, r, to_slot):
        src = pl.multiple_of(idx_ref[0, 0, r], per)
        return pltpu.make_async_copy(u_hbm.at[pl.ds(src, per), :],
                                     xbuf.at[to_slot, pl.ds(r * per, per), :], sem.at[to_slot])

    @pl.when(i == 0)
    def _():
        def body(r, c):
            row_copy(tok_ref, r, 0).start()
            return c
        lax.fori_loop(0, blk, body, 0, unroll=ISSUE_UNROLL)

    @pl.when(i <= n_used)
    def _():
        pltpu.make_async_copy(u_hbm.at[pl.ds(0, blk * per), :], xbuf.at[slot], sem.at[slot]).wait()

    @pl.when(i < n_used)
    def _():
        for r in range(blk):
            row_copy(toknext_ref, r, 1 - slot).start()
        x = _tiles_to_rows(xbuf.at[slot], blk, per).astype(BF16)
        acc = None
        for c0, c1 in chunks:
            a = _dot(x, w1_ref[0, :, c0:c1])
            hid = (a * jax.nn.sigmoid(a) * _dot(x, w3_ref[0, :, c0:c1])).astype(BF16)
            part = _dot(hid, w2_ref[0, c0:c1, :])
            acc = part if acc is None else acc + part
        _rows_to_tiles(y_ref, acc)

    @pl.when(i >= n_used)
    def _():
        y_ref[...] = jnp.zeros(y_ref.shape, F32)


def _experts(u_tiles, tok_buf, blk_e, n_used, w1, w3, w2):
    d, f = w1.shape[1], w1.shape[2]
    per = d // LANES
    nblk = tok_buf.shape[0] // MOE_BLOCK
    tok3 = (tok_buf * per).reshape(nblk, 1, MOE_BLOCK)
    smem = lambda fn: pl.BlockSpec((1, 1, MOE_BLOCK), fn, memory_space=pltpu.SMEM)
    grid_spec = pltpu.PrefetchScalarGridSpec(
        num_scalar_prefetch=2,
        grid=(nblk,),
        in_specs=[
            smem(lambda i, be, nu: (i, 0, 0)),
            smem(lambda i, be, nu: (jnp.minimum(i + 1, nblk - 1), 0, 0)),
            pl.BlockSpec(memory_space=pl.ANY),
            pl.BlockSpec((1, d, f), lambda i, be, nu: (be[i], 0, 0)),
            pl.BlockSpec((1, d, f), lambda i, be, nu: (be[i], 0, 0)),
            pl.BlockSpec((1, f, d), lambda i, be, nu: (be[i], 0, 0)),
        ],
        out_specs=pl.BlockSpec((MOE_BLOCK * per, LANES), lambda i, be, nu: (i, 0)),
        scratch_shapes=[pltpu.VMEM((2, MOE_BLOCK * per, LANES), F32), pltpu.SemaphoreType.DMA((2,))],
    )
    return pl.pallas_call(
        functools.partial(_expert_kernel, chunks=_ff_chunks(f), per=per),
        grid_spec=grid_spec,
        out_shape=jax.ShapeDtypeStruct((nblk * MOE_BLOCK * per, LANES), F32),
        compiler_params=_params("arbitrary"),
        name="moe_experts",
    )(blk_e, n_used, tok3, tok3, u_tiles, w1, w3, w2)


def _combine_kernel(dst_ref, dnext_ref, h_ref, gate_ref, mod_ref, *rest, final):
    if final:
        gf_ref, y_hbm, o_ref, ybuf, sem = rest
    else:
        y_hbm, o_ref, ybuf, sem = rest
    i = pl.program_id(0)
    n = pl.num_programs(0)
    tc, d = h_ref.shape
    per = d // LANES

    def issue(idx_ref, slot):
        for r in range(tc):
            for k in range(TOP_K):
                src = pl.multiple_of(idx_ref[0, 0, k * tc + r], per)
                pltpu.make_async_copy(y_hbm.at[pl.ds(src, per), :],
                                      ybuf.at[slot, k, pl.ds(r * per, per), :],
                                      sem.at[slot]).start(priority=k)

    @pl.when(i == 0)
    def _():
        issue(dst_ref, 0)

    slot = i % 2

    @pl.when(i + 1 < n)
    def _():
        issue(dnext_ref, 1 - slot)

    for k in range(TOP_K):
        pltpu.make_async_copy(y_hbm.at[pl.ds(0, tc * per), :], ybuf.at[slot, k], sem.at[slot]).wait()

    gate = jnp.concatenate([gate_ref[...], jnp.zeros((SUBLANES - TOP_K, tc), F32)], axis=0).T
    y = (gate[:, 0:1] * _tiles_to_rows(ybuf.at[slot, 0], tc, per)
         + gate[:, 1:2] * _tiles_to_rows(ybuf.at[slot, 1], tc, per))
    h = h_ref[...] + mod_ref[0, 5:6, :] * y
    if final:
        h = h * lax.rsqrt(jnp.mean(h * h, axis=-1, keepdims=True) + EPS) * gf_ref[...]
    o_ref[...] = h


def _combine(h_flat, y, dest, gates, mod, mod_row, tokens_per_batch, final_g):
    n_tok, d = h_flat.shape
    tc = COMBINE_TILE
    steps = n_tok // tc
    per_b = tokens_per_batch // tc
    per = d // LANES
    dst2 = (dest * per).reshape(TOP_K, steps, tc).transpose(1, 0, 2).reshape(steps, 1, TOP_K * tc)
    smem = lambda fn: pl.BlockSpec((1, 1, TOP_K * tc), fn, memory_space=pltpu.SMEM)
    in_specs = [
        smem(lambda i: (i, 0, 0)),
        smem(lambda i: (jnp.minimum(i + 1, steps - 1), 0, 0)),
        pl.BlockSpec((tc, d), lambda i: (i, 0)),
        pl.BlockSpec((TOP_K, tc), lambda i: (0, i)),
        pl.BlockSpec((1, 6, d), (lambda i: (i // per_b, 0, 0)) if mod_row is None
                     else (lambda i: (mod_row, 0, 0))),
    ]
    args = [dst2, dst2, h_flat, gates, mod]
    if final_g is not None:
        in_specs.append(pl.BlockSpec((1, d), lambda i: (0, 0)))
        args.append(final_g)
    in_specs.append(pl.BlockSpec(memory_space=pl.ANY))
    args.append(y)
    return pl.pallas_call(
        functools.partial(_combine_kernel, final=final_g is not None),
        grid=(steps,),
        in_specs=in_specs,
        out_specs=pl.BlockSpec((tc, d), lambda i: (i, 0)),
        out_shape=jax.ShapeDtypeStruct((n_tok, d), F32),
        scratch_shapes=[pltpu.VMEM((2, TOP_K, tc * per, LANES), F32), pltpu.SemaphoreType.DMA((2,))],
        compiler_params=_params("arbitrary"),
        name="moe_combine",
    )(*args)


def _routing(top_i, n_exp):
    n_tok = top_i.shape[1]
    e = top_i.reshape(-1)
    a = e.shape[0]
    experts = jnp.arange(n_exp, dtype=jnp.int32)
    onehot = (e[:, None] == experts[None, :]).astype(jnp.int32)
    csum = jnp.cumsum(onehot, axis=0)
    rank = jnp.sum(onehot * csum, axis=1) - 1
    counts = csum[-1]
    padded = (counts + MOE_BLOCK - 1) // MOE_BLOCK * MOE_BLOCK
    pend = jnp.cumsum(padded)
    pstart = pend - padded
    start = jnp.cumsum(counts) - counts
    dest = pstart[e] + rank
    cap = -(-a // MOE_BLOCK) * MOE_BLOCK + n_exp * MOE_BLOCK
    nblk = cap // MOE_BLOCK
    blk_first = jnp.arange(nblk, dtype=jnp.int32) * MOE_BLOCK
    blk_e = jnp.minimum(jnp.sum((pend[None, :] <= blk_first[:, None]).astype(jnp.int32), axis=1), n_exp - 1)
    n_used = (pend[-1] // MOE_BLOCK).astype(jnp.int32).reshape(1)
    order = jnp.argsort(e).astype(jnp.int32)
    row = jnp.arange(cap, dtype=jnp.int32)
    row_e = jnp.repeat(blk_e, MOE_BLOCK)
    within = row - pstart[row_e]
    valid = (within < counts[row_e]) & (row < pend[-1])
    tok_buf = jnp.where(valid, order[jnp.clip(start[row_e] + within, 0, a - 1)] % n_tok, 0)
    return dest.astype(jnp.int32).reshape(TOP_K, n_tok), tok_buf.astype(jnp.int32), blk_e.astype(jnp.int32), n_used


def _moe_mixer(h_mid, u, top_i, top_g, mod, mod_row, moe_w, n_exp, final_g):
    b, t, d = h_mid.shape
    n_tok = b * t
    dest, tok_buf, blk_e, n_used = _routing(top_i, n_exp)
    y = _experts(u, tok_buf, blk_e, n_used, *moe_w)
    out = _combine(h_mid.reshape(n_tok, d), y, dest, top_g, mod, mod_row, t, final_g)
    return out.reshape(b, t, d)


def _layer_weights(w_in_l, conv_w_l, b_gates_l, w_out_l):
    cd = conv_w_l.shape[1]
    n_gate = b_gates_l.size
    md = (w_in_l.shape[1] - 3 * cd - n_gate) // 4
    q0 = 3 * cd
    g0 = q0 + 3 * md
    wq, wv = w_in_l[:, q0:q0 + md], w_in_l[:, q0 + 2 * md:g0]
    wg, wo = w_in_l[:, g0:g0 + n_gate], w_in_l[:, g0 + n_gate:]
    return {
        "wc": w_in_l[:, :q0].astype(BF16),
        "wf_full": jnp.concatenate([wq, wv, wo, wg], axis=1).T.astype(BF16),
        "wf_state": jnp.concatenate([wq, wv, wg], axis=1).T.astype(BF16),
        "wk": w_in_l[:, q0 + md:q0 + 2 * md].astype(BF16),
        "bg": b_gates_l.reshape(n_gate, 1),
        "cw": conv_w_l,
        "woc": w_out_l[:cd].astype(BF16),
        "wom": w_out_l[cd:].astype(BF16),
    }


def kernel(x, c, ctx, c_ctx, norm1_g, norm2_g, w_ada, b_ada, w_in, conv_w, b_gates, mlstm_norm_g, w_out,
           ffn_w1, ffn_w3, ffn_w2, moe_router, moe_w1, moe_w3, moe_w2, final_norm_g):
    b, t, d = x.shape
    tc = ctx.shape[1]
    depth = w_in.shape[0]
    n_exp = moe_router.shape[-1]
    assert t % GRID_W == 0 and t % CHUNK == 0 and tc % CHUNK == 0 and d % (SUBLANES * LANES) == 0

    ctx_row = b
    n_rows = -(-(b + 1) // SUBLANES) * SUBLANES
    c_all = jnp.concatenate([c, c_ctx[None, :], jnp.zeros((n_rows - b - 1, d), F32)], axis=0)
    mod_all = _ada(c_all, w_ada, b_ada).reshape(depth, n_rows, 6, d)
    final_g = final_norm_g.reshape(1, d)

    h, hc = x, ctx
    for layer in range(depth):
        last = layer == depth - 1
        j = layer // 2
        mod = mod_all[layer]
        wts = _layer_weights(w_in[layer], conv_w[layer], b_gates[layer], w_out[layer])
        n1 = norm1_g[layer].reshape(1, d)
        n2 = norm2_g[layer].reshape(1, d)

        ql, kl, vl, gl, ycl, ogl = _in_proj(h, mod, None, n1, wts, GRID_W, True)
        res_c = _in_proj(hc, mod, ctx_row, n1, wts, tc, not last)
        qc, kc, vc, gc = res_c[:4]
        ogc = None if last else res_c[5]
        ymc, yml = _mlstm(gc, gl, (kc, qc, vc), (kl, ql, vl), ogc, ogl,
                          mlstm_norm_g[layer].reshape(1, -1), not last)

        def mixer(hh, yc_, ym_, mod_row, fin):
            if layer % 2 == 0:
                ffn = (ffn_w1[j].astype(BF16), ffn_w3[j].astype(BF16), ffn_w2[j].astype(BF16))
                return _out_dense(hh, yc_, ym_, mod, mod_row, n2, wts, ffn, fin)
            r_full = jnp.pad(moe_router[j], ((0, 0), (0, LANES - n_exp)))
            r_hi = r_full.astype(BF16)
            r_lo = (r_full - r_hi.astype(F32)).astype(BF16)
            router_pad = jnp.concatenate([r_hi, r_lo], axis=1)
            h_mid, u2, top_i, top_g = _out_moe(hh, yc_, ym_, mod, mod_row, n2, wts, router_pad, n_exp)
            moe_w = (moe_w1[j].astype(BF16), moe_w3[j].astype(BF16), moe_w2[j].astype(BF16))
            return _moe_mixer(h_mid, u2, top_i, top_g, mod, mod_row, moe_w, n_exp, fin)

        h = mixer(h, ycl, yml, None, final_g if last else None)
        if not last:
            hc = mixer(hc, res_c[4], ymc, ctx_row, None)
    return h
```

```python
import functools

import jax
import jax.numpy as jnp
from jax import lax
from jax.experimental import pallas as pl
from jax.experimental.pallas import tpu as pltpu

GRID_W = 64
CHUNK = 128
N_GATES = 4
EPS = 1e-6
TOP_K = 2
ROW_TILE = 1024
FFN_ROW_TILE = 512
MOE_BLOCK = 512
COMBINE_TILE = 512
SUB_TILE = 256
FF_CHUNK = 1024
ADA_COL_TILES = 4
ISSUE_UNROLL = 8
LANES = 128
SUBLANES = 8
AUG_ROWS = 32
VMEM_LIMIT = 56 * 1024 * 1024

F32 = jnp.float32
BF16 = jnp.bfloat16
HIGHEST = lax.Precision.HIGHEST


def _dot(a, b):
    return jnp.dot(a, b, preferred_element_type=F32)


def _dot_nt(a, b):
    return lax.dot_general(a, b, (((1,), (1,)), ((), ())), preferred_element_type=F32)


def _dot_tn(a, b):
    return lax.dot_general(a, b, (((0,), (0,)), ((), ())), preferred_element_type=F32)


def _rows_to_tiles(ref, x, row0=0):
    n, d = x.shape
    per = d // LANES
    for s in range(per):
        ref[pl.ds(row0 * per + s, n, stride=per), :] = x[:, s * LANES:(s + 1) * LANES]


def _tiles_to_rows(ref, n, per):
    return jnp.concatenate([ref[pl.ds(s, n, stride=per), :] for s in range(per)], axis=1)


def _sub_tiles(tm):
    sub = min(SUB_TILE, tm)
    return [slice(r0, r0 + sub) for r0 in range(0, tm, sub)]


def _params(*sem):
    return pltpu.CompilerParams(dimension_semantics=sem, vmem_limit_bytes=VMEM_LIMIT)


def _ff_chunks(d_ff):
    out, c0 = [], 0
    while c0 < d_ff:
        c1 = min(c0 + FF_CHUNK, d_ff)
        out.append((c0, c1))
        c0 = c1
    return out


def _ada_kernel(c_ref, w_ref, b_ref, o_ref):
    s = c_ref[...]
    s = s * jax.nn.sigmoid(s)
    o_ref[0] = jnp.dot(s, w_ref[0], precision=HIGHEST, preferred_element_type=F32) + b_ref[0]


def _ada(c_all, w_ada, b_ada):
    depth, d, d6 = w_ada.shape
    rows = c_all.shape[0]
    tn = d6 // ADA_COL_TILES
    return pl.pallas_call(
        _ada_kernel,
        grid=(depth, d6 // tn),
        in_specs=[
            pl.BlockSpec((rows, d), lambda l, j: (0, 0)),
            pl.BlockSpec((1, d, tn), lambda l, j: (l, 0, j)),
            pl.BlockSpec((1, 1, tn), lambda l, j: (l, 0, j)),
        ],
        out_specs=pl.BlockSpec((1, rows, tn), lambda l, j: (l, 0, j)),
        out_shape=jax.ShapeDtypeStruct((depth, rows, d6), F32),
        compiler_params=_params("parallel", "parallel"),
        name="adaln_mod",
    )(c_all, w_ada, b_ada.reshape(depth, 1, d6))


def _norm_mod(x, g, shift, scale):
    y = x * lax.rsqrt(jnp.mean(x * x, axis=-1, keepdims=True) + EPS)
    return (y * g) * (1.0 + scale) + shift


def _in_kernel(x_ref, mod_ref, g_ref, wf_ref, wk_ref, bg_ref, *rest, row_w, full, kscale):
    if full:
        wc_ref, cw_ref, q_ref, k_ref, v_ref, gt_ref, yc_ref, og_ref = rest
    else:
        q_ref, k_ref, v_ref, gt_ref = rest
    md = q_ref.shape[1]
    for rows in _sub_tiles(x_ref.shape[1]):
        u = _norm_mod(x_ref[0, rows, :], g_ref[...], mod_ref[0, 0:1, :], mod_ref[0, 1:2, :]).astype(BF16)
        ft = _dot_nt(wf_ref[...], u)
        q_ref[0, :, rows] = ft[:md].astype(BF16)
        k_ref[0, rows, :] = (_dot(u, wk_ref[...]) * kscale).astype(BF16)
        v_ref[0, :, rows] = ft[md:2 * md].astype(BF16)
        gt_ref[0, :, rows] = ft[-bg_ref.shape[0]:] + bg_ref[...]
        if full:
            cd = cw_ref.shape[1]
            c3 = _dot(u, wc_ref[...])
            z = c3[:, cd:2 * cd] * c3[:, 2 * cd:]
            n = z.shape[0]
            t = lax.broadcasted_iota(jnp.int32, (n, 1), 0) % row_w
            zprev = jnp.where(t == 0, 0.0, pltpu.roll(z, 1, axis=0))
            znext = jnp.where(t == row_w - 1, 0.0, pltpu.roll(z, n - 1, axis=0))
            conv = zprev * cw_ref[0:1, :] + z * cw_ref[1:2, :] + znext * cw_ref[2:3, :]
            yc_ref[0, rows, :] = (c3[:, :cd] * conv).astype(BF16)
            og_ref[0, :, rows] = jax.nn.sigmoid(ft[2 * md:3 * md]).astype(BF16)


def _in_proj(h, mod, mod_row, norm_g, wts, row_w, full):
    b, t, d = h.shape
    tm = min(ROW_TILE, t)
    md = wts["wk"].shape[1]
    n_gate = wts["bg"].shape[0]
    head_dim = md // (n_gate // N_GATES)
    wf = wts["wf_full"] if full else wts["wf_state"]
    const = lambda shape: pl.BlockSpec(shape, lambda i, j: (0,) * len(shape))
    tile = lambda w: pl.BlockSpec((1, tm, w), lambda i, j: (i, j, 0))
    tile_t = pl.BlockSpec((1, md, tm), lambda i, j: (i, 0, j))
    feat_major = jax.ShapeDtypeStruct((b, md, t), BF16)
    in_specs = [
        tile(d),
        pl.BlockSpec((1, 6, d), (lambda i, j: (i, 0, 0)) if mod_row is None
                     else (lambda i, j: (mod_row, 0, 0))),
        const((1, d)), const(wf.shape), const(wts["wk"].shape), const(wts["bg"].shape),
    ]
    args = [h, mod, norm_g, wf, wts["wk"], wts["bg"]]
    out_specs = [tile_t, tile(md), tile_t, pl.BlockSpec((1, n_gate, tm), lambda i, j: (i, 0, j))]
    out_shape = [feat_major, jax.ShapeDtypeStruct((b, t, md), BF16), feat_major,
                 jax.ShapeDtypeStruct((b, n_gate, t), F32)]
    if full:
        cd = wts["cw"].shape[1]
        in_specs += [const(wts["wc"].shape), const(wts["cw"].shape)]
        args += [wts["wc"], wts["cw"]]
        out_specs += [tile(cd), tile_t]
        out_shape += [jax.ShapeDtypeStruct((b, t, cd), BF16), feat_major]
    return pl.pallas_call(
        functools.partial(_in_kernel, row_w=row_w, full=full, kscale=head_dim ** -0.5),
        grid=(b, t // tm),
        in_specs=in_specs,
        out_specs=out_specs,
        out_shape=out_shape,
        compiler_params=_params("parallel", "parallel"),
        name="in_proj",
    )(*args)


def _lane_scan(x, op, ident, reverse):
    n = x.shape[1]
    lane = lax.broadcasted_iota(jnp.int32, x.shape, 1)
    s = 1
    while s < n:
        if reverse:
            shifted = jnp.where(lane < n - s, pltpu.roll(x, n - s, axis=1), ident)
        else:
            shifted = jnp.where(lane >= s, pltpu.roll(x, s, axis=1), ident)
        x = op(x, shifted)
        s *= 2
    return x


def _mlstm_kernel(gc_ref, gl_ref, kc_ref, qc_ref, vc_ref, kl_ref, ql_ref, vl_ref, ogl_ref, gain_ref,
                  *rest, need_ctx):
    if need_ctx:
        ogc_ref, yc_ref, yl_ref, rt_ref, ra_ref, rp_ref, rw_ref, s_ref, upd_ref, st_ref = rest
    else:
        yl_ref, rt_ref, ra_ref, rp_ref, rw_ref, s_ref, upd_ref, st_ref = rest
        ogc_ref = yc_ref = None
    L = CHUNK
    hd = kl_ref.shape[2]
    ncc = kc_ref.shape[1] // L
    ncl = kl_ref.shape[1] // L
    nct = ncc + ncl

    head = pl.program_id(1)
    n_heads = gl_ref.shape[1] // N_GATES

    def gate_rows(gate, hh):
        r = gate * n_heads + hh
        return jnp.concatenate([ref[0, r:r + 1, c * L:(c + 1) * L]
                                for ref, n in ((gc_ref, ncc), (gl_ref, ncl)) for c in range(n)], axis=0)

    @pl.when(head == 0)
    def _():
        for d in range(2):
            rev = d == 1
            last = 0 if rev else L - 1
            for hh in range(n_heads):
                ig = gate_rows(2 * d, hh)
                lf = jax.nn.log_sigmoid(gate_rows(2 * d + 1, hh))
                a = _lane_scan(lf, jnp.add, 0.0, rev)
                r = ig - a
                p = _lane_scan(r, jnp.maximum, -jnp.inf, rev)
                ra_ref[d, hh, 0:nct, :] = a
                rp_ref[d, hh, 0:nct, :] = p
                rw_ref[d, hh, 0:nct, :] = jnp.exp(r - p[:, last:last + 1])
                r_pad = jnp.concatenate([r, jnp.zeros((L - nct, L), F32)], axis=0)
                rt_ref[d, hh] = r_pad.T

    spos = lax.broadcasted_iota(jnp.int32, (L, L), 0)
    tpos = lax.broadcasted_iota(jnp.int32, (L, L), 1)
    ones_row = jnp.where(lax.broadcasted_iota(jnp.int32, (AUG_ROWS, L), 0) == 0, 1.0, 0.0).astype(BF16)
    gain = gain_ref[...]

    ctx_seq = (kc_ref, qc_ref, vc_ref, ogc_ref, yc_ref, ncc, 0, need_ctx)
    lat_seq = (kl_ref, ql_ref, vl_ref, ogl_ref, yl_ref, ncl, ncc, True)

    for k_ref, q_ref, v_ref, _, _, n, g0, want in (ctx_seq, lat_seq):
        for j in range(n):
            g = g0 + j
            k = k_ref[0, pl.ds(j * L, L), :]
            if want:
                s_ref[g] = _dot(k, q_ref[0, :, pl.ds(j * L, L)])
            vaug = jnp.concatenate([v_ref[0, :, pl.ds(j * L, L)], ones_row], axis=0).astype(F32)
            for d in range(2):
                vw = (vaug * rw_ref[d, head, g:g + 1, :]).astype(BF16)
                upd_ref[d, g] = _dot(vw, k)

    m_in = {}
    for d in range(2):
        last = 0 if d == 1 else L - 1
        caug, m = jnp.zeros((hd + AUG_ROWS, hd), F32), jnp.zeros((1, 1), F32)
        for _, _, _, _, _, n, g0, _ in (ctx_seq, lat_seq):
            for i in range(n):
                g = g0 + (n - 1 - i if d == 1 else i)
                st_ref[d, g] = caug.astype(BF16)
                m_in[d, g] = m
                b_tot = ra_ref[d, head, g:g + 1, last:last + 1]
                p_last = rp_ref[d, head, g:g + 1, last:last + 1]
                m_last = jnp.maximum(p_last, m)
                caug = jnp.exp(m - m_last) * caug + jnp.exp(p_last - m_last) * upd_ref[d, g]
                m = b_tot + m_last

    for k_ref, q_ref, v_ref, og_ref, y_ref, n, g0, want in (ctx_seq, lat_seq):
        for j in range(n if want else 0):
            g = g0 + j
            lanes = pl.ds(j * L, L)
            q = q_ref[0, :, lanes]
            vaug = jnp.concatenate([v_ref[0, :, lanes], ones_row], axis=0)
            rhs, scale = [], []
            for d in range(2):
                a_row = ra_ref[d, head, g:g + 1, :]
                p_row = rp_ref[d, head, g:g + 1, :]
                m = m_in[d, g]
                mask = (spos >= tpos) if d == 1 else (spos <= tpos)
                w = jnp.where(mask, jnp.exp(rt_ref[d, head, :, g:g + 1] - p_row), 0.0)
                m_row = jnp.maximum(p_row, m)
                sw = (s_ref[g] * w * jnp.exp(p_row - m_row)).astype(BF16)
                qs = (q.astype(F32) * jnp.exp(m - m_row)).astype(BF16)
                rhs.append((sw, qs))
                scale.append(jnp.exp(-(a_row + m_row)))
            zero = jnp.zeros((hd, L), BF16)
            nd = _dot(jnp.concatenate([vaug, st_ref[0, g], st_ref[1, g]], axis=1),
                      jnp.concatenate([jnp.concatenate([rhs[0][0], rhs[1][0]], axis=1),
                                       jnp.concatenate([rhs[0][1], zero], axis=1),
                                       jnp.concatenate([zero, rhs[1][1]], axis=1)], axis=0))
            hh = None
            for d in range(2):
                part = nd[:, d * L:(d + 1) * L]
                den = jnp.maximum(jnp.abs(part[hd:hd + 1, :]), scale[d])
                hout = part[:hd, :] * (1.0 / den)
                hh = hout if hh is None else hh + hout
            hn = hh * lax.rsqrt(jnp.mean(hh * hh, axis=0, keepdims=True) + EPS)
            y_ref[0, :, lanes] = (hn * gain * og_ref[0, :, lanes].astype(F32)).astype(BF16)


def _mlstm(gates_c, gates_l, kqv_c, kqv_l, og_c, og_l, gain, need_ctx):
    b, tc, md = kqv_c[0].shape
    tl = kqv_l[0].shape[1]
    n_gate = gates_c.shape[1]
    heads = n_gate // N_GATES
    hd = md // heads
    L = CHUNK
    nct = (tc + tl) // L
    assert nct <= L
    nrow = -(-nct // SUBLANES) * SUBLANES
    tok = lambda t: pl.BlockSpec((1, t, hd), lambda i, h: (i, 0, h))
    feat = lambda t: pl.BlockSpec((1, hd, t), lambda i, h: (i, h, 0))
    gate_spec = lambda t: pl.BlockSpec((1, n_gate, t), lambda i, h: (i, 0, 0))
    in_specs = [gate_spec(tc), gate_spec(tl),
                tok(tc), feat(tc), feat(tc), tok(tl), feat(tl), feat(tl), feat(tl),
                pl.BlockSpec((hd, 1), lambda i, h: (h, 0))]
    args = [gates_c, gates_l, *kqv_c, *kqv_l, og_l, gain.reshape(md, 1)]
    out_specs = [feat(tl)]
    out_shape = [jax.ShapeDtypeStruct((b, md, tl), BF16)]
    if need_ctx:
        in_specs.append(feat(tc))
        args.append(og_c)
        out_specs = [feat(tc)] + out_specs
        out_shape = [jax.ShapeDtypeStruct((b, md, tc), BF16)] + out_shape
    res = pl.pallas_call(
        functools.partial(_mlstm_kernel, need_ctx=need_ctx),
        grid=(b, heads),
        in_specs=in_specs,
        out_specs=out_specs,
        out_shape=out_shape,
        scratch_shapes=[
            pltpu.VMEM((2, heads, L, L), F32),
            pltpu.VMEM((2, heads, nrow, L), F32),
            pltpu.VMEM((2, heads, nrow, L), F32),
            pltpu.VMEM((2, heads, nrow, L), F32),
            pltpu.VMEM((nct, L, L), F32),
            pltpu.VMEM((2, nct, hd + AUG_ROWS, hd), F32),
            pltpu.VMEM((2, nct, hd + AUG_ROWS, hd), BF16),
        ],
        compiler_params=_params("parallel", "arbitrary"),
        name="mlstm_scan",
    )(*args)
    return (res[0], res[1]) if need_ctx else (None, res[0])


def _mix_residual(h_ref, yc_ref, ym_ref, mod_ref, woc_ref, wom_ref, rows):
    mix = _dot(yc_ref[0, rows, :], woc_ref[...]) + _dot_tn(ym_ref[0, :, rows], wom_ref[...])
    return h_ref[0, rows, :] + mod_ref[0, 2:3, :] * mix


def _out_dense_kernel(h_ref, yc_ref, ym_ref, mod_ref, g2_ref, woc_ref, wom_ref, w1_ref, w3_ref, w2_ref,
                      *rest, chunks, final):
    if final:
        gf_ref, o_ref = rest
    else:
        (o_ref,) = rest
    h = _mix_residual(h_ref, yc_ref, ym_ref, mod_ref, woc_ref, wom_ref, slice(None))
    u = _norm_mod(h, g2_ref[...], mod_ref[0, 3:4, :], mod_ref[0, 4:5, :]).astype(BF16)
    acc = None
    for c0, c1 in chunks:
        a = _dot(u, w1_ref[:, c0:c1])
        hid = (a * jax.nn.sigmoid(a) * _dot(u, w3_ref[:, c0:c1])).astype(BF16)
        part = _dot(hid, w2_ref[c0:c1, :])
        acc = part if acc is None else acc + part
    h = h + mod_ref[0, 5:6, :] * acc
    if final:
        h = h * lax.rsqrt(jnp.mean(h * h, axis=-1, keepdims=True) + EPS) * gf_ref[...]
    o_ref[0] = h


def _out_moe_kernel(h_ref, yc_ref, ym_ref, mod_ref, g2_ref, woc_ref, wom_ref, router_ref,
                    hm_ref, u_ref, ti_ref, tg_ref, *, n_exp):
    for rows in _sub_tiles(h_ref.shape[1]):
        _out_moe_rows(h_ref, yc_ref, ym_ref, mod_ref, g2_ref, woc_ref, wom_ref, router_ref,
                      hm_ref, u_ref, ti_ref, tg_ref, rows, n_exp)


def _out_moe_rows(h_ref, yc_ref, ym_ref, mod_ref, g2_ref, woc_ref, wom_ref, router_ref,
                  hm_ref, u_ref, ti_ref, tg_ref, rows, n_exp):
    h = _mix_residual(h_ref, yc_ref, ym_ref, mod_ref, woc_ref, wom_ref, rows)
    hm_ref[0, rows, :] = h
    u = _norm_mod(h, g2_ref[...], mod_ref[0, 3:4, :], mod_ref[0, 4:5, :])
    _rows_to_tiles(u_ref, u, rows.start)
    u_hi = u.astype(BF16)
    u_lo = (u - u_hi.astype(F32)).astype(BF16)
    parts = _dot(u_hi, router_ref[...]) + _dot(u_lo, router_ref[...])
    logits = parts[:, :LANES] + parts[:, LANES:]
    n_pad = -(-n_exp // SUBLANES) * SUBLANES
    lt = logits.T[:n_pad, :]
    row = lax.broadcasted_iota(jnp.int32, lt.shape, 0)
    lt = jnp.where(row < n_exp, lt, -jnp.inf)
    v1 = jnp.max(lt, axis=0, keepdims=True)
    i1 = jnp.min(jnp.where(lt == v1, row, n_pad), axis=0, keepdims=True)
    rest = jnp.where(row == i1, -jnp.inf, lt)
    v2 = jnp.max(rest, axis=0, keepdims=True)
    i2 = jnp.min(jnp.where(rest == v2, row, n_pad), axis=0, keepdims=True)
    e2 = jnp.exp(v2 - v1)
    inv = 1.0 / (1.0 + e2)
    ti_ref[:, rows] = jnp.concatenate([i1, i2], axis=0)
    tg_ref[:, rows] = jnp.concatenate([inv, e2 * inv], axis=0)


def _out_common(h, yc, ym, mod, mod_row, norm_g, wts, row_tile):
    b, t, d = h.shape
    tm = min(row_tile, t)
    const = lambda shape: pl.BlockSpec(shape, lambda i, j: (0,) * len(shape))
    tile = lambda w: pl.BlockSpec((1, tm, w), lambda i, j: (i, j, 0))
    in_specs = [
        tile(d), tile(yc.shape[2]), pl.BlockSpec((1, ym.shape[1], tm), lambda i, j: (i, 0, j)),
        pl.BlockSpec((1, 6, d), (lambda i, j: (i, 0, 0)) if mod_row is None
                     else (lambda i, j: (mod_row, 0, 0))),
        const((1, d)), const(wts["woc"].shape), const(wts["wom"].shape),
    ]
    args = [h, yc, ym, mod, norm_g, wts["woc"], wts["wom"]]
    return b, t, d, tm, const, tile, in_specs, args


def _out_dense(h, yc, ym, mod, mod_row, norm_g, wts, ffn, final_g):
    b, t, d, tm, const, tile, in_specs, args = _out_common(h, yc, ym, mod, mod_row, norm_g, wts, FFN_ROW_TILE)
    w1, w3, w2 = ffn
    in_specs += [const(w1.shape), const(w3.shape), const(w2.shape)]
    args += [w1, w3, w2]
    if final_g is not None:
        in_specs.append(const((1, d)))
        args.append(final_g)
    return pl.pallas_call(
        functools.partial(_out_dense_kernel, chunks=_ff_chunks(w1.shape[1]), final=final_g is not None),
        grid=(b, t // tm),
        in_specs=in_specs,
        out_specs=tile(d),
        out_shape=jax.ShapeDtypeStruct((b, t, d), F32),
        compiler_params=_params("parallel", "parallel"),
        name="out_dense_ffn",
    )(*args)


def _out_moe(h, yc, ym, mod, mod_row, norm_g, wts, router_pad, n_exp):
    b, t, d, tm, const, tile, in_specs, args = _out_common(h, yc, ym, mod, mod_row, norm_g, wts, ROW_TILE)
    in_specs.append(const(router_pad.shape))
    args.append(router_pad)
    per = d // LANES
    top2 = pl.BlockSpec((TOP_K, tm), lambda i, j: (0, i * (t // tm) + j))
    return pl.pallas_call(
        functools.partial(_out_moe_kernel, n_exp=n_exp),
        grid=(b, t // tm),
        in_specs=in_specs,
        out_specs=[tile(d), pl.BlockSpec((tm * per, LANES), lambda i, j: (i * (t // tm) + j, 0)),
                   top2, top2],
        out_shape=[jax.ShapeDtypeStruct((b, t, d), F32), jax.ShapeDtypeStruct((b * t * per, LANES), F32),
                   jax.ShapeDtypeStruct((TOP_K, b * t), jnp.int32), jax.ShapeDtypeStruct((TOP_K, b * t), F32)],
        compiler_params=_params("parallel", "parallel"),
        name="out_router",
    )(*args)


def _expert_kernel(be_ref, nu_ref, tok_ref, toknext_ref, gate_ref, u_hbm, w1_ref, w3_ref, w2_ref, y_ref, xbuf, sem,
                   *, chunks, per):
    i = pl.program_id(0)
    n_used = nu_ref[0]
    blk = xbuf.shape[1] // per
    slot = i % 2

    def row_copy(idx_ref, r, to_slot):
        src = pl.multiple_of(idx_ref[0, 0, r], per)
        return pltpu.make_async_copy(u_hbm.at[pl.ds(src, per), :],
                                     xbuf.at[to_slot, pl.ds(r * per, per), :], sem.at[to_slot])

    @pl.when(i == 0)
    def _():
        def body(r, c):
            row_copy(tok_ref, r, 0).start()
            return c
        lax.fori_loop(0, blk, body, 0, unroll=ISSUE_UNROLL)

    @pl.when(i <= n_used)
    def _():
        pltpu.make_async_copy(u_hbm.at[pl.ds(0, blk * per), :], xbuf.at[slot], sem.at[slot]).wait()

    @pl.when(i < n_used)
    def _():
        for r in range(blk):
            row_copy(toknext_ref, r, 1 - slot).start()
        x = _tiles_to_rows(xbuf.at[slot], blk, per).astype(BF16)
        acc = None
        for c0, c1 in chunks:
            a = _dot(x, w1_ref[0, :, c0:c1])
            hid = (a * jax.nn.sigmoid(a) * _dot(x, w3_ref[0, :, c0:c1])).astype(BF16)
            part = _dot(hid, w2_ref[0, c0:c1, :])
            acc = part if acc is None else acc + part
        gate = jnp.concatenate([gate_ref[0], jnp.zeros((SUBLANES - 1, blk), F32)], axis=0).T[:, 0:1]
        _rows_to_tiles(y_ref, acc * gate)

    @pl.when(i >= n_used)
    def _():
        y_ref[...] = jnp.zeros(y_ref.shape, F32)


def _experts(u_tiles, tok_buf, gate_buf, blk_e, n_used, w1, w3, w2):
    d, f = w1.shape[1], w1.shape[2]
    per = d // LANES
    nblk = tok_buf.shape[0] // MOE_BLOCK
    tok3 = (tok_buf * per).reshape(nblk, 1, MOE_BLOCK)
    smem = lambda fn: pl.BlockSpec((1, 1, MOE_BLOCK), fn, memory_space=pltpu.SMEM)
    grid_spec = pltpu.PrefetchScalarGridSpec(
        num_scalar_prefetch=2,
        grid=(nblk,),
        in_specs=[
            smem(lambda i, be, nu: (i, 0, 0)),
            smem(lambda i, be, nu: (jnp.minimum(i + 1, nblk - 1), 0, 0)),
            pl.BlockSpec((1, 1, MOE_BLOCK), lambda i, be, nu: (i, 0, 0)),
            pl.BlockSpec(memory_space=pl.ANY),
            pl.BlockSpec((1, d, f), lambda i, be, nu: (be[i], 0, 0)),
            pl.BlockSpec((1, d, f), lambda i, be, nu: (be[i], 0, 0)),
            pl.BlockSpec((1, f, d), lambda i, be, nu: (be[i], 0, 0)),
        ],
        out_specs=pl.BlockSpec((MOE_BLOCK * per, LANES), lambda i, be, nu: (i, 0)),
        scratch_shapes=[pltpu.VMEM((2, MOE_BLOCK * per, LANES), F32), pltpu.SemaphoreType.DMA((2,))],
    )
    return pl.pallas_call(
        functools.partial(_expert_kernel, chunks=_ff_chunks(f), per=per),
        grid_spec=grid_spec,
        out_shape=jax.ShapeDtypeStruct((nblk * MOE_BLOCK * per, LANES), F32),
        compiler_params=_params("arbitrary"),
        name="moe_experts",
    )(blk_e, n_used, tok3, tok3, gate_buf.reshape(nblk, 1, MOE_BLOCK), u_tiles, w1, w3, w2)


def _combine_kernel(dst_ref, dnext_ref, h_ref, mod_ref, *rest, final):
    if final:
        gf_ref, y_hbm, o_ref, ybuf, sem = rest
    else:
        y_hbm, o_ref, ybuf, sem = rest
    i = pl.program_id(0)
    n = pl.num_programs(0)
    tc, d = h_ref.shape
    per = d // LANES

    def issue(idx_ref, slot):
        for r in range(tc):
            for k in range(TOP_K):
                src = pl.multiple_of(idx_ref[0, 0, k * tc + r], per)
                pltpu.make_async_copy(y_hbm.at[pl.ds(src, per), :],
                                      ybuf.at[slot, k, pl.ds(r * per, per), :],
                                      sem.at[slot]).start(priority=k)

    @pl.when(i == 0)
    def _():
        issue(dst_ref, 0)

    slot = i % 2

    @pl.when(i + 1 < n)
    def _():
        issue(dnext_ref, 1 - slot)

    for k in range(TOP_K):
        pltpu.make_async_copy(y_hbm.at[pl.ds(0, tc * per), :], ybuf.at[slot, k], sem.at[slot]).wait()

    y = _tiles_to_rows(ybuf.at[slot, 0], tc, per) + _tiles_to_rows(ybuf.at[slot, 1], tc, per)
    h = h_ref[...] + mod_ref[0, 5:6, :] * y
    if final:
        h = h * lax.rsqrt(jnp.mean(h * h, axis=-1, keepdims=True) + EPS) * gf_ref[...]
    o_ref[...] = h


def _combine(h_flat, y, dest, mod, mod_row, tokens_per_batch, final_g):
    n_tok, d = h_flat.shape
    tc = COMBINE_TILE
    steps = n_tok // tc
    per_b = tokens_per_batch // tc
    per = d // LANES
    dst2 = (dest * per).reshape(TOP_K, steps, tc).transpose(1, 0, 2).reshape(steps, 1, TOP_K * tc)
    smem = lambda fn: pl.BlockSpec((1, 1, TOP_K * tc), fn, memory_space=pltpu.SMEM)
    in_specs = [
        smem(lambda i: (i, 0, 0)),
        smem(lambda i: (jnp.minimum(i + 1, steps - 1), 0, 0)),
        pl.BlockSpec((tc, d), lambda i: (i, 0)),
        pl.BlockSpec((1, 6, d), (lambda i: (i // per_b, 0, 0)) if mod_row is None
                     else (lambda i: (mod_row, 0, 0))),
    ]
    args = [dst2, dst2, h_flat, mod]
    if final_g is not None:
        in_specs.append(pl.BlockSpec((1, d), lambda i: (0, 0)))
        args.append(final_g)
    in_specs.append(pl.BlockSpec(memory_space=pl.ANY))
    args.append(y)
    return pl.pallas_call(
        functools.partial(_combine_kernel, final=final_g is not None),
        grid=(steps,),
        in_specs=in_specs,
        out_specs=pl.BlockSpec((tc, d), lambda i: (i, 0)),
        out_shape=jax.ShapeDtypeStruct((n_tok, d), F32),
        scratch_shapes=[pltpu.VMEM((2, TOP_K, tc * per, LANES), F32), pltpu.SemaphoreType.DMA((2,))],
        compiler_params=_params("arbitrary"),
        name="moe_combine",
    )(*args)


def _routing(top_i, top_g, n_exp):
    n_tok = top_i.shape[1]
    e = top_i.reshape(-1)
    a = e.shape[0]
    experts = jnp.arange(n_exp, dtype=jnp.int32)
    onehot = (e[:, None] == experts[None, :]).astype(jnp.int32)
    csum = jnp.cumsum(onehot, axis=0)
    rank = jnp.sum(onehot * csum, axis=1) - 1
    counts = csum[-1]
    padded = (counts + MOE_BLOCK - 1) // MOE_BLOCK * MOE_BLOCK
    pend = jnp.cumsum(padded)
    pstart = pend - padded
    start = jnp.cumsum(counts) - counts
    dest = pstart[e] + rank
    cap = -(-a // MOE_BLOCK) * MOE_BLOCK + n_exp * MOE_BLOCK
    nblk = cap // MOE_BLOCK
    blk_first = jnp.arange(nblk, dtype=jnp.int32) * MOE_BLOCK
    blk_e = jnp.minimum(jnp.sum((pend[None, :] <= blk_first[:, None]).astype(jnp.int32), axis=1), n_exp - 1)
    n_used = (pend[-1] // MOE_BLOCK).astype(jnp.int32).reshape(1)
    order = jnp.argsort(e).astype(jnp.int32)
    row = jnp.arange(cap, dtype=jnp.int32)
    row_e = jnp.repeat(blk_e, MOE_BLOCK)
    within = row - pstart[row_e]
    valid = (within < counts[row_e]) & (row < pend[-1])
    asg = order[jnp.clip(start[row_e] + within, 0, a - 1)]
    tok_buf = jnp.where(valid, asg % n_tok, 0)
    gate_buf = jnp.where(valid, top_g.reshape(-1)[asg], 0.0)
    return (dest.astype(jnp.int32).reshape(TOP_K, n_tok), tok_buf.astype(jnp.int32), gate_buf,
            blk_e.astype(jnp.int32), n_used)


def _moe_mixer(h_mid, u, top_i, top_g, mod, mod_row, moe_w, n_exp, final_g):
    b, t, d = h_mid.shape
    n_tok = b * t
    dest, tok_buf, gate_buf, blk_e, n_used = _routing(top_i, top_g, n_exp)
    y = _experts(u, tok_buf, gate_buf, blk_e, n_used, *moe_w)
    out = _combine(h_mid.reshape(n_tok, d), y, dest, mod, mod_row, t, final_g)
    return out.reshape(b, t, d)


def _layer_weights(w_in_l, conv_w_l, b_gates_l, w_out_l):
    cd = conv_w_l.shape[1]
    n_gate = b_gates_l.size
    md = (w_in_l.shape[1] - 3 * cd - n_gate) // 4
    q0 = 3 * cd
    g0 = q0 + 3 * md
    wq, wv = w_in_l[:, q0:q0 + md], w_in_l[:, q0 + 2 * md:g0]
    wg, wo = w_in_l[:, g0:g0 + n_gate], w_in_l[:, g0 + n_gate:]
    return {
        "wc": w_in_l[:, :q0].astype(BF16),
        "wf_full": jnp.concatenate([wq, wv, wo, wg], axis=1).T.astype(BF16),
        "wf_state": jnp.concatenate([wq, wv, wg], axis=1).T.astype(BF16),
        "wk": w_in_l[:, q0 + md:q0 + 2 * md].astype(BF16),
        "bg": b_gates_l.reshape(n_gate, 1),
        "cw": conv_w_l,
        "woc": w_out_l[:cd].astype(BF16),
        "wom": w_out_l[cd:].astype(BF16),
    }


def kernel(x, c, ctx, c_ctx, norm1_g, norm2_g, w_ada, b_ada, w_in, conv_w, b_gates, mlstm_norm_g, w_out,
           ffn_w1, ffn_w3, ffn_w2, moe_router, moe_w1, moe_w3, moe_w2, final_norm_g):
    b, t, d = x.shape
    tc = ctx.shape[1]
    depth = w_in.shape[0]
    n_exp = moe_router.shape[-1]
    assert t % GRID_W == 0 and t % CHUNK == 0 and tc % CHUNK == 0 and d % (SUBLANES * LANES) == 0

    ctx_row = b
    n_rows = -(-(b + 1) // SUBLANES) * SUBLANES
    c_all = jnp.concatenate([c, c_ctx[None, :], jnp.zeros((n_rows - b - 1, d), F32)], axis=0)
    mod_all = _ada(c_all, w_ada, b_ada).reshape(depth, n_rows, 6, d)
    final_g = final_norm_g.reshape(1, d)

    h, hc = x, ctx
    for layer in range(depth):
        last = layer == depth - 1
        j = layer // 2
        mod = mod_all[layer]
        wts = _layer_weights(w_in[layer], conv_w[layer], b_gates[layer], w_out[layer])
        n1 = norm1_g[layer].reshape(1, d)
        n2 = norm2_g[layer].reshape(1, d)

        ql, kl, vl, gl, ycl, ogl = _in_proj(h, mod, None, n1, wts, GRID_W, True)
        res_c = _in_proj(hc, mod, ctx_row, n1, wts, tc, not last)
        qc, kc, vc, gc = res_c[:4]
        ogc = None if last else res_c[5]
        ymc, yml = _mlstm(gc, gl, (kc, qc, vc), (kl, ql, vl), ogc, ogl,
                          mlstm_norm_g[layer].reshape(1, -1), not last)

        def mixer(hh, yc_, ym_, mod_row, fin):
            if layer % 2 == 0:
                ffn = (ffn_w1[j].astype(BF16), ffn_w3[j].astype(BF16), ffn_w2[j].astype(BF16))
                return _out_dense(hh, yc_, ym_, mod, mod_row, n2, wts, ffn, fin)
            r_full = jnp.pad(moe_router[j], ((0, 0), (0, LANES - n_exp)))
            r_hi = r_full.astype(BF16)
            r_lo = (r_full - r_hi.astype(F32)).astype(BF16)
            router_pad = jnp.concatenate([r_hi, r_lo], axis=1)
            h_mid, u2, top_i, top_g = _out_moe(hh, yc_, ym_, mod, mod_row, n2, wts, router_pad, n_exp)
            moe_w = (moe_w1[j].astype(BF16), moe_w3[j].astype(BF16), moe_w2[j].astype(BF16))
            return _moe_mixer(h_mid, u2, top_i, top_g, mod, mod_row, moe_w, n_exp, fin)

        h = mixer(h, ycl, yml, None, final_g if last else None)
        if not last:
            hc = mixer(hc, res_c[4], ymc, ctx_row, None)
    return h
```
